```python
import jax, jax.numpy as jnp
from jax import lax
import numpy as np

D_MODEL = 2048
BATCH = 2
SEQ = 4096
DEPTH = 4
DEC_BATCH = 32
DEC_SEQ = 4
PAST_LEN = 16384
PAGE_SIZE = 128

N_MIXERS = 3
CHUNK = 128
D_SGU = D_MODEL
SGU_GROUPS = 8
SGU_GROUP_DIM = D_SGU // SGU_GROUPS
HEAD_DIM = 64
N_HEADS = D_MODEL // HEAD_DIM
N_KV_HEADS = N_HEADS // 8
GQA_GROUP = N_HEADS // N_KV_HEADS
WINDOW = 128
D_CONV = D_MODEL
CONV_WIDTH = 3
D_FF = 5632
N_EXPERTS = 8
TOP_K = 2
D_EXPERT = 5632
MOE_BLOCK = 128
LN_EPS = 1e-5
DEEPNORM_ALPHA = (2.0 * DEPTH) ** 0.25
DEEPNORM_BETA = (8.0 * DEPTH) ** -0.25
NEG_INF = -1e30

N_A_LAYERS = len(range(0, DEPTH, N_MIXERS))
N_B_LAYERS = len(range(1, DEPTH, N_MIXERS))
N_C_LAYERS = len(range(2, DEPTH, N_MIXERS))
N_DENSE_LAYERS = len(range(0, DEPTH, 2))
N_MOE_LAYERS = len(range(1, DEPTH, 2))

kernel_name = 'hybrid_sgu_swa_shortconv_decoder_step'


def layer_norm(x, g, b):
    xf = x.astype(jnp.float32)
    mu = jnp.mean(xf, axis=-1, keepdims=True)
    var = jnp.mean(jnp.square(xf - mu), axis=-1, keepdims=True)
    return ((xf - mu) * lax.rsqrt(var + LN_EPS) * g + b).astype(x.dtype)


def swiglu(x, w_gu, w_down):
    gu = x @ w_gu
    d = w_down.shape[0]
    return (jax.nn.silu(gu[..., :d]) * gu[..., d:]) @ w_down


def moe_swiglu(x, w_router, w_gu, w_down):
    n, d = x.shape
    logits = jnp.einsum('nd,de->ne', x, w_router).astype(jnp.float32)
    top_logit, top_idx = lax.top_k(logits, TOP_K)
    gates = jax.nn.softmax(top_logit, axis=-1)
    n_rows = n * TOP_K
    flat_e = top_idx.reshape(n_rows).astype(jnp.int32)
    flat_tok = jnp.repeat(jnp.arange(n, dtype=jnp.int32), TOP_K)
    order = jnp.argsort(flat_e)
    e_sorted = flat_e[order]
    counts = jnp.zeros((N_EXPERTS,), jnp.int32).at[flat_e].add(1)
    padded = (counts + MOE_BLOCK - 1) // MOE_BLOCK * MOE_BLOCK
    grp_start = jnp.cumsum(counts) - counts
    pad_end = jnp.cumsum(padded)
    pad_start = pad_end - padded
    dest = pad_start[e_sorted] + jnp.arange(n_rows, dtype=jnp.int32) - grp_start[e_sorted]
    n_blocks = -(-n_rows // MOE_BLOCK) + N_EXPERTS
    n_slots = n_blocks * MOE_BLOCK
    slot_tok = jnp.full((n_slots,), n, jnp.int32).at[dest].set(flat_tok[order])
    slot_gate = jnp.zeros((n_slots,), jnp.float32).at[dest].set(gates.reshape(n_rows)[order])
    block_start = jnp.arange(n_blocks, dtype=jnp.int32) * MOE_BLOCK
    block_e = jnp.minimum(jnp.searchsorted(pad_end, block_start, side='right'), N_EXPERTS - 1).astype(jnp.int32)
    x_pad = jnp.concatenate([x, jnp.zeros((1, d), x.dtype)], axis=0)

    def expert_block(args):
        tok, e = args
        return swiglu(x_pad[tok], w_gu[e], w_down[e])

    out = lax.map(expert_block, (slot_tok.reshape(n_blocks, MOE_BLOCK), block_e))
    y = jnp.zeros((n + 1, d), jnp.float32).at[slot_tok].add(
        out.reshape(n_slots, d).astype(jnp.float32) * slot_gate[:, None])
    return y[:n].astype(x.dtype)


def sgu_project(x, w_in, ln_g, ln_b):
    z = jax.nn.gelu(x @ w_in, approximate=False)
    u = z[..., :D_SGU]
    v = layer_norm(z[..., D_SGU:], ln_g, ln_b)
    return u, v


def causal_spatial(w_s, n_pos):
    mask = jnp.tril(jnp.ones((CHUNK, CHUNK), dtype=bool))
    return jnp.where(mask, w_s, jnp.zeros_like(w_s))[:, :n_pos, :n_pos]


def sgu_prompt(x, w_in, ln_g, ln_b, w_s, b_s, w_out):
    b, s, _ = x.shape
    u, v = sgu_project(x, w_in, ln_g, ln_b)
    vc = v.reshape(b, s // CHUNK, CHUNK, SGU_GROUPS, SGU_GROUP_DIM)
    mixed = jnp.einsum('gij,bcjgd->bcigd', causal_spatial(w_s, CHUNK), vc) + b_s.T[:, :, None]
    return (u * mixed.reshape(b, s, D_SGU)) @ w_out


def sgu_sample(x, w_in, ln_g, ln_b, w_s, b_s, w_out):
    db, ds, _ = x.shape
    u, v = sgu_project(x, w_in, ln_g, ln_b)
    vc = v.reshape(db, ds, SGU_GROUPS, SGU_GROUP_DIM)
    mixed = jnp.einsum('gij,bjgd->bigd', causal_spatial(w_s, ds), vc) + b_s[:, :ds].T[:, :, None]
    return (u * mixed.reshape(db, ds, D_SGU)) @ w_out, v


def alibi_slopes():
    h = jnp.arange(1, N_HEADS + 1, dtype=jnp.float32)
    return jnp.exp2(-8.0 * h / N_HEADS).reshape(N_KV_HEADS, GQA_GROUP)


def split_qkv(x, w_qkv):
    qkv = x @ w_qkv
    lead = x.shape[:-1]
    nq, nk = N_HEADS * HEAD_DIM, N_KV_HEADS * HEAD_DIM
    q = qkv[..., :nq].reshape(lead + (N_KV_HEADS, GQA_GROUP, HEAD_DIM))
    k = qkv[..., nq:nq + nk].reshape(lead + (N_KV_HEADS, HEAD_DIM))
    v = qkv[..., nq + nk:].reshape(lead + (N_KV_HEADS, HEAD_DIM))
    return q, k, v


def sink_softmax(scores, delta, valid, sinks):
    s = scores - alibi_slopes()[:, :, None, None] * delta
    s = jnp.where(valid, s, NEG_INF)
    sink = jnp.broadcast_to(sinks.astype(jnp.float32).reshape(N_KV_HEADS, GQA_GROUP, 1, 1), s.shape[:-1] + (1,))
    return jax.nn.softmax(jnp.concatenate([s, sink], axis=-1), axis=-1)[..., :-1]


def swa_prompt(x, w_qkv, sinks, w_o):
    b, s, _ = x.shape
    nb = s // WINDOW
    q, k, v = split_qkv(x, w_qkv)
    qb = q.reshape(b, nb, WINDOW, N_KV_HEADS, GQA_GROUP, HEAD_DIM)
    kb = k.reshape(b, nb, WINDOW, N_KV_HEADS, HEAD_DIM)
    vb = v.reshape(b, nb, WINDOW, N_KV_HEADS, HEAD_DIM)
    kk = jnp.concatenate([jnp.concatenate([jnp.zeros_like(kb[:, :1]), kb[:, :-1]], axis=1), kb], axis=2)
    vv = jnp.concatenate([jnp.concatenate([jnp.zeros_like(vb[:, :1]), vb[:, :-1]], axis=1), vb], axis=2)
    scores = jnp.einsum('bntkgd,bnskd->bnkgts', qb, kk).astype(jnp.float32) * (HEAD_DIM ** -0.5)
    qi = jnp.arange(WINDOW)[:, None]
    kj = jnp.arange(2 * WINDOW)[None, :]
    delta = WINDOW + qi - kj
    band = (delta >= 0) & (delta <= WINDOW)
    has_prev = (jnp.arange(nb) > 0)[:, None, None] | (kj >= WINDOW)[None]
    valid = (band[None] & has_prev)[:, None, None]
    p = sink_softmax(scores, delta.astype(jnp.float32), valid, sinks)
    o = jnp.einsum('bnkgts,bnskd->bntkgd', p.astype(vv.dtype), vv).reshape(b, s, N_HEADS * HEAD_DIM)
    return o @ w_o, k[:, -WINDOW:], v[:, -WINDOW:]


def swa_sample(x, k_buf, v_buf, w_qkv, sinks, w_o):
    db, ds, _ = x.shape
    q, k, v = split_qkv(x, w_qkv)
    n_buf = k_buf.shape[1]
    kk = jnp.concatenate([k_buf.astype(k.dtype), k], axis=1)
    vv = jnp.concatenate([v_buf.astype(v.dtype), v], axis=1)
    scores = jnp.einsum('bqkgd,bskd->bkgqs', q, kk).astype(jnp.float32) * (HEAD_DIM ** -0.5)
    delta = n_buf + jnp.arange(ds)[:, None] - jnp.arange(n_buf + ds)[None, :]
    valid = (delta >= 0) & (delta <= WINDOW)
    p = sink_softmax(scores, delta.astype(jnp.float32), valid, sinks)
    o = jnp.einsum('bkgqs,bskd->bqkgd', p.astype(vv.dtype), vv).reshape(db, ds, N_HEADS * HEAD_DIM)
    return o @ w_o, kk[:, -n_buf:], vv[:, -n_buf:]


def conv_mixer(x, past, w_in, w_conv, w_out):
    s = x.shape[1]
    proj = x @ w_in
    gate_b = proj[..., :D_CONV]
    gate_c = proj[..., D_CONV:2 * D_CONV]
    h = proj[..., 2 * D_CONV:]
    z = gate_c * h
    zp = jnp.concatenate([past.astype(z.dtype), z], axis=1)
    conv = w_conv[0] * zp[:, 0:s]
    for tap in range(1, CONV_WIDTH):
        conv = conv + w_conv[tap] * zp[:, tap:tap + s]
    return (gate_b * conv) @ w_out, zp[:, -(CONV_WIDTH - 1):]


def setup_inputs(seed: int = 0) -> dict:
    key = jax.random.key(seed)
    key_arr = jax.random.split(key, 40)
    keys = [key_arr[i] for i in range(40)]

    def nrm(shape, scale):
        return jax.random.normal(keys.pop(), shape, jnp.float32) * scale

    d = D_MODEL
    win_buf = min(WINDOW, PAST_LEN)
    b_w_qkv = jnp.concatenate([
        nrm((N_B_LAYERS, d, N_HEADS * HEAD_DIM), d ** -0.5),
        nrm((N_B_LAYERS, d, N_KV_HEADS * HEAD_DIM), d ** -0.5),
        nrm((N_B_LAYERS, d, N_KV_HEADS * HEAD_DIM), d ** -0.5 * DEEPNORM_BETA)], axis=-1)
    return {
        'x_prompt': nrm((BATCH, SEQ, d), 1.0),
        'x_sample': nrm((DEC_BATCH, DEC_SEQ, d), 1.0),
        'state_swa_k': nrm((N_B_LAYERS, DEC_BATCH, win_buf, N_KV_HEADS, HEAD_DIM), 1.0),
        'state_swa_v': nrm((N_B_LAYERS, DEC_BATCH, win_buf, N_KV_HEADS, HEAD_DIM), 1.0),
        'state_conv': nrm((N_C_LAYERS, DEC_BATCH, CONV_WIDTH - 1, D_CONV), 1.0),
        'ln1_g': 1.0 + nrm((DEPTH, d), 0.02),
        'ln1_b': nrm((DEPTH, d), 0.02),
        'ln2_g': 1.0 + nrm((DEPTH, d), 0.02),
        'ln2_b': nrm((DEPTH, d), 0.02),
        'a_w_in': nrm((N_A_LAYERS, d, 2 * D_SGU), d ** -0.5),
        'a_ln_g': 1.0 + nrm((N_A_LAYERS, D_SGU), 0.02),
        'a_ln_b': nrm((N_A_LAYERS, D_SGU), 0.02),
        'a_w_s': nrm((N_A_LAYERS, SGU_GROUPS, CHUNK, CHUNK), CHUNK ** -0.5),
        'a_b_s': 1.0 + nrm((N_A_LAYERS, SGU_GROUPS, CHUNK), 0.1),
        'a_w_out': nrm((N_A_LAYERS, D_SGU, d), D_SGU ** -0.5 * DEEPNORM_BETA),
        'b_w_qkv': b_w_qkv,
        'b_sinks': nrm((N_B_LAYERS, N_HEADS), 0.5),
        'b_w_o': nrm((N_B_LAYERS, N_HEADS * HEAD_DIM, d), (N_HEADS * HEAD_DIM) ** -0.5 * DEEPNORM_BETA),
        'c_w_in': nrm((N_C_LAYERS, d, 3 * D_CONV), d ** -0.5),
        'c_w_conv': nrm((N_C_LAYERS, CONV_WIDTH, D_CONV), CONV_WIDTH ** -0.5),
        'c_w_out': nrm((N_C_LAYERS, D_CONV, d), D_CONV ** -0.5 * DEEPNORM_BETA),
        'f_w_gu': nrm((N_DENSE_LAYERS, d, 2 * D_FF), d ** -0.5),
        'f_w_down': nrm((N_DENSE_LAYERS, D_FF, d), D_FF ** -0.5 * DEEPNORM_BETA),
        'm_w_router': nrm((N_MOE_LAYERS, d, N_EXPERTS), d ** -0.5),
        'm_w_gu': nrm((N_MOE_LAYERS, N_EXPERTS, d, 2 * D_EXPERT), d ** -0.5),
        'm_w_down': nrm((N_MOE_LAYERS, N_EXPERTS, D_EXPERT, d), D_EXPERT ** -0.5 * DEEPNORM_BETA),
    }


def reference(x_prompt, x_sample, state_swa_k, state_swa_v, state_conv,
              ln1_g, ln1_b, ln2_g, ln2_b,
              a_w_in, a_ln_g, a_ln_b, a_w_s, a_b_s, a_w_out,
              b_w_qkv, b_sinks, b_w_o,
              c_w_in, c_w_conv, c_w_out,
              f_w_gu, f_w_down,
              m_w_router, m_w_gu, m_w_down):
    xp, xs = x_prompt, x_sample
    n_prompt = xp.shape[0] * xp.shape[1]
    sgu_v_s = []
    swa_k_p, swa_v_p, swa_k_s, swa_v_s = [], [], [], []
    conv_p, conv_s = [], []
    for i in range(DEPTH):
        kind, j = i % N_MIXERS, i // N_MIXERS
        if kind == 0:
            mp = sgu_prompt(xp, a_w_in[j], a_ln_g[j], a_ln_b[j], a_w_s[j], a_b_s[j], a_w_out[j])
            ms, v_new = sgu_sample(xs, a_w_in[j], a_ln_g[j], a_ln_b[j], a_w_s[j], a_b_s[j], a_w_out[j])
            sgu_v_s.append(v_new)
        elif kind == 1:
            mp, kp, vp = swa_prompt(xp, b_w_qkv[j], b_sinks[j], b_w_o[j])
            ms, ks_new, vs_new = swa_sample(xs, state_swa_k[j], state_swa_v[j], b_w_qkv[j], b_sinks[j], b_w_o[j])
            swa_k_p.append(kp)
            swa_v_p.append(vp)
            swa_k_s.append(ks_new)
            swa_v_s.append(vs_new)
        else:
            zero_past = jnp.zeros((xp.shape[0], CONV_WIDTH - 1, D_CONV), xp.dtype)
            mp, cp = conv_mixer(xp, zero_past, c_w_in[j], c_w_conv[j], c_w_out[j])
            ms, cs = conv_mixer(xs, state_conv[j], c_w_in[j], c_w_conv[j], c_w_out[j])
            conv_p.append(cp)
            conv_s.append(cs)
        xp = layer_norm(DEEPNORM_ALPHA * xp + mp, ln1_g[i], ln1_b[i])
        xs = layer_norm(DEEPNORM_ALPHA * xs + ms, ln1_g[i], ln1_b[i])
        flat = jnp.concatenate([xp.reshape(-1, D_MODEL), xs.reshape(-1, D_MODEL)], axis=0)
        if i % 2 == 0:
            f = swiglu(flat, f_w_gu[i // 2], f_w_down[i // 2])
        else:
            f = moe_swiglu(flat, m_w_router[i // 2], m_w_gu[i // 2], m_w_down[i // 2])
        flat = layer_norm(DEEPNORM_ALPHA * flat + f, ln2_g[i], ln2_b[i])
        xp = flat[:n_prompt].reshape(xp.shape)
        xs = flat[n_prompt:].reshape(xs.shape)
    return (xp, xs, jnp.stack(sgu_v_s), jnp.stack(swa_k_p), jnp.stack(swa_v_p),
            jnp.stack(swa_k_s), jnp.stack(swa_v_s), jnp.stack(conv_p), jnp.stack(conv_s))
```

```python
import functools

import numpy as np
import jax
import jax.numpy as jnp
from jax import lax
from jax.experimental import pallas as pl
from jax.experimental.pallas import tpu as pltpu

D_MODEL = 2048
BATCH = 2
SEQ = 4096
DEPTH = 4
DEC_BATCH = 32
DEC_SEQ = 4
N_MIXERS = 3
CHUNK = 128
SGU_GROUPS = 8
SGU_GROUP_DIM = D_MODEL // SGU_GROUPS
HEAD_DIM = 64
N_HEADS = D_MODEL // HEAD_DIM
N_KV_HEADS = N_HEADS // 8
GQA_GROUP = N_HEADS // N_KV_HEADS
WINDOW = 128
CONV_WIDTH = 3
D_FF = 5632
N_EXPERTS = 8
LN_EPS = 1e-5
DEEPNORM_ALPHA = (2.0 * DEPTH) ** 0.25
NEG_INF = -1e30

N_PROMPT = BATCH * SEQ
N_SAMPLE = DEC_BATCH * DEC_SEQ
N_TOK = N_PROMPT + N_SAMPLE
BM = 640
N_TILES = N_TOK // BM
SGU_BM = 128
SGU_TILES = N_TOK // SGU_BM
CHUNKS_PER_TILE = SGU_BM // CHUNK
MOE_BM = 256
MOE_TILES = (2 * N_TOK) // MOE_BM + N_EXPERTS
MOE_SLOTS = MOE_TILES * MOE_BM
MOE_BN = 512

F32 = jnp.float32
BF16 = jnp.bfloat16
MIB = 1024 * 1024


def _params(semantics, vmem_mib):
    return pltpu.CompilerParams(dimension_semantics=semantics, vmem_limit_bytes=vmem_mib * MIB)


def _layer_norm(y, g, b):
    mu = jnp.mean(y, axis=-1, keepdims=True)
    d = y - mu
    var = jnp.mean(d * d, axis=-1, keepdims=True)
    return d * lax.rsqrt(var + LN_EPS) * g + b


def _gelu(x):
    return 0.5 * x * (1.0 + lax.erf(x * np.float32(np.sqrt(0.5))))


def _dot(a, b):
    return jnp.dot(a, b, preferred_element_type=F32)


def _mm_kernel(x_ref, w_ref, o_ref):
    o_ref[...] = _dot(x_ref[...], w_ref[...]).astype(o_ref.dtype)


def _mm(x, w, bn, out_dtype, name):
    m, k = x.shape
    n = w.shape[1]
    return pl.pallas_call(
        _mm_kernel,
        grid=(n // bn, m // BM),
        in_specs=[pl.BlockSpec((BM, k), lambda j, i: (i, 0)),
                  pl.BlockSpec((k, bn), lambda j, i: (0, j))],
        out_specs=pl.BlockSpec((BM, bn), lambda j, i: (i, j)),
        out_shape=jax.ShapeDtypeStruct((m, n), out_dtype),
        compiler_params=_params(("arbitrary", "arbitrary"), 48),
        name=name,
    )(x, w)


def _mm_ln_kernel(x_ref, w_ref, r_ref, g_ref, b_ref, of_ref, ob_ref):
    f = _dot(x_ref[...], w_ref[...])
    y = _layer_norm(DEEPNORM_ALPHA * r_ref[...] + f, g_ref[...], b_ref[...])
    of_ref[...] = y
    ob_ref[...] = y.astype(BF16)


def _mm_ln(x, w, resid, g, b, bm, name):
    m, k = x.shape
    row = lambda i: (i, 0)
    const = lambda i: (0, 0)
    return pl.pallas_call(
        _mm_ln_kernel,
        grid=(m // bm,),
        in_specs=[pl.BlockSpec((bm, k), row),
                  pl.BlockSpec((k, D_MODEL), const),
                  pl.BlockSpec((bm, D_MODEL), row),
                  pl.BlockSpec((1, D_MODEL), const),
                  pl.BlockSpec((1, D_MODEL), const)],
        out_specs=[pl.BlockSpec((bm, D_MODEL), row), pl.BlockSpec((bm, D_MODEL), row)],
        out_shape=[jax.ShapeDtypeStruct((m, D_MODEL), F32), jax.ShapeDtypeStruct((m, D_MODEL), BF16)],
        compiler_params=_params(("arbitrary",), 56),
        name=name,
    )(x, w, resid, g, b)


def _sgu_in_kernel(x_ref, w_ref, g_ref, b_ref, o_ref):
    z = _gelu(_dot(x_ref[...], w_ref[...]))

    @pl.when(pl.program_id(0) == 0)
    def _():
        o_ref[...] = z

    @pl.when(pl.program_id(0) == 1)
    def _():
        o_ref[...] = _layer_norm(z, g_ref[...], b_ref[...])


def _sgu_in(xb, w_in, ln_g, ln_b):
    const = lambda j, i: (0, 0)
    return pl.pallas_call(
        _sgu_in_kernel,
        grid=(2, N_TILES),
        in_specs=[pl.BlockSpec((BM, D_MODEL), lambda j, i: (i, 0)),
                  pl.BlockSpec((D_MODEL, D_MODEL), lambda j, i: (0, j)),
                  pl.BlockSpec((1, D_MODEL), const),
                  pl.BlockSpec((1, D_MODEL), const)],
        out_specs=pl.BlockSpec((BM, D_MODEL), lambda j, i: (i, j)),
        out_shape=jax.ShapeDtypeStruct((N_TOK, 2 * D_MODEL), F32),
        compiler_params=_params(("arbitrary", "arbitrary"), 56),
        name="sgu_in",
    )(xb, w_in, ln_g, ln_b)


def _sgu_out_kernel(u_ref, v_ref, wp_ref, ws_ref, bp_ref, bs_ref, wo_ref, r_ref, g_ref, b_ref,
                    of_ref, ob_ref, gated_ref):
    is_sample_tile = pl.program_id(0) == SGU_TILES - 1
    row = lax.broadcasted_iota(jnp.int32, (CHUNK, CHUNK), 0)
    col = lax.broadcasted_iota(jnp.int32, (CHUNK, CHUNK), 1)
    causal = row >= col
    for grp in range(SGU_GROUPS):
        cols = slice(grp * SGU_GROUP_DIM, (grp + 1) * SGU_GROUP_DIM)
        w_prompt = jnp.where(causal, wp_ref[grp], 0.0)
        bias_prompt = bp_ref[:, grp:grp + 1]
        w_last = jnp.where(is_sample_tile, ws_ref[grp], w_prompt).astype(BF16)
        bias_last = jnp.where(is_sample_tile, bs_ref[:, grp:grp + 1], bias_prompt)
        w_prompt = w_prompt.astype(BF16)
        for c in range(CHUNKS_PER_TILE):
            rows = slice(c * CHUNK, (c + 1) * CHUNK)
            last = c == CHUNKS_PER_TILE - 1
            mixed = _dot(w_last if last else w_prompt, v_ref[rows, cols].astype(BF16))
            mixed = mixed + (bias_last if last else bias_prompt)
            gated_ref[rows, cols] = (u_ref[rows, cols] * mixed).astype(BF16)
    f = _dot(gated_ref[...], wo_ref[...])
    y = _layer_norm(DEEPNORM_ALPHA * r_ref[...] + f, g_ref[...], b_ref[...])
    of_ref[...] = y
    ob_ref[...] = y.astype(BF16)


def _sgu_out(z2, w_s, w_s_sample, b_t, b_t_sample, w_out, resid, g, b):
    row = lambda i: (i, 0)
    const2 = lambda i: (0, 0)
    const3 = lambda i: (0, 0, 0)
    return pl.pallas_call(
        _sgu_out_kernel,
        grid=(SGU_TILES,),
        in_specs=[pl.BlockSpec((SGU_BM, D_MODEL), lambda i: (i, 0)),
                  pl.BlockSpec((SGU_BM, D_MODEL), lambda i: (i, 1)),
                  pl.BlockSpec((SGU_GROUPS, CHUNK, CHUNK), const3),
                  pl.BlockSpec((SGU_GROUPS, CHUNK, CHUNK), const3),
                  pl.BlockSpec((CHUNK, SGU_GROUPS), const2),
                  pl.BlockSpec((CHUNK, SGU_GROUPS), const2),
                  pl.BlockSpec((D_MODEL, D_MODEL), const2),
                  pl.BlockSpec((SGU_BM, D_MODEL), row),
                  pl.BlockSpec((1, D_MODEL), const2),
                  pl.BlockSpec((1, D_MODEL), const2)],
        out_specs=[pl.BlockSpec((SGU_BM, D_MODEL), row), pl.BlockSpec((SGU_BM, D_MODEL), row)],
        out_shape=[jax.ShapeDtypeStruct((N_TOK, D_MODEL), F32), jax.ShapeDtypeStruct((N_TOK, D_MODEL), BF16)],
        scratch_shapes=[pltpu.VMEM((SGU_BM, D_MODEL), BF16)],
        compiler_params=_params(("arbitrary",), 40),
        name="sgu_out",
    )(z2, z2, w_s, w_s_sample, b_t, b_t_sample, w_out, resid, g, b)


def _gu_kernel(x_ref, wg_ref, wu_ref, o_ref):
    x = x_ref[...]
    gate = _dot(x, wg_ref[...])
    up = _dot(x, wu_ref[...])
    o_ref[...] = (jax.nn.silu(gate) * up).astype(o_ref.dtype)


def _dense_gu(xb, w_gu):
    bn = 1408
    nt = D_FF // bn
    return pl.pallas_call(
        _gu_kernel,
        grid=(nt, N_TILES),
        in_specs=[pl.BlockSpec((BM, D_MODEL), lambda j, i: (i, 0)),
                  pl.BlockSpec((D_MODEL, bn), lambda j, i: (0, j)),
                  pl.BlockSpec((D_MODEL, bn), lambda j, i: (0, j + nt))],
        out_specs=pl.BlockSpec((BM, bn), lambda j, i: (i, j)),
        out_shape=jax.ShapeDtypeStruct((N_TOK, D_FF), BF16),
        compiler_params=_params(("arbitrary", "arbitrary"), 56),
        name="dense_gu",
    )(xb, w_gu, w_gu)


def _attn_kernel(sink_ref, q_ref, kp_ref, vp_ref, kc_ref, vc_ref, o_ref, *, blocks_per_seq):
    if blocks_per_seq is None:
        has_prev = True
    else:
        has_prev = (pl.program_id(0) % blocks_per_seq) != 0
    qi = lax.broadcasted_iota(jnp.int32, (WINDOW, 2 * WINDOW), 0)
    kj = lax.broadcasted_iota(jnp.int32, (WINDOW, 2 * WINDOW), 1)
    delta_i = WINDOW + qi - kj
    valid = (delta_i >= 0) & (delta_i <= WINDOW) & (has_prev | (kj >= WINDOW))
    delta = delta_i.astype(F32)
    low_half = lax.broadcasted_iota(jnp.int32, (2 * WINDOW, 128), 1) < HEAD_DIM

    kk = jnp.concatenate([kp_ref[...], kc_ref[...]], axis=0)
    vv = jnp.concatenate([vp_ref[...], vc_ref[...]], axis=0)

    def halves(slab, kv):
        if kv % 2 == 0:
            lo = jnp.where(low_half, slab, 0.0)
            hi = jnp.where(low_half, 0.0, pltpu.roll(slab, HEAD_DIM, 1))
        else:
            lo = jnp.where(low_half, pltpu.roll(slab, HEAD_DIM, 1), 0.0)
            hi = jnp.where(low_half, 0.0, slab)
        return lo.astype(BF16), hi.astype(BF16)

    for kv in range(N_KV_HEADS):
        lanes = slice((kv // 2) * 128, (kv // 2 + 1) * 128)
        k_halves = halves(kk[:, lanes], kv)
        v_halves = halves(vv[:, lanes], kv)
        for pair in range(GQA_GROUP // 2):
            qcols = slice((kv * 4 + pair) * 128, (kv * 4 + pair + 1) * 128)
            q2 = q_ref[:, qcols].astype(BF16)
            out = None
            for half in range(2):
                h = kv * GQA_GROUP + 2 * pair + half
                slope = float(2.0 ** (-8.0 * (h + 1) / N_HEADS))
                s = lax.dot_general(q2, k_halves[half], (((1,), (1,)), ((), ())),
                                    preferred_element_type=F32) * (HEAD_DIM ** -0.5)
                s = s - slope * delta
                s = jnp.where(valid, s, NEG_INF)
                sink = sink_ref[h]
                mx = jnp.maximum(jnp.max(s, axis=-1, keepdims=True), sink)
                p = jnp.exp(s - mx)
                den = jnp.sum(p, axis=-1, keepdims=True) + jnp.exp(sink - mx)
                p = p / den
                o = _dot(p.astype(BF16), v_halves[half])
                out = o if out is None else out + o
            o_ref[:, qcols] = out.astype(o_ref.dtype)


def _attention(sinks, q, kp, vp, kc, vc, n_blocks, specs, blocks_per_seq, name):
    return pl.pallas_call(
        functools.partial(_attn_kernel, blocks_per_seq=blocks_per_seq),
        grid=(n_blocks,),
        in_specs=[pl.BlockSpec(memory_space=pltpu.SMEM)] + specs,
        out_specs=pl.BlockSpec((WINDOW, D_MODEL), lambda i: (i, 0)),
        out_shape=jax.ShapeDtypeStruct((n_blocks * WINDOW, D_MODEL), BF16),
        compiler_params=_params(("arbitrary",), 32),
        name=name,
    )(sinks, q, kp, vp, kc, vc)


def _swa_mixer(xb, x, state_k, state_v, w_qkv, sinks, w_o, g, b):
    kvw = N_KV_HEADS * HEAD_DIM
    qkv = _mm(xb, w_qkv, 1280, F32, "qkv")
    kcol, vcol = D_MODEL // kvw, D_MODEL // kvw + 1
    nb = SEQ // WINDOW

    def prev_block(i):
        return jnp.maximum(i - 1, 0)

    prompt_specs = [pl.BlockSpec((WINDOW, D_MODEL), lambda i: (i, 0)),
                    pl.BlockSpec((WINDOW, kvw), lambda i: (prev_block(i), kcol)),
                    pl.BlockSpec((WINDOW, kvw), lambda i: (prev_block(i), vcol)),
                    pl.BlockSpec((WINDOW, kvw), lambda i: (i, kcol)),
                    pl.BlockSpec((WINDOW, kvw), lambda i: (i, vcol))]
    o_prompt = _attention(sinks, qkv, qkv, qkv, qkv, qkv, N_PROMPT // WINDOW, prompt_specs, nb, "attn_prompt")

    qkv_s = qkv[N_PROMPT:].reshape(DEC_BATCH, DEC_SEQ, D_MODEL + 2 * kvw)

    def padded(a):
        out = jnp.zeros((DEC_BATCH, WINDOW, a.shape[-1]), F32).at[:, :DEC_SEQ].set(a)
        return out.reshape(DEC_BATCH * WINDOW, a.shape[-1])

    k_new = qkv_s[..., D_MODEL:D_MODEL + kvw]
    v_new = qkv_s[..., D_MODEL + kvw:]
    blk = lambda i: (i, 0)
    sample_specs = [pl.BlockSpec((WINDOW, D_MODEL), blk)] + [pl.BlockSpec((WINDOW, kvw), blk)] * 4
    o_sample = _attention(sinks, padded(qkv_s[..., :D_MODEL]),
                          state_k.reshape(DEC_BATCH * WINDOW, kvw), state_v.reshape(DEC_BATCH * WINDOW, kvw),
                          padded(k_new), padded(v_new), DEC_BATCH, sample_specs, None, "attn_sample")
    o_sample = o_sample.reshape(DEC_BATCH, WINDOW, D_MODEL)[:, :DEC_SEQ].reshape(N_SAMPLE, D_MODEL)
    o_all = jnp.concatenate([o_prompt, o_sample], axis=0)
    x, xb = _mm_ln(o_all, w_o, x, g, b, BM, "attn_out")

    k_p = qkv[:N_PROMPT, D_MODEL:D_MODEL + kvw].reshape(BATCH, SEQ, N_KV_HEADS, HEAD_DIM)[:, -WINDOW:]
    v_p = qkv[:N_PROMPT, D_MODEL + kvw:].reshape(BATCH, SEQ, N_KV_HEADS, HEAD_DIM)[:, -WINDOW:]
    k_s = jnp.concatenate([state_k, k_new.reshape(DEC_BATCH, DEC_SEQ, N_KV_HEADS, HEAD_DIM)], axis=1)[:, -WINDOW:]
    v_s = jnp.concatenate([state_v, v_new.reshape(DEC_BATCH, DEC_SEQ, N_KV_HEADS, HEAD_DIM)], axis=1)[:, -WINDOW:]
    return x, xb, (k_p, v_p, k_s, v_s)


CONV_BM = 512
CONV_BC = 1024


def _conv_taps(z, z1, z2, gb, wc_ref):
    conv = wc_ref[0:1, :] * z2 + wc_ref[1:2, :] * z1 + wc_ref[2:3, :] * z
    return (gb * conv).astype(BF16)


def _conv_prompt_kernel(gb_ref, gc_ref, h_ref, wc_ref, y_ref, z_ref, carry_ref):
    @pl.when(pl.program_id(1) % (SEQ // CONV_BM) == 0)
    def _():
        carry_ref[...] = jnp.zeros_like(carry_ref)

    z = gc_ref[...] * h_ref[...]
    row = lax.broadcasted_iota(jnp.int32, z.shape, 0)
    c6 = carry_ref[6:7, :]
    c7 = carry_ref[7:8, :]
    z1 = jnp.where(row >= 1, pltpu.roll(z, 1, 0), c7)
    z2 = jnp.where(row >= 2, pltpu.roll(z, 2, 0), jnp.where(row == 1, c7, c6))
    y_ref[...] = _conv_taps(z, z1, z2, gb_ref[...], wc_ref)
    z_ref[...] = z
    carry_ref[...] = z[CONV_BM - 8:, :]


def _conv_sample_kernel(gb_ref, gc_ref, h_ref, wc_ref, pa_ref, pb_ref, y_ref, z_ref):
    z = gc_ref[...] * h_ref[...]
    t = lax.broadcasted_iota(jnp.int32, z.shape, 0) % DEC_SEQ
    z1 = jnp.where(t >= 1, pltpu.roll(z, 1, 0), pa_ref[...])
    z2 = jnp.where(t >= 2, pltpu.roll(z, 2, 0), pb_ref[...])
    y_ref[...] = _conv_taps(z, z1, z2, gb_ref[...], wc_ref)
    z_ref[...] = z


def _conv_mixer(xb, x, state, w_in, w_conv, w_out, g, b):
    proj = _mm(xb, w_in, 1024, F32, "conv_in")
    ncb = D_MODEL // CONV_BC
    y_p, z_p = pl.pallas_call(
        _conv_prompt_kernel,
        grid=(ncb, N_PROMPT // CONV_BM),
        in_specs=[pl.BlockSpec((CONV_BM, CONV_BC), lambda c, i: (i, c)),
                  pl.BlockSpec((CONV_BM, CONV_BC), lambda c, i: (i, ncb + c)),
                  pl.BlockSpec((CONV_BM, CONV_BC), lambda c, i: (i, 2 * ncb + c)),
                  pl.BlockSpec((CONV_WIDTH, CONV_BC), lambda c, i: (0, c))],
        out_specs=[pl.BlockSpec((CONV_BM, CONV_BC), lambda c, i: (i, c)),
                   pl.BlockSpec((CONV_BM, CONV_BC), lambda c, i: (i, c))],
        out_shape=[jax.ShapeDtypeStruct((N_PROMPT, D_MODEL), BF16), jax.ShapeDtypeStruct((N_PROMPT, D_MODEL), F32)],
        scratch_shapes=[pltpu.VMEM((8, CONV_BC), F32)],
        compiler_params=_params(("arbitrary", "arbitrary"), 40),
        name="conv_prompt",
    )(proj, proj, proj, w_conv)

    zeros = jnp.zeros((DEC_BATCH, DEC_SEQ, D_MODEL), F32)
    past_a = zeros.at[:, 0].set(state[:, 1]).reshape(N_SAMPLE, D_MODEL)
    past_b = zeros.at[:, 0].set(state[:, 0]).at[:, 1].set(state[:, 1]).reshape(N_SAMPLE, D_MODEL)
    srow = N_PROMPT // N_SAMPLE
    y_s, z_s = pl.pallas_call(
        _conv_sample_kernel,
        grid=(1,),
        in_specs=[pl.BlockSpec((N_SAMPLE, D_MODEL), lambda i: (srow, 0)),
                  pl.BlockSpec((N_SAMPLE, D_MODEL), lambda i: (srow, 1)),
                  pl.BlockSpec((N_SAMPLE, D_MODEL), lambda i: (srow, 2)),
                  pl.BlockSpec((CONV_WIDTH, D_MODEL), lambda i: (0, 0)),
                  pl.BlockSpec((N_SAMPLE, D_MODEL), lambda i: (0, 0)),
                  pl.BlockSpec((N_SAMPLE, D_MODEL), lambda i: (0, 0))],
        out_specs=[pl.BlockSpec((N_SAMPLE, D_MODEL), lambda i: (0, 0)),
                   pl.BlockSpec((N_SAMPLE, D_MODEL), lambda i: (0, 0))],
        out_shape=[jax.ShapeDtypeStruct((N_SAMPLE, D_MODEL), BF16), jax.ShapeDtypeStruct((N_SAMPLE, D_MODEL), F32)],
        compiler_params=_params(("arbitrary",), 32),
        name="conv_sample",
    )(proj, proj, proj, w_conv, past_a, past_b)

    y_all = jnp.concatenate([y_p, y_s], axis=0)
    x, xb = _mm_ln(y_all, w_out, x, g, b, BM, "conv_out")
    conv_p = z_p.reshape(BATCH, SEQ, D_MODEL)[:, -(CONV_WIDTH - 1):]
    conv_s = z_s.reshape(DEC_BATCH, DEC_SEQ, D_MODEL)[:, -(CONV_WIDTH - 1):]
    return x, xb, (conv_p, conv_s)


def _router_kernel(x_ref, w_ref, sel_ref, gate_ref):
    logits = jnp.dot(x_ref[...], w_ref[...], preferred_element_type=F32, precision=lax.Precision.HIGHEST)
    lane = lax.broadcasted_iota(jnp.int32, logits.shape, 1)
    neg = jnp.float32(-jnp.inf)
    l1 = jnp.where(lane < N_EXPERTS, logits, neg)
    m1 = jnp.max(l1, axis=-1, keepdims=True)
    i1 = jnp.min(jnp.where(l1 == m1, lane, 128), axis=-1, keepdims=True)
    l2 = jnp.where(lane == i1, neg, l1)
    m2 = jnp.max(l2, axis=-1, keepdims=True)
    i2 = jnp.min(jnp.where(l2 == m2, lane, 128), axis=-1, keepdims=True)
    e2 = jnp.exp(m2 - m1)
    den = 1.0 + e2
    g1 = 1.0 / den
    g2 = e2 / den
    sel_ref[...] = ((lane == i1) | (lane == i2)).astype(jnp.int32)
    gate_ref[...] = jnp.where(lane == i1, g1, jnp.where(lane == i2, g2, 0.0))


def _router(x, w_router_padded):
    row = lambda i: (i, 0)
    return pl.pallas_call(
        _router_kernel,
        grid=(N_TILES,),
        in_specs=[pl.BlockSpec((BM, D_MODEL), row), pl.BlockSpec((D_MODEL, 128), lambda i: (0, 0))],
        out_specs=[pl.BlockSpec((BM, 128), row), pl.BlockSpec((BM, 128), row)],
        out_shape=[jax.ShapeDtypeStruct((N_TOK, 128), jnp.int32), jax.ShapeDtypeStruct((N_TOK, 128), F32)],
        compiler_params=_params(("arbitrary",), 32),
        name="router",
    )(x, w_router_padded)


ROW_TILE = (D_MODEL // 128, 128)


def _row_dma(src_ref, src_row, dst_ref, dst_row, sem):
    return pltpu.make_async_copy(src_ref.at[src_row], dst_ref.at[dst_row], sem)


def _dispatch_kernel(idx_ref, src_ref, init_ref, dst_ref, sem):
    del init_ref
    base = pl.program_id(0) * BM

    def start(r, carry):
        _row_dma(src_ref, base + r, dst_ref, idx_ref[0, 0, r], sem).start()
        _row_dma(src_ref, base + r, dst_ref, idx_ref[0, 0, BM + r], sem).start()
        return carry

    def wait(r, carry):
        _row_dma(src_ref, 0, dst_ref, 0, sem).wait()
        return carry

    lax.fori_loop(0, BM, start, 0)
    lax.fori_loop(0, 2 * BM, wait, 0)


def _dispatch(xb, dest_chunks):
    init = jnp.zeros((MOE_SLOTS,) + ROW_TILE, BF16)
    xs = pl.pallas_call(
        _dispatch_kernel,
        grid=(N_TILES,),
        in_specs=[pl.BlockSpec((1, 1, 2 * BM), lambda i: (i, 0, 0), memory_space=pltpu.SMEM),
                  pl.BlockSpec(memory_space=pl.ANY),
                  pl.BlockSpec(memory_space=pl.ANY)],
        out_specs=pl.BlockSpec(memory_space=pl.ANY),
        out_shape=jax.ShapeDtypeStruct((MOE_SLOTS,) + ROW_TILE, BF16),
        scratch_shapes=[pltpu.SemaphoreType.DMA(())],
        input_output_aliases={2: 0},
        compiler_params=_params(("arbitrary",), 16),
        name="moe_dispatch",
    )(dest_chunks, xb.reshape((N_TOK,) + ROW_TILE), init)
    return xs.reshape(MOE_SLOTS, D_MODEL)


def _collect_kernel(idx_ref, src_ref, dst_ref, sem):
    base = pl.program_id(0) * BM

    def start(r, carry):
        _row_dma(src_ref, idx_ref[0, 0, r], dst_ref.at[0], base + r, sem).start()
        _row_dma(src_ref, idx_ref[0, 0, BM + r], dst_ref.at[1], base + r, sem).start()
        return carry

    def wait(r, carry):
        _row_dma(src_ref, 0, dst_ref.at[0], 0, sem).wait()
        return carry

    lax.fori_loop(0, BM, start, 0)
    lax.fori_loop(0, 2 * BM, wait, 0)


def _collect(ys, dest_chunks):
    y01 = pl.pallas_call(
        _collect_kernel,
        grid=(N_TILES,),
        in_specs=[pl.BlockSpec((1, 1, 2 * BM), lambda i: (i, 0, 0), memory_space=pltpu.SMEM),
                  pl.BlockSpec(memory_space=pl.ANY)],
        out_specs=pl.BlockSpec(memory_space=pl.ANY),
        out_shape=jax.ShapeDtypeStruct((2, N_TOK) + ROW_TILE, F32),
        scratch_shapes=[pltpu.SemaphoreType.DMA(())],
        compiler_params=_params(("arbitrary",), 16),
        name="moe_collect",
    )(dest_chunks, ys.reshape((MOE_SLOTS,) + ROW_TILE))
    return y01.reshape(2, N_TOK, D_MODEL)


def _expert_changed(be_ref, i):
    return (i == 0) | (be_ref[i] != be_ref[jnp.maximum(i - 1, 0)])


def _moe_gu_kernel(be_ref, nu_ref, x_ref, wg_ref, wu_ref, o_ref, wgb_ref, wub_ref):
    i = pl.program_id(1)

    @pl.when(_expert_changed(be_ref, i))
    def _():
        wgb_ref[...] = wg_ref[...].astype(BF16)
        wub_ref[...] = wu_ref[...].astype(BF16)

    @pl.when(i < nu_ref[0])
    def _():
        x = x_ref[...]
        gate = _dot(x, wgb_ref[...])
        up = _dot(x, wub_ref[...])
        o_ref[...] = (jax.nn.silu(gate) * up).astype(o_ref.dtype)

    @pl.when(i >= nu_ref[0])
    def _():
        o_ref[...] = jnp.zeros_like(o_ref)


def _moe_gu(xs, w_gu, block_e, n_used):
    nt = D_FF // MOE_BN
    grid_spec = pltpu.PrefetchScalarGridSpec(
        num_scalar_prefetch=2,
        grid=(nt, MOE_TILES),
        in_specs=[pl.BlockSpec((MOE_BM, D_MODEL), lambda j, i, be, nu: (i, 0)),
                  pl.BlockSpec((None, D_MODEL, MOE_BN), lambda j, i, be, nu: (be[i], 0, j)),
                  pl.BlockSpec((None, D_MODEL, MOE_BN), lambda j, i, be, nu: (be[i], 0, j + nt))],
        out_specs=pl.BlockSpec((MOE_BM, MOE_BN), lambda j, i, be, nu: (i, j)),
        scratch_shapes=[pltpu.VMEM((D_MODEL, MOE_BN), BF16), pltpu.VMEM((D_MODEL, MOE_BN), BF16)],
    )
    return pl.pallas_call(
        _moe_gu_kernel,
        grid_spec=grid_spec,
        out_shape=jax.ShapeDtypeStruct((MOE_SLOTS, D_FF), BF16),
        compiler_params=_params(("arbitrary", "arbitrary"), 40),
        name="moe_gu",
    )(block_e, n_used, xs, w_gu, w_gu)


def _moe_down_kernel(be_ref, nu_ref, h_ref, w_ref, o_ref, wb_ref):
    i = pl.program_id(1)

    @pl.when(_expert_changed(be_ref, i))
    def _():
        wb_ref[...] = w_ref[...].astype(BF16)

    @pl.when(i < nu_ref[0])
    def _():
        o_ref[...] = _dot(h_ref[...], wb_ref[...])

    @pl.when(i >= nu_ref[0])
    def _():
        o_ref[...] = jnp.zeros_like(o_ref)


def _moe_down(h, w_down, block_e, n_used):
    grid_spec = pltpu.PrefetchScalarGridSpec(
        num_scalar_prefetch=2,
        grid=(D_MODEL // MOE_BN, MOE_TILES),
        in_specs=[pl.BlockSpec((MOE_BM, D_FF), lambda j, i, be, nu: (i, 0)),
                  pl.BlockSpec((None, D_FF, MOE_BN), lambda j, i, be, nu: (be[i], 0, j))],
        out_specs=pl.BlockSpec((MOE_BM, MOE_BN), lambda j, i, be, nu: (i, j)),
        scratch_shapes=[pltpu.VMEM((D_FF, MOE_BN), BF16)],
    )
    return pl.pallas_call(
        _moe_down_kernel,
        grid_spec=grid_spec,
        out_shape=jax.ShapeDtypeStruct((MOE_SLOTS, D_MODEL), F32),
        compiler_params=_params(("arbitrary", "arbitrary"), 48),
        name="moe_down",
    )(block_e, n_used, h, w_down)


def _combine_ln_kernel(x_ref, y_ref, g0_ref, g1_ref, g_ref, b_ref, of_ref, ob_ref):
    f = y_ref[0] * g0_ref[...] + y_ref[1] * g1_ref[...]
    y = _layer_norm(DEEPNORM_ALPHA * x_ref[...] + f, g_ref[...], b_ref[...])
    of_ref[...] = y
    ob_ref[...] = y.astype(BF16)


def _combine_ln(x, y01, g0, g1, g, b):
    row = lambda i: (i, 0)
    const = lambda i: (0, 0)
    bm = BM // 2
    return pl.pallas_call(
        _combine_ln_kernel,
        grid=(N_TOK // bm,),
        in_specs=[pl.BlockSpec((bm, D_MODEL), row),
                  pl.BlockSpec((2, bm, D_MODEL), lambda i: (0, i, 0)),
                  pl.BlockSpec((bm, 1), row),
                  pl.BlockSpec((bm, 1), row),
                  pl.BlockSpec((1, D_MODEL), const),
                  pl.BlockSpec((1, D_MODEL), const)],
        out_specs=[pl.BlockSpec((bm, D_MODEL), row), pl.BlockSpec((bm, D_MODEL), row)],
        out_shape=[jax.ShapeDtypeStruct((N_TOK, D_MODEL), F32), jax.ShapeDtypeStruct((N_TOK, D_MODEL), BF16)],
        compiler_params=_params(("arbitrary",), 48),
        name="moe_combine_ln",
    )(x, y01, g0, g1, g, b)


def _moe_layer(x, xb, w_router, w_gu, w_down, g, b):
    w_r = jnp.zeros((D_MODEL, 128), F32).at[:, :N_EXPERTS].set(w_router)
    sel, gates = _router(x, w_r)
    sel = sel[:, :N_EXPERTS]
    gates = gates[:, :N_EXPERTS]

    counts = jnp.sum(sel, axis=0)
    rank = jnp.cumsum(sel, axis=0) - sel
    padded = (counts + MOE_BM - 1) // MOE_BM * MOE_BM
    pad_end = jnp.cumsum(padded)
    pad_start = pad_end - padded
    slot = pad_start[None, :] + rank
    chosen = sel > 0
    d0 = jnp.min(jnp.where(chosen, slot, MOE_SLOTS), axis=1)
    d1 = jnp.max(jnp.where(chosen, slot, -1), axis=1)
    g0 = jnp.sum(jnp.where(chosen & (slot == d0[:, None]), gates, 0.0), axis=1, keepdims=True)
    g1 = jnp.sum(jnp.where(chosen & (slot == d1[:, None]), gates, 0.0), axis=1, keepdims=True)
    dest_chunks = jnp.concatenate([d0.reshape(N_TILES, BM), d1.reshape(N_TILES, BM)], axis=1)
    dest_chunks = dest_chunks.astype(jnp.int32).reshape(N_TILES, 1, 2 * BM)
    tile_start = jnp.arange(MOE_TILES, dtype=jnp.int32) * MOE_BM
    block_e = jnp.minimum(jnp.searchsorted(pad_end, tile_start, side="right"), N_EXPERTS - 1).astype(jnp.int32)
    n_used = (pad_end[-1:] // MOE_BM).astype(jnp.int32)

    xs = _dispatch(xb, dest_chunks)
    h = _moe_gu(xs, w_gu, block_e, n_used)
    ys = _moe_down(h, w_down, block_e, n_used)
    y01 = _collect(ys, dest_chunks)
    return _combine_ln(x, y01, g0, g1, g, b)


def _block_diag_causal(w_s):
    small = jnp.tril(w_s[:, :DEC_SEQ, :DEC_SEQ])
    eye = jnp.eye(DEC_BATCH, dtype=w_s.dtype)
    return jnp.einsum("ab,gij->gaibj", eye, small).reshape(SGU_GROUPS, CHUNK, CHUNK)


def kernel(x_prompt, x_sample, state_swa_k, state_swa_v, state_conv, ln1_g, ln1_b, ln2_g, ln2_b, a_w_in, a_ln_g, a_ln_b, a_w_s, a_b_s, a_w_out, b_w_qkv, b_sinks, b_w_o, c_w_in, c_w_conv, c_w_out, f_w_gu, f_w_down, m_w_router, m_w_gu, m_w_down):
    x = jnp.concatenate([x_prompt.reshape(N_PROMPT, D_MODEL), x_sample.reshape(N_SAMPLE, D_MODEL)], axis=0)
    xb = x.astype(BF16)
    row = lambda a: a.reshape(1, D_MODEL)

    sgu_v, swa_states, conv_states = [], [], []
    for i in range(DEPTH):
        kind, j = i % N_MIXERS, i // N_MIXERS
        g1, b1, g2, b2 = row(ln1_g[i]), row(ln1_b[i]), row(ln2_g[i]), row(ln2_b[i])
        if kind == 0:
            z2 = _sgu_in(xb, a_w_in[j].astype(BF16), row(a_ln_g[j]), row(a_ln_b[j]))
            sgu_v.append(z2[N_PROMPT:, D_MODEL:].reshape(DEC_BATCH, DEC_SEQ, D_MODEL))
            b_t = a_b_s[j].T
            b_t_sample = jnp.tile(b_t[:DEC_SEQ], (DEC_BATCH, 1))
            x, xb = _sgu_out(z2, a_w_s[j], _block_diag_causal(a_w_s[j]), b_t, b_t_sample,
                             a_w_out[j].astype(BF16), x, g1, b1)
        elif kind == 1:
            x, xb, st = _swa_mixer(xb, x, state_swa_k[j], state_swa_v[j], b_w_qkv[j].astype(BF16),
                                   b_sinks[j], b_w_o[j].astype(BF16), g1, b1)
            swa_states.append(st)
        else:
            x, xb, st = _conv_mixer(xb, x, state_conv[j], c_w_in[j].astype(BF16), c_w_conv[j],
                                    c_w_out[j].astype(BF16), g1, b1)
            conv_states.append(st)
        if i % 2 == 0:
            h = _dense_gu(xb, f_w_gu[i // 2].astype(BF16))
            x, xb = _mm_ln(h, f_w_down[i // 2].astype(BF16), x, g2, b2, 320, "dense_down")
        else:
            x, xb = _moe_layer(x, xb, m_w_router[i // 2], m_w_gu[i // 2], m_w_down[i // 2], g2, b2)

    y_prompt = x[:N_PROMPT].reshape(BATCH, SEQ, D_MODEL)
    y_sample = x[N_PROMPT:].reshape(DEC_BATCH, DEC_SEQ, D_MODEL)
    return (y_prompt, y_sample, jnp.stack(sgu_v),
            jnp.stack([s[0] for s in swa_states]), jnp.stack([s[1] for s in swa_states]),
            jnp.stack([s[2] for s in swa_states]), jnp.stack([s[3] for s in swa_states]),
            jnp.stack([s[0] for s in conv_states]), jnp.stack([s[1] for s in conv_states]))
```

```python
import functools

import numpy as np
import jax
import jax.numpy as jnp
from jax import lax
from jax.experimental import pallas as pl
from jax.experimental.pallas import tpu as pltpu

D_MODEL = 2048
BATCH = 2
SEQ = 4096
DEPTH = 4
DEC_BATCH = 32
DEC_SEQ = 4
N_MIXERS = 3
CHUNK = 128
SGU_GROUPS = 8
SGU_GROUP_DIM = D_MODEL // SGU_GROUPS
HEAD_DIM = 64
N_HEADS = D_MODEL // HEAD_DIM
N_KV_HEADS = N_HEADS // 8
GQA_GROUP = N_HEADS // N_KV_HEADS
WINDOW = 128
CONV_WIDTH = 3
D_FF = 5632
N_EXPERTS = 8
LN_EPS = 1e-5
DEEPNORM_ALPHA = (2.0 * DEPTH) ** 0.25
NEG_INF = -1e30

N_PROMPT = BATCH * SEQ
N_SAMPLE = DEC_BATCH * DEC_SEQ
N_TOK = N_PROMPT + N_SAMPLE
BM = 640
N_TILES = N_TOK // BM
SGU_BM = 128
SGU_TILES = N_TOK // SGU_BM
CHUNKS_PER_TILE = SGU_BM // CHUNK

MOE_SUB = 256
MOE_BM = 2 * MOE_SUB
MOE_TILES = -(-(2 * N_TOK) // MOE_BM) + N_EXPERTS
MOE_SLOTS = MOE_TILES * MOE_BM
MOE_BN = 512
TOK_CHUNKS_FULL = N_TOK // MOE_SUB
TOK_TAIL = N_TOK - TOK_CHUNKS_FULL * MOE_SUB
TOK_CHUNKS = TOK_CHUNKS_FULL + 1
COMBINE_KMAX = N_EXPERTS * ((MOE_SUB - 1 + BM - 1) // MOE_SUB + 1)

F32 = jnp.float32
BF16 = jnp.bfloat16
MIB = 1024 * 1024


def _params(semantics, vmem_mib):
    return pltpu.CompilerParams(dimension_semantics=semantics, vmem_limit_bytes=vmem_mib * MIB)


def _layer_norm(y, g, b):
    mu = jnp.mean(y, axis=-1, keepdims=True)
    d = y - mu
    var = jnp.mean(d * d, axis=-1, keepdims=True)
    return d * lax.rsqrt(var + LN_EPS) * g + b


def _gelu(x):
    return 0.5 * x * (1.0 + lax.erf(x * np.float32(np.sqrt(0.5))))


def _dot(a, b):
    return jnp.dot(a, b, preferred_element_type=F32)


def _ln_specs(norm, layer, ngrid):
    def spec(kind):
        r = (2 * norm + kind) * DEPTH + layer
        if ngrid == 1:
            return pl.BlockSpec((None, 1, D_MODEL), lambda i: (r, 0, 0))
        return pl.BlockSpec((None, 1, D_MODEL), lambda j, i: (r, 0, 0))
    return [spec(0), spec(1)]


def _mm_kernel(x_ref, w_ref, o_ref):
    o_ref[...] = _dot(x_ref[...], w_ref[...]).astype(o_ref.dtype)


def _mm(x, w, layer, bn, out_dtype, name):
    m, k = x.shape
    n = w.shape[-1]
    return pl.pallas_call(
        _mm_kernel,
        grid=(n // bn, m // BM),
        in_specs=[pl.BlockSpec((BM, k), lambda j, i: (i, 0)),
                  pl.BlockSpec((None, k, bn), lambda j, i: (layer, 0, j))],
        out_specs=pl.BlockSpec((BM, bn), lambda j, i: (i, j)),
        out_shape=jax.ShapeDtypeStruct((m, n), out_dtype),
        compiler_params=_params(("arbitrary", "arbitrary"), 48),
        name=name,
    )(x, w)


def _mm_ln_kernel(x_ref, w_ref, r_ref, g_ref, b_ref, of_ref, ob_ref):
    f = _dot(x_ref[...], w_ref[...])
    y = _layer_norm(DEEPNORM_ALPHA * r_ref[...] + f, g_ref[...], b_ref[...])
    of_ref[...] = y
    ob_ref[...] = y.astype(BF16)


def _mm_ln(x, w, w_layer, resid, lnp, norm, layer, bm, name):
    m, k = x.shape
    row = lambda i: (i, 0)
    return pl.pallas_call(
        _mm_ln_kernel,
        grid=(m // bm,),
        in_specs=[pl.BlockSpec((bm, k), row),
                  pl.BlockSpec((None, k, D_MODEL), lambda i: (w_layer, 0, 0), pipeline_mode=pl.Buffered(1)),
                  pl.BlockSpec((bm, D_MODEL), row)] + _ln_specs(norm, layer, 1),
        out_specs=[pl.BlockSpec((bm, D_MODEL), row), pl.BlockSpec((bm, D_MODEL), row)],
        out_shape=[jax.ShapeDtypeStruct((m, D_MODEL), F32), jax.ShapeDtypeStruct((m, D_MODEL), BF16)],
        compiler_params=_params(("arbitrary",), 56),
        name=name,
    )(x, w, resid, lnp, lnp)


def _sgu_in_kernel(x_ref, w_ref, g_ref, b_ref, o_ref):
    z = _gelu(_dot(x_ref[...], w_ref[...]))

    @pl.when(pl.program_id(0) == 0)
    def _():
        o_ref[...] = z

    @pl.when(pl.program_id(0) == 1)
    def _():
        o_ref[...] = _layer_norm(z, g_ref[...], b_ref[...])


def _sgu_in(xb, w_in, a_lnp, j):
    return pl.pallas_call(
        _sgu_in_kernel,
        grid=(2, N_TILES),
        in_specs=[pl.BlockSpec((BM, D_MODEL), lambda c, i: (i, 0)),
                  pl.BlockSpec((None, D_MODEL, D_MODEL), lambda c, i: (j, 0, c)),
                  pl.BlockSpec((None, 1, D_MODEL), lambda c, i: (j, 0, 0)),
                  pl.BlockSpec((None, 1, D_MODEL), lambda c, i: (2 + j, 0, 0))],
        out_specs=pl.BlockSpec((BM, D_MODEL), lambda c, i: (i, c)),
        out_shape=jax.ShapeDtypeStruct((N_TOK, 2 * D_MODEL), F32),
        compiler_params=_params(("arbitrary", "arbitrary"), 56),
        name="sgu_in",
    )(xb, w_in, a_lnp, a_lnp)


def _sgu_out_kernel(u_ref, v_ref, wp_ref, ws_ref, bp_ref, bs_ref, wo_ref, r_ref, g_ref, b_ref,
                    of_ref, ob_ref, gated_ref):
    is_sample_tile = pl.program_id(0) == SGU_TILES - 1
    row = lax.broadcasted_iota(jnp.int32, (CHUNK, CHUNK), 0)
    col = lax.broadcasted_iota(jnp.int32, (CHUNK, CHUNK), 1)
    causal = row >= col
    for grp in range(SGU_GROUPS):
        cols = slice(grp * SGU_GROUP_DIM, (grp + 1) * SGU_GROUP_DIM)
        w_prompt = jnp.where(causal, wp_ref[grp], 0.0)
        bias_prompt = bp_ref[:, grp:grp + 1]
        w_last = jnp.where(is_sample_tile, ws_ref[grp], w_prompt).astype(BF16)
        bias_last = jnp.where(is_sample_tile, bs_ref[:, grp:grp + 1], bias_prompt)
        w_prompt = w_prompt.astype(BF16)
        for c in range(CHUNKS_PER_TILE):
            rows = slice(c * CHUNK, (c + 1) * CHUNK)
            last = c == CHUNKS_PER_TILE - 1
            mixed = _dot(w_last if last else w_prompt, v_ref[rows, cols].astype(BF16))
            mixed = mixed + (bias_last if last else bias_prompt)
            gated_ref[rows, cols] = (u_ref[rows, cols] * mixed).astype(BF16)
    f = _dot(gated_ref[...], wo_ref[...])
    y = _layer_norm(DEEPNORM_ALPHA * r_ref[...] + f, g_ref[...], b_ref[...])
    of_ref[...] = y
    ob_ref[...] = y.astype(BF16)


def _sgu_out(z2, w_s, w_s_sample, b_t, b_t_sample, w_out, j, resid, lnp, layer):
    row = lambda i: (i, 0)
    const2 = lambda i: (0, 0)
    const3 = lambda i: (0, 0, 0)
    return pl.pallas_call(
        _sgu_out_kernel,
        grid=(SGU_TILES,),
        in_specs=[pl.BlockSpec((SGU_BM, D_MODEL), lambda i: (i, 0)),
                  pl.BlockSpec((SGU_BM, D_MODEL), lambda i: (i, 1)),
                  pl.BlockSpec((None, SGU_GROUPS, CHUNK, CHUNK), lambda i: (j, 0, 0, 0)),
                  pl.BlockSpec((SGU_GROUPS, CHUNK, CHUNK), const3),
                  pl.BlockSpec((CHUNK, SGU_GROUPS), const2),
                  pl.BlockSpec((CHUNK, SGU_GROUPS), const2),
                  pl.BlockSpec((None, D_MODEL, D_MODEL), lambda i: (j, 0, 0), pipeline_mode=pl.Buffered(1)),
                  pl.BlockSpec((SGU_BM, D_MODEL), row)] + _ln_specs(0, layer, 1),
        out_specs=[pl.BlockSpec((SGU_BM, D_MODEL), row), pl.BlockSpec((SGU_BM, D_MODEL), row)],
        out_shape=[jax.ShapeDtypeStruct((N_TOK, D_MODEL), F32), jax.ShapeDtypeStruct((N_TOK, D_MODEL), BF16)],
        scratch_shapes=[pltpu.VMEM((SGU_BM, D_MODEL), BF16)],
        compiler_params=_params(("arbitrary",), 40),
        name="sgu_out",
    )(z2, z2, w_s, w_s_sample, b_t, b_t_sample, w_out, resid, lnp, lnp)


def _gu_kernel(x_ref, wg_ref, wu_ref, o_ref):
    x = x_ref[...]
    gate = _dot(x, wg_ref[...])
    up = _dot(x, wu_ref[...])
    o_ref[...] = (jax.nn.silu(gate) * up).astype(o_ref.dtype)


def _dense_gu(xb, w_gu, layer):
    bn = 1408
    nt = D_FF // bn
    return pl.pallas_call(
        _gu_kernel,
        grid=(nt, N_TILES),
        in_specs=[pl.BlockSpec((BM, D_MODEL), lambda j, i: (i, 0)),
                  pl.BlockSpec((None, D_MODEL, bn), lambda j, i: (layer, 0, j)),
                  pl.BlockSpec((None, D_MODEL, bn), lambda j, i: (layer, 0, j + nt))],
        out_specs=pl.BlockSpec((BM, bn), lambda j, i: (i, j)),
        out_shape=jax.ShapeDtypeStruct((N_TOK, D_FF), BF16),
        compiler_params=_params(("arbitrary", "arbitrary"), 56),
        name="dense_gu",
    )(xb, w_gu, w_gu)


def _attn_kernel(sink_ref, q_ref, kp_ref, vp_ref, kc_ref, vc_ref, o_ref, *, blocks_per_seq):
    if blocks_per_seq is None:
        has_prev = True
    else:
        has_prev = (pl.program_id(0) % blocks_per_seq) != 0
    qi = lax.broadcasted_iota(jnp.int32, (WINDOW, 2 * WINDOW), 0)
    kj = lax.broadcasted_iota(jnp.int32, (WINDOW, 2 * WINDOW), 1)
    delta_i = WINDOW + qi - kj
    valid = (delta_i >= 0) & (delta_i <= WINDOW) & (has_prev | (kj >= WINDOW))
    delta = delta_i.astype(F32)
    low_half = lax.broadcasted_iota(jnp.int32, (2 * WINDOW, 128), 1) < HEAD_DIM

    kk = jnp.concatenate([kp_ref[...], kc_ref[...]], axis=0)
    vv = jnp.concatenate([vp_ref[...], vc_ref[...]], axis=0)

    def halves(slab, kv):
        if kv % 2 == 0:
            lo = jnp.where(low_half, slab, 0.0)
            hi = jnp.where(low_half, 0.0, pltpu.roll(slab, HEAD_DIM, 1))
        else:
            lo = jnp.where(low_half, pltpu.roll(slab, HEAD_DIM, 1), 0.0)
            hi = jnp.where(low_half, 0.0, slab)
        return lo.astype(BF16), hi.astype(BF16)

    for kv in range(N_KV_HEADS):
        lanes = slice((kv // 2) * 128, (kv // 2 + 1) * 128)
        k_halves = halves(kk[:, lanes], kv)
        v_halves = halves(vv[:, lanes], kv)
        for pair in range(GQA_GROUP // 2):
            qcols = slice((kv * 4 + pair) * 128, (kv * 4 + pair + 1) * 128)
            q2 = q_ref[:, qcols].astype(BF16)
            out = None
            for half in range(2):
                h = kv * GQA_GROUP + 2 * pair + half
                slope = float(2.0 ** (-8.0 * (h + 1) / N_HEADS))
                s = lax.dot_general(q2, k_halves[half], (((1,), (1,)), ((), ())),
                                    preferred_element_type=F32) * (HEAD_DIM ** -0.5)
                s = s - slope * delta
                s = jnp.where(valid, s, NEG_INF)
                sink = sink_ref[h]
                mx = jnp.maximum(jnp.max(s, axis=-1, keepdims=True), sink)
                p = jnp.exp(s - mx)
                den = jnp.sum(p, axis=-1, keepdims=True) + jnp.exp(sink - mx)
                p = p / den
                o = _dot(p.astype(BF16), v_halves[half])
                out = o if out is None else out + o
            o_ref[:, qcols] = out.astype(o_ref.dtype)


def _attention(sinks, q, kp, vp, kc, vc, n_blocks, specs, blocks_per_seq, name):
    return pl.pallas_call(
        functools.partial(_attn_kernel, blocks_per_seq=blocks_per_seq),
        grid=(n_blocks,),
        in_specs=[pl.BlockSpec(memory_space=pltpu.SMEM)] + specs,
        out_specs=pl.BlockSpec((WINDOW, D_MODEL), lambda i: (i, 0)),
        out_shape=jax.ShapeDtypeStruct((n_blocks * WINDOW, D_MODEL), BF16),
        compiler_params=_params(("arbitrary",), 32),
        name=name,
    )(sinks, q, kp, vp, kc, vc)


def _swa_mixer(xb, x, state_k, state_v, w_qkv, sinks, w_o, j, lnp, layer):
    kvw = N_KV_HEADS * HEAD_DIM
    qkv = _mm(xb, w_qkv, j, 1280, F32, "qkv")
    kcol, vcol = D_MODEL // kvw, D_MODEL // kvw + 1
    nb = SEQ // WINDOW

    def prev_block(i):
        return jnp.maximum(i - 1, 0)

    prompt_specs = [pl.BlockSpec((WINDOW, D_MODEL), lambda i: (i, 0)),
                    pl.BlockSpec((WINDOW, kvw), lambda i: (prev_block(i), kcol)),
                    pl.BlockSpec((WINDOW, kvw), lambda i: (prev_block(i), vcol)),
                    pl.BlockSpec((WINDOW, kvw), lambda i: (i, kcol)),
                    pl.BlockSpec((WINDOW, kvw), lambda i: (i, vcol))]
    o_prompt = _attention(sinks, qkv, qkv, qkv, qkv, qkv, N_PROMPT // WINDOW, prompt_specs, nb, "attn_prompt")

    qkv_s = qkv[N_PROMPT:].reshape(DEC_BATCH, DEC_SEQ, D_MODEL + 2 * kvw)
    padded = jnp.zeros((DEC_BATCH, WINDOW, D_MODEL + 2 * kvw), F32).at[:, :DEC_SEQ].set(qkv_s)
    padded = padded.reshape(DEC_BATCH * WINDOW, D_MODEL + 2 * kvw)
    k_new = qkv_s[..., D_MODEL:D_MODEL + kvw]
    v_new = qkv_s[..., D_MODEL + kvw:]
    blk = lambda i: (i, 0)
    sample_specs = [pl.BlockSpec((WINDOW, D_MODEL), blk),
                    pl.BlockSpec((WINDOW, kvw), blk),
                    pl.BlockSpec((WINDOW, kvw), blk),
                    pl.BlockSpec((WINDOW, kvw), lambda i: (i, kcol)),
                    pl.BlockSpec((WINDOW, kvw), lambda i: (i, vcol))]
    o_sample = _attention(sinks, padded, state_k.reshape(DEC_BATCH * WINDOW, kvw),
                          state_v.reshape(DEC_BATCH * WINDOW, kvw), padded, padded,
                          DEC_BATCH, sample_specs, None, "attn_sample")
    o_sample = o_sample.reshape(DEC_BATCH, WINDOW, D_MODEL)[:, :DEC_SEQ].reshape(N_SAMPLE, D_MODEL)
    o_all = jnp.concatenate([o_prompt, o_sample], axis=0)
    x, xb = _mm_ln(o_all, w_o, j, x, lnp, 0, layer, BM, "attn_out")

    k_p = qkv[:N_PROMPT, D_MODEL:D_MODEL + kvw].reshape(BATCH, SEQ, N_KV_HEADS, HEAD_DIM)[:, -WINDOW:]
    v_p = qkv[:N_PROMPT, D_MODEL + kvw:].reshape(BATCH, SEQ, N_KV_HEADS, HEAD_DIM)[:, -WINDOW:]
    k_s = jnp.concatenate([state_k, k_new.reshape(DEC_BATCH, DEC_SEQ, N_KV_HEADS, HEAD_DIM)], axis=1)[:, -WINDOW:]
    v_s = jnp.concatenate([state_v, v_new.reshape(DEC_BATCH, DEC_SEQ, N_KV_HEADS, HEAD_DIM)], axis=1)[:, -WINDOW:]
    return x, xb, (k_p, v_p, k_s, v_s)


CONV_BM = 512
CONV_BC = 1024


def _conv_taps(z, z1, z2, gb, wc_ref):
    conv = wc_ref[0:1, :] * z2 + wc_ref[1:2, :] * z1 + wc_ref[2:3, :] * z
    return (gb * conv).astype(BF16)


def _conv_prompt_kernel(gb_ref, gc_ref, h_ref, wc_ref, y_ref, z_ref, carry_ref):
    @pl.when(pl.program_id(1) % (SEQ // CONV_BM) == 0)
    def _():
        carry_ref[...] = jnp.zeros_like(carry_ref)

    z = gc_ref[...] * h_ref[...]
    row = lax.broadcasted_iota(jnp.int32, z.shape, 0)
    c6 = carry_ref[6:7, :]
    c7 = carry_ref[7:8, :]
    z1 = jnp.where(row >= 1, pltpu.roll(z, 1, 0), c7)
    z2 = jnp.where(row >= 2, pltpu.roll(z, 2, 0), jnp.where(row == 1, c7, c6))
    y_ref[...] = _conv_taps(z, z1, z2, gb_ref[...], wc_ref)
    z_ref[...] = z
    carry_ref[...] = z[CONV_BM - 8:, :]


def _conv_sample_kernel(gb_ref, gc_ref, h_ref, wc_ref, pa_ref, pb_ref, y_ref, z_ref):
    z = gc_ref[...] * h_ref[...]
    t = lax.broadcasted_iota(jnp.int32, z.shape, 0) % DEC_SEQ
    z1 = jnp.where(t >= 1, pltpu.roll(z, 1, 0), pa_ref[...])
    z2 = jnp.where(t >= 2, pltpu.roll(z, 2, 0), pb_ref[...])
    y_ref[...] = _conv_taps(z, z1, z2, gb_ref[...], wc_ref)
    z_ref[...] = z


def _conv_mixer(xb, x, state, w_in, w_conv, w_out, j, lnp, layer):
    proj = _mm(xb, w_in, j, 1024, F32, "conv_in")
    ncb = D_MODEL // CONV_BC
    y_p, z_p = pl.pallas_call(
        _conv_prompt_kernel,
        grid=(ncb, N_PROMPT // CONV_BM),
        in_specs=[pl.BlockSpec((CONV_BM, CONV_BC), lambda c, i: (i, c)),
                  pl.BlockSpec((CONV_BM, CONV_BC), lambda c, i: (i, ncb + c)),
                  pl.BlockSpec((CONV_BM, CONV_BC), lambda c, i: (i, 2 * ncb + c)),
                  pl.BlockSpec((None, CONV_WIDTH, CONV_BC), lambda c, i: (j, 0, c))],
        out_specs=[pl.BlockSpec((CONV_BM, CONV_BC), lambda c, i: (i, c)),
                   pl.BlockSpec((CONV_BM, CONV_BC), lambda c, i: (i, c))],
        out_shape=[jax.ShapeDtypeStruct((N_PROMPT, D_MODEL), BF16), jax.ShapeDtypeStruct((N_PROMPT, D_MODEL), F32)],
        scratch_shapes=[pltpu.VMEM((8, CONV_BC), F32)],
        compiler_params=_params(("arbitrary", "arbitrary"), 40),
        name="conv_prompt",
    )(proj, proj, proj, w_conv)

    zeros = jnp.zeros((DEC_BATCH, DEC_SEQ, D_MODEL), F32)
    past_a = zeros.at[:, 0].set(state[:, 1]).reshape(N_SAMPLE, D_MODEL)
    past_b = zeros.at[:, 0].set(state[:, 0]).at[:, 1].set(state[:, 1]).reshape(N_SAMPLE, D_MODEL)
    srow = N_PROMPT // N_SAMPLE
    whole = lambda i: (0, 0)
    y_s, z_s = pl.pallas_call(
        _conv_sample_kernel,
        grid=(1,),
        in_specs=[pl.BlockSpec((N_SAMPLE, D_MODEL), lambda i: (srow, 0)),
                  pl.BlockSpec((N_SAMPLE, D_MODEL), lambda i: (srow, 1)),
                  pl.BlockSpec((N_SAMPLE, D_MODEL), lambda i: (srow, 2)),
                  pl.BlockSpec((None, CONV_WIDTH, D_MODEL), lambda i: (j, 0, 0)),
                  pl.BlockSpec((N_SAMPLE, D_MODEL), whole),
                  pl.BlockSpec((N_SAMPLE, D_MODEL), whole)],
        out_specs=[pl.BlockSpec((N_SAMPLE, D_MODEL), whole), pl.BlockSpec((N_SAMPLE, D_MODEL), whole)],
        out_shape=[jax.ShapeDtypeStruct((N_SAMPLE, D_MODEL), BF16), jax.ShapeDtypeStruct((N_SAMPLE, D_MODEL), F32)],
        compiler_params=_params(("arbitrary",), 32),
        name="conv_sample",
    )(proj, proj, proj, w_conv, past_a, past_b)

    y_all = jnp.concatenate([y_p, y_s], axis=0)
    x, xb = _mm_ln(y_all, w_out, j, x, lnp, 0, layer, BM, "conv_out")
    conv_p = z_p.reshape(BATCH, SEQ, D_MODEL)[:, -(CONV_WIDTH - 1):]
    conv_s = z_s.reshape(DEC_BATCH, DEC_SEQ, D_MODEL)[:, -(CONV_WIDTH - 1):]
    return x, xb, (conv_p, conv_s)


def _router_kernel(x_ref, w_ref, sel_ref, gate_ref):
    logits = jnp.dot(x_ref[...], w_ref[...], preferred_element_type=F32, precision=lax.Precision.HIGHEST)
    lane = lax.broadcasted_iota(jnp.int32, logits.shape, 1)
    neg = jnp.float32(-jnp.inf)
    l1 = jnp.where(lane < N_EXPERTS, logits, neg)
    m1 = jnp.max(l1, axis=-1, keepdims=True)
    i1 = jnp.min(jnp.where(l1 == m1, lane, 128), axis=-1, keepdims=True)
    l2 = jnp.where(lane == i1, neg, l1)
    m2 = jnp.max(l2, axis=-1, keepdims=True)
    i2 = jnp.min(jnp.where(l2 == m2, lane, 128), axis=-1, keepdims=True)
    e2 = jnp.exp(m2 - m1)
    den = 1.0 + e2
    g1 = 1.0 / den
    g2 = e2 / den
    sel_ref[...] = ((lane == i1) | (lane == i2)).astype(jnp.int32)
    gate_ref[...] = jnp.where(lane == i1, g1, jnp.where(lane == i2, g2, 0.0))


def _router(x, w_router_padded, layer):
    row = lambda i: (i, 0)
    return pl.pallas_call(
        _router_kernel,
        grid=(N_TILES,),
        in_specs=[pl.BlockSpec((BM, D_MODEL), row),
                  pl.BlockSpec((None, D_MODEL, 128), lambda i: (layer, 0, 0))],
        out_specs=[pl.BlockSpec((BM, 128), row), pl.BlockSpec((BM, 128), row)],
        out_shape=[jax.ShapeDtypeStruct((N_TOK, 128), jnp.int32), jax.ShapeDtypeStruct((N_TOK, 128), F32)],
        compiler_params=_params(("arbitrary",), 32),
        name="router",
    )(x, w_router_padded)


def _dispatch_kernel(lo_ref, hi_ref, d0_ref, d1_ref, g0_ref, g1_ref, x_ref, xs_ref, gs_ref, acc_ref, gacc_ref):
    i = pl.program_id(0)
    lo = lo_ref[i]
    hi = hi_ref[i]
    slot = i * MOE_SUB + lax.broadcasted_iota(jnp.int32, (MOE_SUB, MOE_SUB), 0)
    acc_ref[...] = jnp.zeros_like(acc_ref)
    gacc_ref[...] = jnp.zeros_like(gacc_ref)

    def chunk(c, width, x_rows):
        s = slot[:, :width]
        m0 = s == d0_ref[c][:, :width]
        m1 = s == d1_ref[c][:, :width]
        onehot = jnp.where(m0 | m1, 1.0, 0.0).astype(BF16)
        acc_ref[...] += _dot(onehot, x_rows)
        picked = jnp.where(m0, g0_ref[c][:, :width], 0.0) + jnp.where(m1, g1_ref[c][:, :width], 0.0)
        gacc_ref[...] += jnp.sum(picked, axis=1, keepdims=True)

    def body(c, carry):
        start = pl.multiple_of(c * MOE_SUB, MOE_SUB)
        chunk(c, MOE_SUB, x_ref[pl.ds(start, MOE_SUB), :])
        return carry

    lax.fori_loop(lo, jnp.minimum(hi, TOK_CHUNKS_FULL - 1) + 1, body, 0)

    @pl.when(hi == TOK_CHUNKS_FULL)
    def _():
        chunk(TOK_CHUNKS_FULL, TOK_TAIL, x_ref[TOK_CHUNKS_FULL * MOE_SUB:, :])

    xs_ref[...] = acc_ref[...].astype(BF16)
    gs_ref[...] = gacc_ref[...]


def _dispatch(xb, chunk_lo, chunk_hi, d0r, d1r, g0r, g1r):
    n_tiles = MOE_SLOTS // MOE_SUB
    whole3 = lambda i, lo, hi: (0, 0, 0)
    rows = pl.BlockSpec((TOK_CHUNKS, 1, MOE_SUB), whole3)
    grid_spec = pltpu.PrefetchScalarGridSpec(
        num_scalar_prefetch=2,
        grid=(n_tiles,),
        in_specs=[rows, rows, rows, rows,
                  pl.BlockSpec((N_TOK, D_MODEL), lambda i, lo, hi: (0, 0))],
        out_specs=[pl.BlockSpec((MOE_SUB, D_MODEL), lambda i, lo, hi: (i, 0)),
                   pl.BlockSpec((MOE_SUB, 1), lambda i, lo, hi: (i, 0))],
        scratch_shapes=[pltpu.VMEM((MOE_SUB, D_MODEL), F32), pltpu.VMEM((MOE_SUB, 1), F32)],
    )
    return pl.pallas_call(
        _dispatch_kernel,
        grid_spec=grid_spec,
        out_shape=[jax.ShapeDtypeStruct((MOE_SLOTS, D_MODEL), BF16), jax.ShapeDtypeStruct((MOE_SLOTS, 1), F32)],
        compiler_params=_params(("arbitrary",), 52),
        name="moe_dispatch",
    )(chunk_lo, chunk_hi, d0r, d1r, g0r, g1r, xb)


def _expert_changed(be_ref, i):
    return (i == 0) | (be_ref[i] != be_ref[jnp.maximum(i - 1, 0)])


def _for_sub_blocks(valid_rows, live, dead):
    for sub in range(MOE_BM // MOE_SUB):
        rows = slice(sub * MOE_SUB, (sub + 1) * MOE_SUB)
        pl.when(valid_rows > sub * MOE_SUB)(functools.partial(live, rows))
        pl.when(valid_rows <= sub * MOE_SUB)(functools.partial(dead, rows))


def _moe_gu_kernel(be_ref, nv_ref, x_ref, wg_ref, wu_ref, o_ref, wgb_ref, wub_ref):
    i = pl.program_id(1)

    @pl.when(_expert_changed(be_ref, i))
    def _():
        wgb_ref[...] = wg_ref[...].astype(BF16)
        wub_ref[...] = wu_ref[...].astype(BF16)

    def live(rows):
        x = x_ref[rows, :]
        gate = _dot(x, wgb_ref[...])
        up = _dot(x, wub_ref[...])
        o_ref[rows, :] = (jax.nn.silu(gate) * up).astype(o_ref.dtype)

    def dead(rows):
        o_ref[rows, :] = jnp.zeros((MOE_SUB, MOE_BN), o_ref.dtype)

    _for_sub_blocks(nv_ref[i], live, dead)


def _moe_gu(xs, w_gu, layer, block_e, valid_rows):
    nt = D_FF // MOE_BN
    grid_spec = pltpu.PrefetchScalarGridSpec(
        num_scalar_prefetch=2,
        grid=(nt, MOE_TILES),
        in_specs=[pl.BlockSpec((MOE_BM, D_MODEL), lambda j, i, be, nv: (i, 0)),
                  pl.BlockSpec((None, None, D_MODEL, MOE_BN), lambda j, i, be, nv: (layer, be[i], 0, j)),
                  pl.BlockSpec((None, None, D_MODEL, MOE_BN), lambda j, i, be, nv: (layer, be[i], 0, j + nt))],
        out_specs=pl.BlockSpec((MOE_BM, MOE_BN), lambda j, i, be, nv: (i, j)),
        scratch_shapes=[pltpu.VMEM((D_MODEL, MOE_BN), BF16), pltpu.VMEM((D_MODEL, MOE_BN), BF16)],
    )
    return pl.pallas_call(
        _moe_gu_kernel,
        grid_spec=grid_spec,
        out_shape=jax.ShapeDtypeStruct((MOE_SLOTS, D_FF), BF16),
        compiler_params=_params(("arbitrary", "arbitrary"), 40),
        name="moe_gu",
    )(block_e, valid_rows, xs, w_gu, w_gu)


def _moe_down_kernel(be_ref, nv_ref, h_ref, w_ref, gs_ref, o_ref, wb_ref):
    i = pl.program_id(1)

    @pl.when(_expert_changed(be_ref, i))
    def _():
        wb_ref[...] = w_ref[...].astype(BF16)

    def live(rows):
        o_ref[rows, :] = (_dot(h_ref[rows, :], wb_ref[...]) * gs_ref[rows, :]).astype(o_ref.dtype)

    def dead(rows):
        o_ref[rows, :] = jnp.zeros((MOE_SUB, MOE_BN), o_ref.dtype)

    _for_sub_blocks(nv_ref[i], live, dead)


def _moe_down(h, w_down, layer, gs, block_e, valid_rows):
    grid_spec = pltpu.PrefetchScalarGridSpec(
        num_scalar_prefetch=2,
        grid=(D_MODEL // MOE_BN, MOE_TILES),
        in_specs=[pl.BlockSpec((MOE_BM, D_FF), lambda j, i, be, nv: (i, 0)),
                  pl.BlockSpec((None, None, D_FF, MOE_BN), lambda j, i, be, nv: (layer, be[i], 0, j)),
                  pl.BlockSpec((MOE_BM, 1), lambda j, i, be, nv: (i, 0))],
        out_specs=pl.BlockSpec((MOE_BM, MOE_BN), lambda j, i, be, nv: (i, j)),
        scratch_shapes=[pltpu.VMEM((D_FF, MOE_BN), BF16)],
    )
    return pl.pallas_call(
        _moe_down_kernel,
        grid_spec=grid_spec,
        out_shape=jax.ShapeDtypeStruct((MOE_SLOTS, D_MODEL), BF16),
        compiler_params=_params(("arbitrary", "arbitrary"), 52),
        name="moe_down",
    )(block_e, valid_rows, h, w_down, gs)


def _combine_ln_kernel(nk_ref, ck_ref, d0_ref, d1_ref, x_ref, ys_ref, g_ref, b_ref, of_ref, ob_ref,
                       buf_ref, sem_ref, acc_ref):
    i = pl.program_id(0)
    n = nk_ref[i]

    def fetch(k, which):
        start = pl.multiple_of(ck_ref[i * COMBINE_KMAX + k] * MOE_SUB, MOE_SUB)
        return pltpu.make_async_copy(ys_ref.at[pl.ds(start, MOE_SUB)], buf_ref.at[which], sem_ref.at[which])

    @pl.when(n > 0)
    def _():
        fetch(0, 0).start()

    acc_ref[...] = jnp.zeros_like(acc_ref)
    d0 = d0_ref[...]
    d1 = d1_ref[...]
    lane = lax.broadcasted_iota(jnp.int32, (BM, MOE_SUB), 1)

    def body(k, carry):
        which = k % 2
        fetch(k, which).wait()

        @pl.when(k + 1 < n)
        def _():
            fetch(k + 1, 1 - which).start()

        slot = ck_ref[i * COMBINE_KMAX + k] * MOE_SUB + lane
        onehot = jnp.where((d0 == slot) | (d1 == slot), 1.0, 0.0).astype(BF16)
        acc_ref[...] += _dot(onehot, buf_ref[which])
        return carry

    lax.fori_loop(0, n, body, 0)
    y = _layer_norm(DEEPNORM_ALPHA * x_ref[...] + acc_ref[...], g_ref[...], b_ref[...])
    of_ref[...] = y
    ob_ref[...] = y.astype(BF16)


def _combine_ln(x, ys, n_chunks, chunk_ids, d0c, d1c, lnp, layer):
    row = lambda i, nk, ck: (i, 0)

    def ln_spec(kind):
        r = (2 + kind) * DEPTH + layer
        return pl.BlockSpec((None, 1, D_MODEL), lambda i, nk, ck: (r, 0, 0))

    grid_spec = pltpu.PrefetchScalarGridSpec(
        num_scalar_prefetch=2,
        grid=(N_TILES,),
        in_specs=[pl.BlockSpec((BM, 1), row),
                  pl.BlockSpec((BM, 1), row),
                  pl.BlockSpec((BM, D_MODEL), row),
                  pl.BlockSpec(memory_space=pl.ANY),
                  ln_spec(0), ln_spec(1)],
        out_specs=[pl.BlockSpec((BM, D_MODEL), row), pl.BlockSpec((BM, D_MODEL), row)],
        scratch_shapes=[pltpu.VMEM((2, MOE_SUB, D_MODEL), BF16),
                        pltpu.SemaphoreType.DMA((2,)),
                        pltpu.VMEM((BM, D_MODEL), F32)],
    )
    return pl.pallas_call(
        _combine_ln_kernel,
        grid_spec=grid_spec,
        out_shape=[jax.ShapeDtypeStruct((N_TOK, D_MODEL), F32), jax.ShapeDtypeStruct((N_TOK, D_MODEL), BF16)],
        compiler_params=_params(("arbitrary",), 48),
        name="moe_combine_ln",
    )(n_chunks, chunk_ids, d0c, d1c, x, ys, lnp, lnp)


def _token_rows(a, fill):
    pad = TOK_CHUNKS * MOE_SUB - N_TOK
    return jnp.pad(a, (0, pad), constant_values=fill).reshape(TOK_CHUNKS, 1, MOE_SUB)


def _moe_layer(x, xb, w_router_padded, w_gu, w_down, j, lnp, layer):
    sel, gates = _router(x, w_router_padded, j)
    sel = sel[:, :N_EXPERTS]
    gates = gates[:, :N_EXPERTS]

    cum = jnp.cumsum(sel, axis=0)
    rank = cum - sel
    counts = cum[-1]
    padded = (counts + MOE_BM - 1) // MOE_BM * MOE_BM
    pad_end = jnp.cumsum(padded)
    pad_start = pad_end - padded
    slot = pad_start[None, :] + rank
    chosen = sel > 0
    d0 = jnp.min(jnp.where(chosen, slot, MOE_SLOTS), axis=1).astype(jnp.int32)
    d1 = jnp.max(jnp.where(chosen, slot, -1), axis=1).astype(jnp.int32)
    g0 = jnp.sum(jnp.where(chosen & (slot == d0[:, None]), gates, 0.0), axis=1)
    g1 = jnp.sum(jnp.where(chosen & (slot == d1[:, None]), gates, 0.0), axis=1)

    def expert_of(first_slot):
        return jnp.minimum(jnp.searchsorted(pad_end, first_slot, side="right"), N_EXPERTS - 1).astype(jnp.int32)

    tile_start = jnp.arange(MOE_TILES, dtype=jnp.int32) * MOE_BM
    block_e = expert_of(tile_start)
    valid_rows = jnp.clip(counts[block_e] - (tile_start - pad_start[block_e]), 0, MOE_BM).astype(jnp.int32)

    sub_start = jnp.arange(MOE_SLOTS // MOE_SUB, dtype=jnp.int32) * MOE_SUB
    sub_e = expert_of(sub_start)
    r0 = sub_start - pad_start[sub_e]
    r1 = jnp.minimum(r0 + MOE_SUB, counts[sub_e])
    cum_rows = cum.T[sub_e]
    tok_lo = jnp.sum(cum_rows < (r0 + 1)[:, None], axis=1)
    tok_hi = jnp.sum(cum_rows < r1[:, None], axis=1)
    nonempty = r1 > r0
    chunk_lo = jnp.where(nonempty, tok_lo // MOE_SUB, 1).astype(jnp.int32)
    chunk_hi = jnp.where(nonempty, tok_hi // MOE_SUB, 0).astype(jnp.int32)

    rank_edges = jnp.concatenate([jnp.zeros((1, N_EXPERTS), cum.dtype), cum[BM - 1::BM]], axis=0)
    lo = pad_start[None, :] + rank_edges[:-1]
    hi = pad_start[None, :] + rank_edges[1:]
    per_expert = COMBINE_KMAX // N_EXPERTS
    cand = (lo // MOE_SUB)[:, :, None] + jnp.arange(per_expert)[None, None, :]
    ok = (hi > lo)[:, :, None] & (cand <= ((hi - 1) // MOE_SUB)[:, :, None])
    cand = cand.reshape(N_TILES, COMBINE_KMAX)
    ok = ok.reshape(N_TILES, COMBINE_KMAX)
    order = jnp.argsort(jnp.logical_not(ok), axis=1, stable=True)
    chunk_ids = jnp.where(jnp.take_along_axis(ok, order, axis=1), jnp.take_along_axis(cand, order, axis=1), 0)
    chunk_ids = chunk_ids.astype(jnp.int32).reshape(-1)
    n_chunks = jnp.sum(ok, axis=1).astype(jnp.int32)

    xs, gs = _dispatch(xb, chunk_lo, chunk_hi, _token_rows(d0, -1), _token_rows(d1, -1),
                       _token_rows(g0, 0.0), _token_rows(g1, 0.0))
    h = _moe_gu(xs, w_gu, j, block_e, valid_rows)
    ys = _moe_down(h, w_down, j, gs, block_e, valid_rows)
    return _combine_ln(x, ys, n_chunks, chunk_ids, d0[:, None], d1[:, None], lnp, layer)


def _block_diag_causal(w_s):
    small = jnp.tril(w_s[:, :DEC_SEQ, :DEC_SEQ])
    eye = jnp.eye(DEC_BATCH, dtype=w_s.dtype)
    return jnp.einsum("ab,gij->gaibj", eye, small).reshape(SGU_GROUPS, CHUNK, CHUNK)


def kernel(x_prompt, x_sample, state_swa_k, state_swa_v, state_conv, ln1_g, ln1_b, ln2_g, ln2_b, a_w_in, a_ln_g, a_ln_b, a_w_s, a_b_s, a_w_out, b_w_qkv, b_sinks, b_w_o, c_w_in, c_w_conv, c_w_out, f_w_gu, f_w_down, m_w_router, m_w_gu, m_w_down):
    x = jnp.concatenate([x_prompt.reshape(N_PROMPT, D_MODEL), x_sample.reshape(N_SAMPLE, D_MODEL)], axis=0)
    xb = x.astype(BF16)
    lnp = jnp.concatenate([ln1_g, ln1_b, ln2_g, ln2_b], axis=0).reshape(4 * DEPTH, 1, D_MODEL)
    a_lnp = jnp.concatenate([a_ln_g, a_ln_b], axis=0).reshape(-1, 1, D_MODEL)
    a_w_in_b, a_w_out_b = a_w_in.astype(BF16), a_w_out.astype(BF16)
    b_w_qkv_b, b_w_o_b = b_w_qkv.astype(BF16), b_w_o.astype(BF16)
    c_w_in_b, c_w_out_b = c_w_in.astype(BF16), c_w_out.astype(BF16)
    f_w_gu_b, f_w_down_b = f_w_gu.astype(BF16), f_w_down.astype(BF16)
    w_router_padded = jnp.pad(m_w_router, ((0, 0), (0, 0), (0, 128 - N_EXPERTS)))

    sgu_v, swa_states, conv_states = [], [], []
    for i in range(DEPTH):
        kind, j = i % N_MIXERS, i // N_MIXERS
        if kind == 0:
            z2 = _sgu_in(xb, a_w_in_b, a_lnp, j)
            sgu_v.append(z2[N_PROMPT:, D_MODEL:].reshape(DEC_BATCH, DEC_SEQ, D_MODEL))
            b_t = a_b_s[j].T
            b_t_sample = jnp.tile(b_t[:DEC_SEQ], (DEC_BATCH, 1))
            x, xb = _sgu_out(z2, a_w_s, _block_diag_causal(a_w_s[j]), b_t, b_t_sample, a_w_out_b, j, x, lnp, i)
        elif kind == 1:
            x, xb, st = _swa_mixer(xb, x, state_swa_k[j], state_swa_v[j], b_w_qkv_b, b_sinks[j], b_w_o_b, j, lnp, i)
            swa_states.append(st)
        else:
            x, xb, st = _conv_mixer(xb, x, state_conv[j], c_w_in_b, c_w_conv, c_w_out_b, j, lnp, i)
            conv_states.append(st)
        if i % 2 == 0:
            h = _dense_gu(xb, f_w_gu_b, i // 2)
            x, xb = _mm_ln(h, f_w_down_b, i // 2, x, lnp, 1, i, 320, "dense_down")
        else:
            x, xb = _moe_layer(x, xb, w_router_padded, m_w_gu, m_w_down, i // 2, lnp, i)

    y_prompt = x[:N_PROMPT].reshape(BATCH, SEQ, D_MODEL)
    y_sample = x[N_PROMPT:].reshape(DEC_BATCH, DEC_SEQ, D_MODEL)
    return (y_prompt, y_sample, jnp.stack(sgu_v),
            jnp.stack([s[0] for s in swa_states]), jnp.stack([s[1] for s in swa_states]),
            jnp.stack([s[2] for s in swa_states]), jnp.stack([s[3] for s in swa_states]),
            jnp.stack([s[0] for s in conv_states]), jnp.stack([s[1] for s in conv_states]))
```

```python
import functools

import numpy as np
import jax
import jax.numpy as jnp
from jax import lax
from jax.experimental import pallas as pl
from jax.experimental.pallas import tpu as pltpu

D_MODEL = 2048
BATCH = 2
SEQ = 4096
DEPTH = 4
DEC_BATCH = 32
DEC_SEQ = 4
N_MIXERS = 3
CHUNK = 128
SGU_GROUPS = 8
SGU_GROUP_DIM = D_MODEL // SGU_GROUPS
HEAD_DIM = 64
N_HEADS = D_MODEL // HEAD_DIM
N_KV_HEADS = N_HEADS // 8
GQA_GROUP = N_HEADS // N_KV_HEADS
WINDOW = 128
CONV_WIDTH = 3
D_FF = 5632
N_EXPERTS = 8
LN_EPS = 1e-5
DEEPNORM_ALPHA = (2.0 * DEPTH) ** 0.25
NEG_INF = -1e30

N_PROMPT = BATCH * SEQ
N_SAMPLE = DEC_BATCH * DEC_SEQ
N_TOK = N_PROMPT + N_SAMPLE
BM = 640
N_TILES = N_TOK // BM
SGU_BM = 128
SGU_TILES = N_TOK // SGU_BM
CHUNKS_PER_TILE = SGU_BM // CHUNK

MOE_SUB = 256
MOE_PAD = 4 * MOE_SUB
MOE_SLOTS = (-(-(2 * N_TOK) // MOE_PAD) + N_EXPERTS) * MOE_PAD
MOE_GU_BM = MOE_PAD
MOE_DOWN_BM = 2 * MOE_SUB
MOE_BN = 512
COMBINE_GROUP = 4
TOK_CHUNKS_FULL = N_TOK // MOE_SUB
TOK_TAIL = N_TOK - TOK_CHUNKS_FULL * MOE_SUB
TOK_CHUNKS = TOK_CHUNKS_FULL + 1
COMBINE_KMAX = N_EXPERTS * ((MOE_SUB - 1 + BM - 1) // MOE_SUB + 1)

F32 = jnp.float32
BF16 = jnp.bfloat16
MIB = 1024 * 1024


def _params(semantics, vmem_mib):
    return pltpu.CompilerParams(dimension_semantics=semantics, vmem_limit_bytes=vmem_mib * MIB)


def _layer_norm(y, g, b):
    mu = jnp.mean(y, axis=-1, keepdims=True)
    d = y - mu
    var = jnp.mean(d * d, axis=-1, keepdims=True)
    return d * lax.rsqrt(var + LN_EPS) * g + b


def _gelu(x):
    return 0.5 * x * (1.0 + lax.erf(x * np.float32(np.sqrt(0.5))))


def _dot(a, b):
    return jnp.dot(a, b, preferred_element_type=F32)


def _ln_specs(norm, layer, ngrid):
    def spec(kind):
        r = (2 * norm + kind) * DEPTH + layer
        if ngrid == 1:
            return pl.BlockSpec((None, 1, D_MODEL), lambda i: (r, 0, 0))
        return pl.BlockSpec((None, 1, D_MODEL), lambda j, i: (r, 0, 0))
    return [spec(0), spec(1)]


def _mm_kernel(x_ref, w_ref, o_ref):
    o_ref[...] = _dot(x_ref[...], w_ref[...]).astype(o_ref.dtype)


def _mm(x, w, layer, bn, out_dtype, name):
    m, k = x.shape
    n = w.shape[-1]
    return pl.pallas_call(
        _mm_kernel,
        grid=(n // bn, m // BM),
        in_specs=[pl.BlockSpec((BM, k), lambda j, i: (i, 0)),
                  pl.BlockSpec((None, k, bn), lambda j, i: (layer, 0, j))],
        out_specs=pl.BlockSpec((BM, bn), lambda j, i: (i, j)),
        out_shape=jax.ShapeDtypeStruct((m, n), out_dtype),
        compiler_params=_params(("arbitrary", "arbitrary"), 48),
        name=name,
    )(x, w)


def _mm_ln_kernel(x_ref, w_ref, r_ref, g_ref, b_ref, of_ref, ob_ref):
    f = _dot(x_ref[...], w_ref[...])
    y = _layer_norm(DEEPNORM_ALPHA * r_ref[...] + f, g_ref[...], b_ref[...])
    of_ref[...] = y
    ob_ref[...] = y.astype(BF16)


def _mm_ln(x, w, w_layer, resid, lnp, norm, layer, bm, name):
    m, k = x.shape
    row = lambda i: (i, 0)
    return pl.pallas_call(
        _mm_ln_kernel,
        grid=(m // bm,),
        in_specs=[pl.BlockSpec((bm, k), row),
                  pl.BlockSpec((None, k, D_MODEL), lambda i: (w_layer, 0, 0), pipeline_mode=pl.Buffered(1)),
                  pl.BlockSpec((bm, D_MODEL), row)] + _ln_specs(norm, layer, 1),
        out_specs=[pl.BlockSpec((bm, D_MODEL), row), pl.BlockSpec((bm, D_MODEL), row)],
        out_shape=[jax.ShapeDtypeStruct((m, D_MODEL), F32), jax.ShapeDtypeStruct((m, D_MODEL), BF16)],
        compiler_params=_params(("arbitrary",), 56),
        name=name,
    )(x, w, resid, lnp, lnp)


def _sgu_in_kernel(x_ref, w_ref, g_ref, b_ref, o_ref):
    z = _gelu(_dot(x_ref[...], w_ref[...]))

    @pl.when(pl.program_id(0) == 0)
    def _():
        o_ref[...] = z

    @pl.when(pl.program_id(0) == 1)
    def _():
        o_ref[...] = _layer_norm(z, g_ref[...], b_ref[...])


def _sgu_in(xb, w_in, a_lnp, j):
    return pl.pallas_call(
        _sgu_in_kernel,
        grid=(2, N_TILES),
        in_specs=[pl.BlockSpec((BM, D_MODEL), lambda c, i: (i, 0)),
                  pl.BlockSpec((None, D_MODEL, D_MODEL), lambda c, i: (j, 0, c)),
                  pl.BlockSpec((None, 1, D_MODEL), lambda c, i: (j, 0, 0)),
                  pl.BlockSpec((None, 1, D_MODEL), lambda c, i: (2 + j, 0, 0))],
        out_specs=pl.BlockSpec((BM, D_MODEL), lambda c, i: (i, c)),
        out_shape=jax.ShapeDtypeStruct((N_TOK, 2 * D_MODEL), F32),
        compiler_params=_params(("arbitrary", "arbitrary"), 56),
        name="sgu_in",
    )(xb, w_in, a_lnp, a_lnp)


def _sgu_out_kernel(u_ref, v_ref, wp_ref, ws_ref, bp_ref, bs_ref, wo_ref, r_ref, g_ref, b_ref,
                    of_ref, ob_ref, gated_ref):
    is_sample_tile = pl.program_id(0) == SGU_TILES - 1
    row = lax.broadcasted_iota(jnp.int32, (CHUNK, CHUNK), 0)
    col = lax.broadcasted_iota(jnp.int32, (CHUNK, CHUNK), 1)
    causal = row >= col
    for grp in range(SGU_GROUPS):
        cols = slice(grp * SGU_GROUP_DIM, (grp + 1) * SGU_GROUP_DIM)
        w_prompt = jnp.where(causal, wp_ref[grp], 0.0)
        bias_prompt = bp_ref[:, grp:grp + 1]
        w_last = jnp.where(is_sample_tile, ws_ref[grp], w_prompt).astype(BF16)
        bias_last = jnp.where(is_sample_tile, bs_ref[:, grp:grp + 1], bias_prompt)
        w_prompt = w_prompt.astype(BF16)
        for c in range(CHUNKS_PER_TILE):
            rows = slice(c * CHUNK, (c + 1) * CHUNK)
            last = c == CHUNKS_PER_TILE - 1
            mixed = _dot(w_last if last else w_prompt, v_ref[rows, cols].astype(BF16))
            mixed = mixed + (bias_last if last else bias_prompt)
            gated_ref[rows, cols] = (u_ref[rows, cols] * mixed).astype(BF16)
    f = _dot(gated_ref[...], wo_ref[...])
    y = _layer_norm(DEEPNORM_ALPHA * r_ref[...] + f, g_ref[...], b_ref[...])
    of_ref[...] = y
    ob_ref[...] = y.astype(BF16)


def _sgu_out(z2, w_s, w_s_sample, b_t, b_t_sample, w_out, j, resid, lnp, layer):
    row = lambda i: (i, 0)
    const2 = lambda i: (0, 0)
    const3 = lambda i: (0, 0, 0)
    return pl.pallas_call(
        _sgu_out_kernel,
        grid=(SGU_TILES,),
        in_specs=[pl.BlockSpec((SGU_BM, D_MODEL), lambda i: (i, 0)),
                  pl.BlockSpec((SGU_BM, D_MODEL), lambda i: (i, 1)),
                  pl.BlockSpec((None, SGU_GROUPS, CHUNK, CHUNK), lambda i: (j, 0, 0, 0)),
                  pl.BlockSpec((SGU_GROUPS, CHUNK, CHUNK), const3),
                  pl.BlockSpec((CHUNK, SGU_GROUPS), const2),
                  pl.BlockSpec((CHUNK, SGU_GROUPS), const2),
                  pl.BlockSpec((None, D_MODEL, D_MODEL), lambda i: (j, 0, 0), pipeline_mode=pl.Buffered(1)),
                  pl.BlockSpec((SGU_BM, D_MODEL), row)] + _ln_specs(0, layer, 1),
        out_specs=[pl.BlockSpec((SGU_BM, D_MODEL), row), pl.BlockSpec((SGU_BM, D_MODEL), row)],
        out_shape=[jax.ShapeDtypeStruct((N_TOK, D_MODEL), F32), jax.ShapeDtypeStruct((N_TOK, D_MODEL), BF16)],
        scratch_shapes=[pltpu.VMEM((SGU_BM, D_MODEL), BF16)],
        compiler_params=_params(("arbitrary",), 40),
        name="sgu_out",
    )(z2, z2, w_s, w_s_sample, b_t, b_t_sample, w_out, resid, lnp, lnp)


def _gu_kernel(x_ref, wg_ref, wu_ref, o_ref):
    x = x_ref[...]
    gate = _dot(x, wg_ref[...])
    up = _dot(x, wu_ref[...])
    o_ref[...] = (jax.nn.silu(gate) * up).astype(o_ref.dtype)


def _dense_gu(xb, w_gu, layer):
    bn = 1408
    nt = D_FF // bn
    return pl.pallas_call(
        _gu_kernel,
        grid=(nt, N_TILES),
        in_specs=[pl.BlockSpec((BM, D_MODEL), lambda j, i: (i, 0)),
                  pl.BlockSpec((None, D_MODEL, bn), lambda j, i: (layer, 0, j)),
                  pl.BlockSpec((None, D_MODEL, bn), lambda j, i: (layer, 0, j + nt))],
        out_specs=pl.BlockSpec((BM, bn), lambda j, i: (i, j)),
        out_shape=jax.ShapeDtypeStruct((N_TOK, D_FF), BF16),
        compiler_params=_params(("arbitrary", "arbitrary"), 56),
        name="dense_gu",
    )(xb, w_gu, w_gu)


def _attn_kernel(sink_ref, q_ref, kp_ref, vp_ref, kc_ref, vc_ref, o_ref, *, blocks_per_seq):
    has_prev = (pl.program_id(0) % blocks_per_seq) != 0
    qi = lax.broadcasted_iota(jnp.int32, (WINDOW, 2 * WINDOW), 0)
    kj = lax.broadcasted_iota(jnp.int32, (WINDOW, 2 * WINDOW), 1)
    delta_i = WINDOW + qi - kj
    valid = (delta_i >= 0) & (delta_i <= WINDOW) & (has_prev | (kj >= WINDOW))
    delta = delta_i.astype(F32)
    low_half = lax.broadcasted_iota(jnp.int32, (2 * WINDOW, 128), 1) < HEAD_DIM

    kk = jnp.concatenate([kp_ref[...], kc_ref[...]], axis=0)
    vv = jnp.concatenate([vp_ref[...], vc_ref[...]], axis=0)

    def halves(slab, kv):
        if kv % 2 == 0:
            lo = jnp.where(low_half, slab, 0.0)
            hi = jnp.where(low_half, 0.0, pltpu.roll(slab, HEAD_DIM, 1))
        else:
            lo = jnp.where(low_half, pltpu.roll(slab, HEAD_DIM, 1), 0.0)
            hi = jnp.where(low_half, 0.0, slab)
        return lo.astype(BF16), hi.astype(BF16)

    for kv in range(N_KV_HEADS):
        lanes = slice((kv // 2) * 128, (kv // 2 + 1) * 128)
        k_halves = halves(kk[:, lanes], kv)
        v_halves = halves(vv[:, lanes], kv)
        for pair in range(GQA_GROUP // 2):
            qcols = slice((kv * 4 + pair) * 128, (kv * 4 + pair + 1) * 128)
            q2 = q_ref[:, qcols].astype(BF16)
            out = None
            for half in range(2):
                h = kv * GQA_GROUP + 2 * pair + half
                slope = float(2.0 ** (-8.0 * (h + 1) / N_HEADS))
                s = lax.dot_general(q2, k_halves[half], (((1,), (1,)), ((), ())),
                                    preferred_element_type=F32) * (HEAD_DIM ** -0.5)
                s = s - slope * delta
                s = jnp.where(valid, s, NEG_INF)
                sink = sink_ref[h]
                mx = jnp.maximum(jnp.max(s, axis=-1, keepdims=True), sink)
                p = jnp.exp(s - mx)
                den = jnp.sum(p, axis=-1, keepdims=True) + jnp.exp(sink - mx)
                p = p / den
                o = _dot(p.astype(BF16), v_halves[half])
                out = o if out is None else out + o
            o_ref[:, qcols] = out.astype(o_ref.dtype)


def _attention(sinks, q, kp, vp, kc, vc, n_blocks, specs, blocks_per_seq, name):
    return pl.pallas_call(
        functools.partial(_attn_kernel, blocks_per_seq=blocks_per_seq),
        grid=(n_blocks,),
        in_specs=[pl.BlockSpec(memory_space=pltpu.SMEM)] + specs,
        out_specs=pl.BlockSpec((WINDOW, D_MODEL), lambda i: (i, 0)),
        out_shape=jax.ShapeDtypeStruct((n_blocks * WINDOW, D_MODEL), BF16),
        compiler_params=_params(("arbitrary",), 32),
        name=name,
    )(sinks, q, kp, vp, kc, vc)


SAMPLE_ROWS = GQA_GROUP * DEC_SEQ
SAMPLE_KEYS = 2 * WINDOW


def _attn_sample_kernel(q_ref, k_ref, v_ref, sink_ref, slope_ref, o_ref):
    step = lax.broadcasted_iota(jnp.int32, (SAMPLE_ROWS, SAMPLE_KEYS), 0) % DEC_SEQ
    kj = lax.broadcasted_iota(jnp.int32, (SAMPLE_ROWS, SAMPLE_KEYS), 1)
    delta_i = WINDOW + step - kj
    valid = (delta_i >= 0) & (delta_i <= WINDOW)
    delta = delta_i.astype(F32)
    for kv in range(N_KV_HEADS):
        lanes = slice((kv // 2) * 128, (kv // 2 + 1) * 128)
        s = lax.dot_general(q_ref[kv].astype(BF16), k_ref[:, lanes].astype(BF16), (((1,), (1,)), ((), ())),
                            preferred_element_type=F32) * (HEAD_DIM ** -0.5)
        s = s - slope_ref[kv] * delta
        s = jnp.where(valid, s, NEG_INF)
        sink = sink_ref[kv]
        mx = jnp.maximum(jnp.max(s, axis=-1, keepdims=True), sink)
        p = jnp.exp(s - mx)
        den = jnp.sum(p, axis=-1, keepdims=True) + jnp.exp(sink - mx)
        p = p / den
        o_ref[kv] = _dot(p.astype(BF16), v_ref[:, lanes].astype(BF16)).astype(o_ref.dtype)


def _attn_sample(q_s, k_new, v_new, state_k, state_v, sinks):
    kvw = N_KV_HEADS * HEAD_DIM
    q = q_s.reshape(DEC_BATCH, DEC_SEQ, N_KV_HEADS, GQA_GROUP, HEAD_DIM).transpose(0, 2, 3, 1, 4)
    q = q.reshape(DEC_BATCH, N_KV_HEADS, SAMPLE_ROWS, HEAD_DIM)
    zero = jnp.zeros_like(q)
    odd = (jnp.arange(N_KV_HEADS) % 2 == 1)[None, :, None, None]
    q = jnp.where(odd, jnp.concatenate([zero, q], axis=-1), jnp.concatenate([q, zero], axis=-1))

    def keys(state, new):
        pad = jnp.zeros((DEC_BATCH, SAMPLE_KEYS - WINDOW - DEC_SEQ, kvw), F32)
        return jnp.concatenate([state.reshape(DEC_BATCH, WINDOW, kvw), new, pad], axis=1)

    head = np.arange(N_KV_HEADS)[:, None] * GQA_GROUP + np.arange(SAMPLE_ROWS)[None, :] // DEC_SEQ
    slopes = jnp.asarray(np.exp2(-8.0 * (head + 1) / N_HEADS), F32)[:, :, None]
    sink_rows = sinks[head][:, :, None]
    seq3 = lambda i: (i, 0, 0)
    whole = lambda i: (0, 0, 0)
    out = pl.pallas_call(
        _attn_sample_kernel,
        grid=(DEC_BATCH,),
        in_specs=[pl.BlockSpec((None, N_KV_HEADS, SAMPLE_ROWS, 128), lambda i: (i, 0, 0, 0)),
                  pl.BlockSpec((None, SAMPLE_KEYS, kvw), seq3),
                  pl.BlockSpec((None, SAMPLE_KEYS, kvw), seq3),
                  pl.BlockSpec((N_KV_HEADS, SAMPLE_ROWS, 1), whole),
                  pl.BlockSpec((N_KV_HEADS, SAMPLE_ROWS, 1), whole)],
        out_specs=pl.BlockSpec((None, N_KV_HEADS, SAMPLE_ROWS, 128), lambda i: (i, 0, 0, 0)),
        out_shape=jax.ShapeDtypeStruct((DEC_BATCH, N_KV_HEADS, SAMPLE_ROWS, 128), BF16),
        compiler_params=_params(("arbitrary",), 16),
        name="attn_sample",
    )(q, keys(state_k, k_new), keys(state_v, v_new), sink_rows, slopes)
    out = jnp.where(odd, out[..., HEAD_DIM:], out[..., :HEAD_DIM])
    out = out.reshape(DEC_BATCH, N_KV_HEADS, GQA_GROUP, DEC_SEQ, HEAD_DIM).transpose(0, 3, 1, 2, 4)
    return out.reshape(N_SAMPLE, D_MODEL)


def _swa_mixer(xb, x, state_k, state_v, w_qkv, sinks, w_o, j, lnp, layer):
    kvw = N_KV_HEADS * HEAD_DIM
    qkv = _mm(xb, w_qkv, j, 1280, F32, "qkv")
    kcol, vcol = D_MODEL // kvw, D_MODEL // kvw + 1
    nb = SEQ // WINDOW

    def prev_block(i):
        return jnp.maximum(i - 1, 0)

    prompt_specs = [pl.BlockSpec((WINDOW, D_MODEL), lambda i: (i, 0)),
                    pl.BlockSpec((WINDOW, kvw), lambda i: (prev_block(i), kcol)),
                    pl.BlockSpec((WINDOW, kvw), lambda i: (prev_block(i), vcol)),
                    pl.BlockSpec((WINDOW, kvw), lambda i: (i, kcol)),
                    pl.BlockSpec((WINDOW, kvw), lambda i: (i, vcol))]
    o_prompt = _attention(sinks, qkv, qkv, qkv, qkv, qkv, N_PROMPT // WINDOW, prompt_specs, nb, "attn_prompt")

    qkv_s = qkv[N_PROMPT:].reshape(DEC_BATCH, DEC_SEQ, D_MODEL + 2 * kvw)
    k_new = qkv_s[..., D_MODEL:D_MODEL + kvw]
    v_new = qkv_s[..., D_MODEL + kvw:]
    o_sample = _attn_sample(qkv_s[..., :D_MODEL], k_new, v_new, state_k, state_v, sinks)
    o_all = jnp.concatenate([o_prompt, o_sample], axis=0)
    x, xb = _mm_ln(o_all, w_o, j, x, lnp, 0, layer, BM, "attn_out")

    k_p = qkv[:N_PROMPT, D_MODEL:D_MODEL + kvw].reshape(BATCH, SEQ, N_KV_HEADS, HEAD_DIM)[:, -WINDOW:]
    v_p = qkv[:N_PROMPT, D_MODEL + kvw:].reshape(BATCH, SEQ, N_KV_HEADS, HEAD_DIM)[:, -WINDOW:]
    k_s = jnp.concatenate([state_k, k_new.reshape(DEC_BATCH, DEC_SEQ, N_KV_HEADS, HEAD_DIM)], axis=1)[:, -WINDOW:]
    v_s = jnp.concatenate([state_v, v_new.reshape(DEC_BATCH, DEC_SEQ, N_KV_HEADS, HEAD_DIM)], axis=1)[:, -WINDOW:]
    return x, xb, (k_p, v_p, k_s, v_s)


CONV_BM = 512
CONV_BC = 1024


def _conv_taps(z, z1, z2, gb, wc_ref):
    conv = wc_ref[0:1, :] * z2 + wc_ref[1:2, :] * z1 + wc_ref[2:3, :] * z
    return (gb * conv).astype(BF16)


def _conv_prompt_kernel(gb_ref, gc_ref, h_ref, wc_ref, y_ref, z_ref, carry_ref):
    @pl.when(pl.program_id(1) % (SEQ // CONV_BM) == 0)
    def _():
        carry_ref[...] = jnp.zeros_like(carry_ref)

    z = gc_ref[...] * h_ref[...]
    row = lax.broadcasted_iota(jnp.int32, z.shape, 0)
    c6 = carry_ref[6:7, :]
    c7 = carry_ref[7:8, :]
    z1 = jnp.where(row >= 1, pltpu.roll(z, 1, 0), c7)
    z2 = jnp.where(row >= 2, pltpu.roll(z, 2, 0), jnp.where(row == 1, c7, c6))
    y_ref[...] = _conv_taps(z, z1, z2, gb_ref[...], wc_ref)
    z_ref[...] = z
    carry_ref[...] = z[CONV_BM - 8:, :]


def _conv_sample_kernel(gb_ref, gc_ref, h_ref, wc_ref, pa_ref, pb_ref, y_ref, z_ref):
    z = gc_ref[...] * h_ref[...]
    t = lax.broadcasted_iota(jnp.int32, z.shape, 0) % DEC_SEQ
    z1 = jnp.where(t >= 1, pltpu.roll(z, 1, 0), pa_ref[...])
    z2 = jnp.where(t >= 2, pltpu.roll(z, 2, 0), pb_ref[...])
    y_ref[...] = _conv_taps(z, z1, z2, gb_ref[...], wc_ref)
    z_ref[...] = z


def _conv_mixer(xb, x, state, w_in, w_conv, w_out, j, lnp, layer):
    proj = _mm(xb, w_in, j, 1024, F32, "conv_in")
    ncb = D_MODEL // CONV_BC
    y_p, z_p = pl.pallas_call(
        _conv_prompt_kernel,
        grid=(ncb, N_PROMPT // CONV_BM),
        in_specs=[pl.BlockSpec((CONV_BM, CONV_BC), lambda c, i: (i, c)),
                  pl.BlockSpec((CONV_BM, CONV_BC), lambda c, i: (i, ncb + c)),
                  pl.BlockSpec((CONV_BM, CONV_BC), lambda c, i: (i, 2 * ncb + c)),
                  pl.BlockSpec((None, CONV_WIDTH, CONV_BC), lambda c, i: (j, 0, c))],
        out_specs=[pl.BlockSpec((CONV_BM, CONV_BC), lambda c, i: (i, c)),
                   pl.BlockSpec((CONV_BM, CONV_BC), lambda c, i: (i, c))],
        out_shape=[jax.ShapeDtypeStruct((N_PROMPT, D_MODEL), BF16), jax.ShapeDtypeStruct((N_PROMPT, D_MODEL), F32)],
        scratch_shapes=[pltpu.VMEM((8, CONV_BC), F32)],
        compiler_params=_params(("arbitrary", "arbitrary"), 40),
        name="conv_prompt",
    )(proj, proj, proj, w_conv)

    zeros = jnp.zeros((DEC_BATCH, DEC_SEQ, D_MODEL), F32)
    past_a = zeros.at[:, 0].set(state[:, 1]).reshape(N_SAMPLE, D_MODEL)
    past_b = zeros.at[:, 0].set(state[:, 0]).at[:, 1].set(state[:, 1]).reshape(N_SAMPLE, D_MODEL)
    srow = N_PROMPT // N_SAMPLE
    whole = lambda i: (0, 0)
    y_s, z_s = pl.pallas_call(
        _conv_sample_kernel,
        grid=(1,),
        in_specs=[pl.BlockSpec((N_SAMPLE, D_MODEL), lambda i: (srow, 0)),
                  pl.BlockSpec((N_SAMPLE, D_MODEL), lambda i: (srow, 1)),
                  pl.BlockSpec((N_SAMPLE, D_MODEL), lambda i: (srow, 2)),
                  pl.BlockSpec((None, CONV_WIDTH, D_MODEL), lambda i: (j, 0, 0)),
                  pl.BlockSpec((N_SAMPLE, D_MODEL), whole),
                  pl.BlockSpec((N_SAMPLE, D_MODEL), whole)],
        out_specs=[pl.BlockSpec((N_SAMPLE, D_MODEL), whole), pl.BlockSpec((N_SAMPLE, D_MODEL), whole)],
        out_shape=[jax.ShapeDtypeStruct((N_SAMPLE, D_MODEL), BF16), jax.ShapeDtypeStruct((N_SAMPLE, D_MODEL), F32)],
        compiler_params=_params(("arbitrary",), 32),
        name="conv_sample",
    )(proj, proj, proj, w_conv, past_a, past_b)

    y_all = jnp.concatenate([y_p, y_s], axis=0)
    x, xb = _mm_ln(y_all, w_out, j, x, lnp, 0, layer, BM, "conv_out")
    conv_p = z_p.reshape(BATCH, SEQ, D_MODEL)[:, -(CONV_WIDTH - 1):]
    conv_s = z_s.reshape(DEC_BATCH, DEC_SEQ, D_MODEL)[:, -(CONV_WIDTH - 1):]
    return x, xb, (conv_p, conv_s)


def _router_kernel(x_ref, w_ref, sel_ref, gate_ref):
    logits = jnp.dot(x_ref[...], w_ref[...], preferred_element_type=F32, precision=lax.Precision.HIGHEST)
    lane = lax.broadcasted_iota(jnp.int32, logits.shape, 1)
    neg = jnp.float32(-jnp.inf)
    l1 = jnp.where(lane < N_EXPERTS, logits, neg)
    m1 = jnp.max(l1, axis=-1, keepdims=True)
    i1 = jnp.min(jnp.where(l1 == m1, lane, 128), axis=-1, keepdims=True)
    l2 = jnp.where(lane == i1, neg, l1)
    m2 = jnp.max(l2, axis=-1, keepdims=True)
    i2 = jnp.min(jnp.where(l2 == m2, lane, 128), axis=-1, keepdims=True)
    e2 = jnp.exp(m2 - m1)
    den = 1.0 + e2
    g1 = 1.0 / den
    g2 = e2 / den
    sel_ref[...] = ((lane == i1) | (lane == i2)).astype(jnp.int32)
    gate_ref[...] = jnp.where(lane == i1, g1, jnp.where(lane == i2, g2, 0.0))


def _router(x, w_router_padded, layer):
    row = lambda i: (i, 0)
    return pl.pallas_call(
        _router_kernel,
        grid=(N_TILES,),
        in_specs=[pl.BlockSpec((BM, D_MODEL), row),
                  pl.BlockSpec((None, D_MODEL, 128), lambda i: (layer, 0, 0))],
        out_specs=[pl.BlockSpec((BM, 128), row), pl.BlockSpec((BM, 128), row)],
        out_shape=[jax.ShapeDtypeStruct((N_TOK, 128), jnp.int32), jax.ShapeDtypeStruct((N_TOK, 128), F32)],
        compiler_params=_params(("arbitrary",), 32),
        name="router",
    )(x, w_router_padded)


def _dispatch_kernel(lo_ref, hi_ref, d0_ref, d1_ref, g0_ref, g1_ref, x_ref, xs_ref, gs_ref, acc_ref, gacc_ref):
    i = pl.program_id(0)
    lo = lo_ref[i]
    hi = hi_ref[i]
    slot = i * MOE_SUB + lax.broadcasted_iota(jnp.int32, (MOE_SUB, MOE_SUB), 0)
    acc_ref[...] = jnp.zeros_like(acc_ref)
    gacc_ref[...] = jnp.zeros_like(gacc_ref)

    def chunk(c, width, x_rows):
        s = slot[:, :width]
        m0 = s == d0_ref[c][:, :width]
        m1 = s == d1_ref[c][:, :width]
        onehot = jnp.where(m0 | m1, 1.0, 0.0).astype(BF16)
        acc_ref[...] += _dot(onehot, x_rows)
        picked = jnp.where(m0, g0_ref[c][:, :width], 0.0) + jnp.where(m1, g1_ref[c][:, :width], 0.0)
        gacc_ref[...] += jnp.sum(picked, axis=1, keepdims=True)

    def body(c, carry):
        start = pl.multiple_of(c * MOE_SUB, MOE_SUB)
        chunk(c, MOE_SUB, x_ref[pl.ds(start, MOE_SUB), :])
        return carry

    lax.fori_loop(lo, jnp.minimum(hi, TOK_CHUNKS_FULL - 1) + 1, body, 0)

    @pl.when(hi == TOK_CHUNKS_FULL)
    def _():
        chunk(TOK_CHUNKS_FULL, TOK_TAIL, x_ref[TOK_CHUNKS_FULL * MOE_SUB:, :])

    xs_ref[...] = acc_ref[...].astype(BF16)
    gs_ref[...] = gacc_ref[...]


def _dispatch(xb, chunk_lo, chunk_hi, d0r, d1r, g0r, g1r):
    n_tiles = MOE_SLOTS // MOE_SUB
    whole3 = lambda i, lo, hi: (0, 0, 0)
    rows = pl.BlockSpec((TOK_CHUNKS, 1, MOE_SUB), whole3)
    grid_spec = pltpu.PrefetchScalarGridSpec(
        num_scalar_prefetch=2,
        grid=(n_tiles,),
        in_specs=[rows, rows, rows, rows,
                  pl.BlockSpec((N_TOK, D_MODEL), lambda i, lo, hi: (0, 0))],
        out_specs=[pl.BlockSpec((MOE_SUB, D_MODEL), lambda i, lo, hi: (i, 0)),
                   pl.BlockSpec((MOE_SUB, 1), lambda i, lo, hi: (i, 0))],
        scratch_shapes=[pltpu.VMEM((MOE_SUB, D_MODEL), F32), pltpu.VMEM((MOE_SUB, 1), F32)],
    )
    return pl.pallas_call(
        _dispatch_kernel,
        grid_spec=grid_spec,
        out_shape=[jax.ShapeDtypeStruct((MOE_SLOTS, D_MODEL), BF16), jax.ShapeDtypeStruct((MOE_SLOTS, 1), F32)],
        compiler_params=_params(("arbitrary",), 52),
        name="moe_dispatch",
    )(chunk_lo, chunk_hi, d0r, d1r, g0r, g1r, xb)


def _expert_changed(be_ref, i):
    return (i == 0) | (be_ref[i] != be_ref[jnp.maximum(i - 1, 0)])


def _for_live_rows(n_sub, live_subs, o_ref, compute):
    for k in range(n_sub + 1):
        @pl.when(live_subs == k)
        def _(k=k):
            rows = k * MOE_SUB
            if k > 0:
                o_ref[:rows, :] = compute(rows)
            if k < n_sub:
                o_ref[rows:, :] = jnp.zeros((n_sub * MOE_SUB - rows, o_ref.shape[1]), o_ref.dtype)


def _moe_gu_kernel(be_ref, ls_ref, nt_ref, x_ref, wg_ref, wu_ref, o_ref, wgb_ref, wub_ref):
    i = pl.program_id(1)

    @pl.when(_expert_changed(be_ref, i) & (i < nt_ref[0]))
    def _():
        wgb_ref[...] = wg_ref[...].astype(BF16)
        wub_ref[...] = wu_ref[...].astype(BF16)

    def compute(rows):
        x = x_ref[:rows, :]
        gate = _dot(x, wgb_ref[...])
        up = _dot(x, wub_ref[...])
        return (jax.nn.silu(gate) * up).astype(o_ref.dtype)

    _for_live_rows(MOE_GU_BM // MOE_SUB, ls_ref[i], o_ref, compute)


def _live_tile(i, nt):
    return jnp.minimum(i, nt[0] - 1)


def _moe_gu(xs, w_gu, layer, block_e, live_subs, n_live):
    nt = D_FF // MOE_BN
    grid_spec = pltpu.PrefetchScalarGridSpec(
        num_scalar_prefetch=3,
        grid=(nt, MOE_SLOTS // MOE_GU_BM),
        in_specs=[pl.BlockSpec((MOE_GU_BM, D_MODEL), lambda j, i, be, ls, n: (_live_tile(i, n), 0)),
                  pl.BlockSpec((None, None, D_MODEL, MOE_BN),
                               lambda j, i, be, ls, n: (layer, be[_live_tile(i, n)], 0, j)),
                  pl.BlockSpec((None, None, D_MODEL, MOE_BN),
                               lambda j, i, be, ls, n: (layer, be[_live_tile(i, n)], 0, j + nt))],
        out_specs=pl.BlockSpec((MOE_GU_BM, MOE_BN), lambda j, i, be, ls, n: (i, j)),
        scratch_shapes=[pltpu.VMEM((D_MODEL, MOE_BN), BF16), pltpu.VMEM((D_MODEL, MOE_BN), BF16)],
    )
    return pl.pallas_call(
        _moe_gu_kernel,
        grid_spec=grid_spec,
        out_shape=jax.ShapeDtypeStruct((MOE_SLOTS, D_FF), BF16),
        compiler_params=_params(("arbitrary", "arbitrary"), 48),
        name="moe_gu",
    )(block_e, live_subs, n_live, xs, w_gu, w_gu)


def _moe_down_kernel(be_ref, ls_ref, nt_ref, h_ref, w_ref, gs_ref, o_ref, wb_ref):
    i = pl.program_id(1)

    @pl.when(_expert_changed(be_ref, i) & (i < nt_ref[0]))
    def _():
        wb_ref[...] = w_ref[...].astype(BF16)

    def compute(rows):
        return (_dot(h_ref[:rows, :], wb_ref[...]) * gs_ref[:rows, :]).astype(o_ref.dtype)

    _for_live_rows(MOE_DOWN_BM // MOE_SUB, ls_ref[i], o_ref, compute)


def _moe_down(h, w_down, layer, gs, block_e, live_subs, n_live):
    grid_spec = pltpu.PrefetchScalarGridSpec(
        num_scalar_prefetch=3,
        grid=(D_MODEL // MOE_BN, MOE_SLOTS // MOE_DOWN_BM),
        in_specs=[pl.BlockSpec((MOE_DOWN_BM, D_FF), lambda j, i, be, ls, n: (_live_tile(i, n), 0)),
                  pl.BlockSpec((None, None, D_FF, MOE_BN),
                               lambda j, i, be, ls, n: (layer, be[_live_tile(i, n)], 0, j)),
                  pl.BlockSpec((MOE_DOWN_BM, 1), lambda j, i, be, ls, n: (_live_tile(i, n), 0))],
        out_specs=pl.BlockSpec((MOE_DOWN_BM, MOE_BN), lambda j, i, be, ls, n: (i, j)),
        scratch_shapes=[pltpu.VMEM((D_FF, MOE_BN), BF16)],
    )
    return pl.pallas_call(
        _moe_down_kernel,
        grid_spec=grid_spec,
        out_shape=jax.ShapeDtypeStruct((MOE_SLOTS, D_MODEL), BF16),
        compiler_params=_params(("arbitrary", "arbitrary"), 52),
        name="moe_down",
    )(block_e, live_subs, n_live, h, w_down, gs)


def _combine_ln_kernel(ng_ref, ck_ref, d0_ref, d1_ref, x_ref, ys_ref, g_ref, b_ref, of_ref, ob_ref,
                       buf_ref, sem_ref, acc_ref):
    i = pl.program_id(0)
    n = ng_ref[i]

    def chunk_id(g, c):
        return ck_ref[i * COMBINE_KMAX + g * COMBINE_GROUP + c]

    def copies(g, which):
        out = []
        for c in range(COMBINE_GROUP):
            cid = chunk_id(g, c)
            start = pl.multiple_of(jnp.maximum(cid, 0) * MOE_SUB, MOE_SUB)
            out.append((cid >= 0, pltpu.make_async_copy(
                ys_ref.at[pl.ds(start, MOE_SUB)],
                buf_ref.at[which, pl.ds(c * MOE_SUB, MOE_SUB)],
                sem_ref.at[which])))
        return out

    def start_group(g, which):
        for present, copy in copies(g, which):
            pl.when(present)(copy.start)

    def wait_group(g, which):
        for present, copy in copies(g, which):
            pl.when(present)(copy.wait)

    @pl.when(i == 0)
    def _():
        buf_ref[...] = jnp.zeros_like(buf_ref)

    @pl.when(n > 0)
    def _():
        start_group(0, 0)

    acc_ref[...] = jnp.zeros_like(acc_ref)
    d0 = d0_ref[...]
    d1 = d1_ref[...]
    lane = lax.broadcasted_iota(jnp.int32, (BM, MOE_SUB), 1)

    def body(g, carry):
        which = g % 2
        wait_group(g, which)

        @pl.when(g + 1 < n)
        def _():
            start_group(g + 1, 1 - which)

        pieces = []
        for c in range(COMBINE_GROUP):
            slot = chunk_id(g, c) * MOE_SUB + lane
            pieces.append(jnp.where((d0 == slot) | (d1 == slot), 1.0, 0.0).astype(BF16))
        acc_ref[...] += _dot(jnp.concatenate(pieces, axis=1), buf_ref[which])
        return carry

    lax.fori_loop(0, n, body, 0)
    y = _layer_norm(DEEPNORM_ALPHA * x_ref[...] + acc_ref[...], g_ref[...], b_ref[...])
    of_ref[...] = y
    ob_ref[...] = y.astype(BF16)


def _combine_ln(x, ys, n_groups, chunk_ids, d0c, d1c, lnp, layer):
    row = lambda i, ng, ck: (i, 0)

    def ln_spec(kind):
        r = (2 + kind) * DEPTH + layer
        return pl.BlockSpec((None, 1, D_MODEL), lambda i, ng, ck: (r, 0, 0))

    grid_spec = pltpu.PrefetchScalarGridSpec(
        num_scalar_prefetch=2,
        grid=(N_TILES,),
        in_specs=[pl.BlockSpec((BM, 1), row),
                  pl.BlockSpec((BM, 1), row),
                  pl.BlockSpec((BM, D_MODEL), row),
                  pl.BlockSpec(memory_space=pl.ANY),
                  ln_spec(0), ln_spec(1)],
        out_specs=[pl.BlockSpec((BM, D_MODEL), row), pl.BlockSpec((BM, D_MODEL), row)],
        scratch_shapes=[pltpu.VMEM((2, COMBINE_GROUP * MOE_SUB, D_MODEL), BF16),
                        pltpu.SemaphoreType.DMA((2,)),
                        pltpu.VMEM((BM, D_MODEL), F32)],
    )
    return pl.pallas_call(
        _combine_ln_kernel,
        grid_spec=grid_spec,
        out_shape=[jax.ShapeDtypeStruct((N_TOK, D_MODEL), F32), jax.ShapeDtypeStruct((N_TOK, D_MODEL), BF16)],
        compiler_params=_params(("arbitrary",), 56),
        name="moe_combine_ln",
    )(n_groups, chunk_ids, d0c, d1c, x, ys, lnp, lnp)


def _token_rows(a, fill):
    pad = TOK_CHUNKS * MOE_SUB - N_TOK
    return jnp.pad(a, (0, pad), constant_values=fill).reshape(TOK_CHUNKS, 1, MOE_SUB)


def _moe_layer(x, xb, w_router_padded, w_gu, w_down, j, lnp, layer):
    sel, gates = _router(x, w_router_padded, j)
    sel = sel[:, :N_EXPERTS]
    gates = gates[:, :N_EXPERTS]

    cum = jnp.cumsum(sel, axis=0)
    rank = cum - sel
    counts = cum[-1]
    padded = (counts + MOE_PAD - 1) // MOE_PAD * MOE_PAD
    pad_end = jnp.cumsum(padded)
    pad_start = pad_end - padded
    slot = pad_start[None, :] + rank
    chosen = sel > 0
    d0 = jnp.min(jnp.where(chosen, slot, MOE_SLOTS), axis=1).astype(jnp.int32)
    d1 = jnp.max(jnp.where(chosen, slot, -1), axis=1).astype(jnp.int32)
    g0 = jnp.sum(jnp.where(chosen & (slot == d0[:, None]), gates, 0.0), axis=1)
    g1 = jnp.sum(jnp.where(chosen & (slot == d1[:, None]), gates, 0.0), axis=1)

    def tiles(rows_per_tile):
        start = jnp.arange(MOE_SLOTS // rows_per_tile, dtype=jnp.int32) * rows_per_tile
        e = jnp.minimum(jnp.sum(pad_end[None, :] <= start[:, None], axis=1), N_EXPERTS - 1).astype(jnp.int32)
        r0 = start - pad_start[e]
        valid = jnp.clip(counts[e] - r0, 0, rows_per_tile)
        return e, r0, valid

    def matmul_tiles(rows_per_tile):
        e, _, valid = tiles(rows_per_tile)
        live_subs = ((valid + MOE_SUB - 1) // MOE_SUB).astype(jnp.int32)
        n_live = (pad_end[-1:] // rows_per_tile).astype(jnp.int32)
        return e, live_subs, n_live

    sub_e, r0, valid = tiles(MOE_SUB)
    cum_rows = cum.T[sub_e]
    tok_lo = jnp.sum(cum_rows < (r0 + 1)[:, None], axis=1)
    tok_hi = jnp.sum(cum_rows < (r0 + valid)[:, None], axis=1)
    chunk_lo = jnp.where(valid > 0, tok_lo // MOE_SUB, 1).astype(jnp.int32)
    chunk_hi = jnp.where(valid > 0, tok_hi // MOE_SUB, 0).astype(jnp.int32)

    rank_edges = jnp.concatenate([jnp.zeros((1, N_EXPERTS), cum.dtype), cum[BM - 1::BM]], axis=0)
    lo = pad_start[None, :] + rank_edges[:-1]
    hi = pad_start[None, :] + rank_edges[1:]
    per_expert = COMBINE_KMAX // N_EXPERTS
    cand = (lo // MOE_SUB)[:, :, None] + jnp.arange(per_expert)[None, None, :]
    ok = (hi > lo)[:, :, None] & (cand <= ((hi - 1) // MOE_SUB)[:, :, None])
    cand = cand.reshape(N_TILES, COMBINE_KMAX)
    ok = ok.reshape(N_TILES, COMBINE_KMAX)
    order = jnp.argsort(jnp.logical_not(ok), axis=1, stable=True)
    chunk_ids = jnp.where(jnp.take_along_axis(ok, order, axis=1), jnp.take_along_axis(cand, order, axis=1), -1)
    chunk_ids = chunk_ids.astype(jnp.int32).reshape(-1)
    n_groups = ((jnp.sum(ok, axis=1) + COMBINE_GROUP - 1) // COMBINE_GROUP).astype(jnp.int32)

    xs, gs = _dispatch(xb, chunk_lo, chunk_hi, _token_rows(d0, -1), _token_rows(d1, -1),
                       _token_rows(g0, 0.0), _token_rows(g1, 0.0))
    h = _moe_gu(xs, w_gu, j, *matmul_tiles(MOE_GU_BM))
    ys = _moe_down(h, w_down, j, gs, *matmul_tiles(MOE_DOWN_BM))
    return _combine_ln(x, ys, n_groups, chunk_ids, d0[:, None], d1[:, None], lnp, layer)


def _block_diag_causal(w_s):
    small = jnp.tril(w_s[:, :DEC_SEQ, :DEC_SEQ])
    eye = jnp.eye(DEC_BATCH, dtype=w_s.dtype)
    return jnp.einsum("ab,gij->gaibj", eye, small).reshape(SGU_GROUPS, CHUNK, CHUNK)


def kernel(x_prompt, x_sample, state_swa_k, state_swa_v, state_conv, ln1_g, ln1_b, ln2_g, ln2_b, a_w_in, a_ln_g, a_ln_b, a_w_s, a_b_s, a_w_out, b_w_qkv, b_sinks, b_w_o, c_w_in, c_w_conv, c_w_out, f_w_gu, f_w_down, m_w_router, m_w_gu, m_w_down):
    x = jnp.concatenate([x_prompt.reshape(N_PROMPT, D_MODEL), x_sample.reshape(N_SAMPLE, D_MODEL)], axis=0)
    xb = x.astype(BF16)
    lnp = jnp.concatenate([ln1_g, ln1_b, ln2_g, ln2_b], axis=0).reshape(4 * DEPTH, 1, D_MODEL)
    a_lnp = jnp.concatenate([a_ln_g, a_ln_b], axis=0).reshape(-1, 1, D_MODEL)
    a_w_in_b, a_w_out_b = a_w_in.astype(BF16), a_w_out.astype(BF16)
    b_w_qkv_b, b_w_o_b = b_w_qkv.astype(BF16), b_w_o.astype(BF16)
    c_w_in_b, c_w_out_b = c_w_in.astype(BF16), c_w_out.astype(BF16)
    f_w_gu_b, f_w_down_b = f_w_gu.astype(BF16), f_w_down.astype(BF16)
    w_router_padded = jnp.pad(m_w_router, ((0, 0), (0, 0), (0, 128 - N_EXPERTS)))

    sgu_v, swa_states, conv_states = [], [], []
    for i in range(DEPTH):
        kind, j = i % N_MIXERS, i // N_MIXERS
        if kind == 0:
            z2 = _sgu_in(xb, a_w_in_b, a_lnp, j)
            sgu_v.append(z2[N_PROMPT:, D_MODEL:].reshape(DEC_BATCH, DEC_SEQ, D_MODEL))
            b_t = a_b_s[j].T
            b_t_sample = jnp.tile(b_t[:DEC_SEQ], (DEC_BATCH, 1))
            x, xb = _sgu_out(z2, a_w_s, _block_diag_causal(a_w_s[j]), b_t, b_t_sample, a_w_out_b, j, x, lnp, i)
        elif kind == 1:
            x, xb, st = _swa_mixer(xb, x, state_swa_k[j], state_swa_v[j], b_w_qkv_b, b_sinks[j], b_w_o_b, j, lnp, i)
            swa_states.append(st)
        else:
            x, xb, st = _conv_mixer(xb, x, state_conv[j], c_w_in_b, c_w_conv, c_w_out_b, j, lnp, i)
            conv_states.append(st)
        if i % 2 == 0:
            h = _dense_gu(xb, f_w_gu_b, i // 2)
            x, xb = _mm_ln(h, f_w_down_b, i // 2, x, lnp, 1, i, 320, "dense_down")
        else:
            x, xb = _moe_layer(x, xb, w_router_padded, m_w_gu, m_w_down, i // 2, lnp, i)

    y_prompt = x[:N_PROMPT].reshape(BATCH, SEQ, D_MODEL)
    y_sample = x[N_PROMPT:].reshape(DEC_BATCH, DEC_SEQ, D_MODEL)
    return (y_prompt, y_sample, jnp.stack(sgu_v),
            jnp.stack([s[0] for s in swa_states]), jnp.stack([s[1] for s in swa_states]),
            jnp.stack([s[2] for s in swa_states]), jnp.stack([s[3] for s in swa_states]),
            jnp.stack([s[0] for s in conv_states]), jnp.stack([s[1] for s in conv_states]))
```

```python
import functools

import numpy as np
import jax
import jax.numpy as jnp
from jax import lax
from jax.experimental import pallas as pl
from jax.experimental.pallas import tpu as pltpu

D_MODEL = 2048
BATCH = 2
SEQ = 4096
DEPTH = 4
DEC_BATCH = 32
DEC_SEQ = 4
N_MIXERS = 3
CHUNK = 128
SGU_GROUPS = 8
SGU_GROUP_DIM = D_MODEL // SGU_GROUPS
HEAD_DIM = 64
N_HEADS = D_MODEL // HEAD_DIM
N_KV_HEADS = N_HEADS // 8
GQA_GROUP = N_HEADS // N_KV_HEADS
WINDOW = 128
CONV_WIDTH = 3
D_FF = 5632
N_EXPERTS = 8
LN_EPS = 1e-5
DEEPNORM_ALPHA = (2.0 * DEPTH) ** 0.25
NEG_INF = -1e30

N_PROMPT = BATCH * SEQ
N_SAMPLE = DEC_BATCH * DEC_SEQ
N_TOK = N_PROMPT + N_SAMPLE
BM = 640
N_TILES = N_TOK // BM
SGU_BM = 128
SGU_TILES = N_TOK // SGU_BM
CHUNKS_PER_TILE = SGU_BM // CHUNK

MOE_SUB = 256
MOE_PAD = 4 * MOE_SUB
MOE_SLOTS = (-(-(2 * N_TOK) // MOE_PAD) + N_EXPERTS) * MOE_PAD
MOE_GU_BM = MOE_PAD
MOE_DOWN_BM = 2 * MOE_SUB
MOE_BN = 512
COMBINE_GROUP = 4
TOK_CHUNK = 512
TOK_CHUNKS_FULL = N_TOK // TOK_CHUNK
TOK_TAIL = N_TOK - TOK_CHUNKS_FULL * TOK_CHUNK
TOK_CHUNKS = TOK_CHUNKS_FULL + 1
COMBINE_KMAX = N_EXPERTS * ((MOE_SUB - 1 + BM - 1) // MOE_SUB + 1)

F32 = jnp.float32
BF16 = jnp.bfloat16
MIB = 1024 * 1024


def _params(semantics, vmem_mib):
    return pltpu.CompilerParams(dimension_semantics=semantics, vmem_limit_bytes=vmem_mib * MIB)


def _layer_norm(y, g, b):
    mu = jnp.mean(y, axis=-1, keepdims=True)
    d = y - mu
    var = jnp.mean(d * d, axis=-1, keepdims=True)
    return d * lax.rsqrt(var + LN_EPS) * g + b


def _gelu(x):
    return 0.5 * x * (1.0 + lax.erf(x * np.float32(np.sqrt(0.5))))


def _dot(a, b):
    return jnp.dot(a, b, preferred_element_type=F32)


def _ln_specs(norm, layer, ngrid):
    def spec(kind):
        r = (2 * norm + kind) * DEPTH + layer
        if ngrid == 1:
            return pl.BlockSpec((None, 1, D_MODEL), lambda i: (r, 0, 0))
        return pl.BlockSpec((None, 1, D_MODEL), lambda j, i: (r, 0, 0))
    return [spec(0), spec(1)]


def _mm_kernel(x_ref, w_ref, o_ref):
    o_ref[...] = _dot(x_ref[...], w_ref[...]).astype(o_ref.dtype)


def _mm(x, w, layer, bn, out_dtype, name):
    m, k = x.shape
    n = w.shape[-1]
    return pl.pallas_call(
        _mm_kernel,
        grid=(n // bn, m // BM),
        in_specs=[pl.BlockSpec((BM, k), lambda j, i: (i, 0)),
                  pl.BlockSpec((None, k, bn), lambda j, i: (layer, 0, j))],
        out_specs=pl.BlockSpec((BM, bn), lambda j, i: (i, j)),
        out_shape=jax.ShapeDtypeStruct((m, n), out_dtype),
        compiler_params=_params(("arbitrary", "arbitrary"), 48),
        name=name,
    )(x, w)


def _mm_ln_kernel(x_ref, w_ref, r_ref, g_ref, b_ref, of_ref, ob_ref):
    bm = x_ref.shape[0]
    group = bm // 2
    for start in range(0, bm, group):
        rows = slice(start, start + group)
        f = _dot(x_ref[rows, :], w_ref[...])
        y = _layer_norm(DEEPNORM_ALPHA * r_ref[rows, :] + f, g_ref[...], b_ref[...])
        of_ref[rows, :] = y
        ob_ref[rows, :] = y.astype(BF16)


def _mm_ln(x, w, w_layer, resid, lnp, norm, layer, bm, name):
    m, k = x.shape
    row = lambda i: (i, 0)
    return pl.pallas_call(
        _mm_ln_kernel,
        grid=(m // bm,),
        in_specs=[pl.BlockSpec((bm, k), row),
                  pl.BlockSpec((None, k, D_MODEL), lambda i: (w_layer, 0, 0), pipeline_mode=pl.Buffered(1)),
                  pl.BlockSpec((bm, D_MODEL), row)] + _ln_specs(norm, layer, 1),
        out_specs=[pl.BlockSpec((bm, D_MODEL), row), pl.BlockSpec((bm, D_MODEL), row)],
        out_shape=[jax.ShapeDtypeStruct((m, D_MODEL), F32), jax.ShapeDtypeStruct((m, D_MODEL), BF16)],
        compiler_params=_params(("arbitrary",), 56),
        name=name,
    )(x, w, resid, lnp, lnp)


def _sgu_in_kernel(x_ref, w_ref, g_ref, b_ref, o_ref):
    z = _gelu(_dot(x_ref[...], w_ref[...]))

    @pl.when(pl.program_id(0) == 0)
    def _():
        o_ref[...] = z

    @pl.when(pl.program_id(0) == 1)
    def _():
        o_ref[...] = _layer_norm(z, g_ref[...], b_ref[...])


def _sgu_in(xb, w_in, a_lnp, j):
    return pl.pallas_call(
        _sgu_in_kernel,
        grid=(2, N_TILES),
        in_specs=[pl.BlockSpec((BM, D_MODEL), lambda c, i: (i, 0)),
                  pl.BlockSpec((None, D_MODEL, D_MODEL), lambda c, i: (j, 0, c)),
                  pl.BlockSpec((None, 1, D_MODEL), lambda c, i: (j, 0, 0)),
                  pl.BlockSpec((None, 1, D_MODEL), lambda c, i: (2 + j, 0, 0))],
        out_specs=pl.BlockSpec((BM, D_MODEL), lambda c, i: (i, c)),
        out_shape=jax.ShapeDtypeStruct((N_TOK, 2 * D_MODEL), F32),
        compiler_params=_params(("arbitrary", "arbitrary"), 56),
        name="sgu_in",
    )(xb, w_in, a_lnp, a_lnp)


def _sgu_out_kernel(u_ref, v_ref, wp_ref, ws_ref, bp_ref, bs_ref, wo_ref, r_ref, g_ref, b_ref,
                    of_ref, ob_ref, gated_ref):
    is_sample_tile = pl.program_id(0) == SGU_TILES - 1
    row = lax.broadcasted_iota(jnp.int32, (CHUNK, CHUNK), 0)
    col = lax.broadcasted_iota(jnp.int32, (CHUNK, CHUNK), 1)
    causal = row >= col
    for grp in range(SGU_GROUPS):
        cols = slice(grp * SGU_GROUP_DIM, (grp + 1) * SGU_GROUP_DIM)
        w_prompt = jnp.where(causal, wp_ref[grp], 0.0)
        bias_prompt = bp_ref[:, grp:grp + 1]
        w_last = jnp.where(is_sample_tile, ws_ref[grp], w_prompt).astype(BF16)
        bias_last = jnp.where(is_sample_tile, bs_ref[:, grp:grp + 1], bias_prompt)
        w_prompt = w_prompt.astype(BF16)
        for c in range(CHUNKS_PER_TILE):
            rows = slice(c * CHUNK, (c + 1) * CHUNK)
            last = c == CHUNKS_PER_TILE - 1
            mixed = _dot(w_last if last else w_prompt, v_ref[rows, cols].astype(BF16))
            mixed = mixed + (bias_last if last else bias_prompt)
            gated_ref[rows, cols] = (u_ref[rows, cols] * mixed).astype(BF16)
    f = _dot(gated_ref[...], wo_ref[...])
    y = _layer_norm(DEEPNORM_ALPHA * r_ref[...] + f, g_ref[...], b_ref[...])
    of_ref[...] = y
    ob_ref[...] = y.astype(BF16)


def _sgu_out(z2, w_s, w_s_sample, b_t, b_t_sample, w_out, j, resid, lnp, layer):
    row = lambda i: (i, 0)
    const2 = lambda i: (0, 0)
    const3 = lambda i: (0, 0, 0)
    return pl.pallas_call(
        _sgu_out_kernel,
        grid=(SGU_TILES,),
        in_specs=[pl.BlockSpec((SGU_BM, D_MODEL), lambda i: (i, 0)),
                  pl.BlockSpec((SGU_BM, D_MODEL), lambda i: (i, 1)),
                  pl.BlockSpec((None, SGU_GROUPS, CHUNK, CHUNK), lambda i: (j, 0, 0, 0)),
                  pl.BlockSpec((SGU_GROUPS, CHUNK, CHUNK), const3),
                  pl.BlockSpec((CHUNK, SGU_GROUPS), const2),
                  pl.BlockSpec((CHUNK, SGU_GROUPS), const2),
                  pl.BlockSpec((None, D_MODEL, D_MODEL), lambda i: (j, 0, 0), pipeline_mode=pl.Buffered(1)),
                  pl.BlockSpec((SGU_BM, D_MODEL), row)] + _ln_specs(0, layer, 1),
        out_specs=[pl.BlockSpec((SGU_BM, D_MODEL), row), pl.BlockSpec((SGU_BM, D_MODEL), row)],
        out_shape=[jax.ShapeDtypeStruct((N_TOK, D_MODEL), F32), jax.ShapeDtypeStruct((N_TOK, D_MODEL), BF16)],
        scratch_shapes=[pltpu.VMEM((SGU_BM, D_MODEL), BF16)],
        compiler_params=_params(("arbitrary",), 40),
        name="sgu_out",
    )(z2, z2, w_s, w_s_sample, b_t, b_t_sample, w_out, resid, lnp, lnp)


def _gu_kernel(x_ref, wg_ref, wu_ref, o_ref):
    x = x_ref[...]
    gate = _dot(x, wg_ref[...])
    up = _dot(x, wu_ref[...])
    o_ref[...] = (jax.nn.silu(gate) * up).astype(o_ref.dtype)


def _dense_gu(xb, w_gu, layer):
    bn = 1408
    nt = D_FF // bn
    return pl.pallas_call(
        _gu_kernel,
        grid=(nt, N_TILES),
        in_specs=[pl.BlockSpec((BM, D_MODEL), lambda j, i: (i, 0)),
                  pl.BlockSpec((None, D_MODEL, bn), lambda j, i: (layer, 0, j)),
                  pl.BlockSpec((None, D_MODEL, bn), lambda j, i: (layer, 0, j + nt))],
        out_specs=pl.BlockSpec((BM, bn), lambda j, i: (i, j)),
        out_shape=jax.ShapeDtypeStruct((N_TOK, D_FF), BF16),
        compiler_params=_params(("arbitrary", "arbitrary"), 56),
        name="dense_gu",
    )(xb, w_gu, w_gu)


def _attn_kernel(sink_ref, q_ref, kp_ref, vp_ref, kc_ref, vc_ref, o_ref, *, blocks_per_seq):
    has_prev = (pl.program_id(0) % blocks_per_seq) != 0
    qi = lax.broadcasted_iota(jnp.int32, (WINDOW, 2 * WINDOW), 0)
    kj = lax.broadcasted_iota(jnp.int32, (WINDOW, 2 * WINDOW), 1)
    delta_i = WINDOW + qi - kj
    valid = (delta_i >= 0) & (delta_i <= WINDOW) & (has_prev | (kj >= WINDOW))
    delta = delta_i.astype(F32)
    low_half = lax.broadcasted_iota(jnp.int32, (2 * WINDOW, 128), 1) < HEAD_DIM

    kk = jnp.concatenate([kp_ref[...], kc_ref[...]], axis=0)
    vv = jnp.concatenate([vp_ref[...], vc_ref[...]], axis=0)

    def halves(slab, kv):
        if kv % 2 == 0:
            lo = jnp.where(low_half, slab, 0.0)
            hi = jnp.where(low_half, 0.0, pltpu.roll(slab, HEAD_DIM, 1))
        else:
            lo = jnp.where(low_half, pltpu.roll(slab, HEAD_DIM, 1), 0.0)
            hi = jnp.where(low_half, 0.0, slab)
        return lo.astype(BF16), hi.astype(BF16)

    for kv in range(N_KV_HEADS):
        lanes = slice((kv // 2) * 128, (kv // 2 + 1) * 128)
        k_halves = halves(kk[:, lanes], kv)
        v_halves = halves(vv[:, lanes], kv)
        for pair in range(GQA_GROUP // 2):
            qcols = slice((kv * 4 + pair) * 128, (kv * 4 + pair + 1) * 128)
            q2 = q_ref[:, qcols].astype(BF16)
            out = None
            for half in range(2):
                h = kv * GQA_GROUP + 2 * pair + half
                slope = float(2.0 ** (-8.0 * (h + 1) / N_HEADS))
                s = lax.dot_general(q2, k_halves[half], (((1,), (1,)), ((), ())),
                                    preferred_element_type=F32) * (HEAD_DIM ** -0.5)
                s = s - slope * delta
                s = jnp.where(valid, s, NEG_INF)
                sink = sink_ref[h]
                mx = jnp.maximum(jnp.max(s, axis=-1, keepdims=True), sink)
                p = jnp.exp(s - mx)
                den = jnp.sum(p, axis=-1, keepdims=True) + jnp.exp(sink - mx)
                p = p / den
                o = _dot(p.astype(BF16), v_halves[half])
                out = o if out is None else out + o
            o_ref[:, qcols] = out.astype(o_ref.dtype)


def _attention(sinks, q, kp, vp, kc, vc, n_blocks, specs, blocks_per_seq, name):
    return pl.pallas_call(
        functools.partial(_attn_kernel, blocks_per_seq=blocks_per_seq),
        grid=(n_blocks,),
        in_specs=[pl.BlockSpec(memory_space=pltpu.SMEM)] + specs,
        out_specs=pl.BlockSpec((WINDOW, D_MODEL), lambda i: (i, 0)),
        out_shape=jax.ShapeDtypeStruct((n_blocks * WINDOW, D_MODEL), BF16),
        compiler_params=_params(("arbitrary",), 32),
        name=name,
    )(sinks, q, kp, vp, kc, vc)


SAMPLE_ROWS = GQA_GROUP * DEC_SEQ
SAMPLE_KEYS = 2 * WINDOW


def _attn_sample_kernel(q_ref, k_ref, v_ref, sink_ref, slope_ref, o_ref):
    step = lax.broadcasted_iota(jnp.int32, (SAMPLE_ROWS, SAMPLE_KEYS), 0) % DEC_SEQ
    kj = lax.broadcasted_iota(jnp.int32, (SAMPLE_ROWS, SAMPLE_KEYS), 1)
    delta_i = WINDOW + step - kj
    valid = (delta_i >= 0) & (delta_i <= WINDOW)
    delta = delta_i.astype(F32)
    for kv in range(N_KV_HEADS):
        lanes = slice((kv // 2) * 128, (kv // 2 + 1) * 128)
        s = lax.dot_general(q_ref[kv].astype(BF16), k_ref[:, lanes].astype(BF16), (((1,), (1,)), ((), ())),
                            preferred_element_type=F32) * (HEAD_DIM ** -0.5)
        s = s - slope_ref[kv] * delta
        s = jnp.where(valid, s, NEG_INF)
        sink = sink_ref[kv]
        mx = jnp.maximum(jnp.max(s, axis=-1, keepdims=True), sink)
        p = jnp.exp(s - mx)
        den = jnp.sum(p, axis=-1, keepdims=True) + jnp.exp(sink - mx)
        p = p / den
        o_ref[kv] = _dot(p.astype(BF16), v_ref[:, lanes].astype(BF16)).astype(o_ref.dtype)


def _attn_sample(q_s, k_new, v_new, state_k, state_v, sinks):
    kvw = N_KV_HEADS * HEAD_DIM
    q = q_s.reshape(DEC_BATCH, DEC_SEQ, N_KV_HEADS, GQA_GROUP, HEAD_DIM).transpose(0, 2, 3, 1, 4)
    q = q.reshape(DEC_BATCH, N_KV_HEADS, SAMPLE_ROWS, HEAD_DIM)
    zero = jnp.zeros_like(q)
    odd = (jnp.arange(N_KV_HEADS) % 2 == 1)[None, :, None, None]
    q = jnp.where(odd, jnp.concatenate([zero, q], axis=-1), jnp.concatenate([q, zero], axis=-1))

    def keys(state, new):
        pad = jnp.zeros((DEC_BATCH, SAMPLE_KEYS - WINDOW - DEC_SEQ, kvw), F32)
        return jnp.concatenate([state.reshape(DEC_BATCH, WINDOW, kvw), new, pad], axis=1)

    head = np.arange(N_KV_HEADS)[:, None] * GQA_GROUP + np.arange(SAMPLE_ROWS)[None, :] // DEC_SEQ
    slopes = jnp.asarray(np.exp2(-8.0 * (head + 1) / N_HEADS), F32)[:, :, None]
    sink_rows = sinks[head][:, :, None]
    seq3 = lambda i: (i, 0, 0)
    whole = lambda i: (0, 0, 0)
    out = pl.pallas_call(
        _attn_sample_kernel,
        grid=(DEC_BATCH,),
        in_specs=[pl.BlockSpec((None, N_KV_HEADS, SAMPLE_ROWS, 128), lambda i: (i, 0, 0, 0)),
                  pl.BlockSpec((None, SAMPLE_KEYS, kvw), seq3),
                  pl.BlockSpec((None, SAMPLE_KEYS, kvw), seq3),
                  pl.BlockSpec((N_KV_HEADS, SAMPLE_ROWS, 1), whole),
                  pl.BlockSpec((N_KV_HEADS, SAMPLE_ROWS, 1), whole)],
        out_specs=pl.BlockSpec((None, N_KV_HEADS, SAMPLE_ROWS, 128), lambda i: (i, 0, 0, 0)),
        out_shape=jax.ShapeDtypeStruct((DEC_BATCH, N_KV_HEADS, SAMPLE_ROWS, 128), BF16),
        compiler_params=_params(("arbitrary",), 16),
        name="attn_sample",
    )(q, keys(state_k, k_new), keys(state_v, v_new), sink_rows, slopes)
    out = jnp.where(odd, out[..., HEAD_DIM:], out[..., :HEAD_DIM])
    out = out.reshape(DEC_BATCH, N_KV_HEADS, GQA_GROUP, DEC_SEQ, HEAD_DIM).transpose(0, 3, 1, 2, 4)
    return out.reshape(N_SAMPLE, D_MODEL)


def _swa_mixer(xb, x, state_k, state_v, w_qkv, sinks, w_o, j, lnp, layer):
    kvw = N_KV_HEADS * HEAD_DIM
    qkv = _mm(xb, w_qkv, j, 1280, F32, "qkv")
    kcol, vcol = D_MODEL // kvw, D_MODEL // kvw + 1
    nb = SEQ // WINDOW

    def prev_block(i):
        return jnp.maximum(i - 1, 0)

    prompt_specs = [pl.BlockSpec((WINDOW, D_MODEL), lambda i: (i, 0)),
                    pl.BlockSpec((WINDOW, kvw), lambda i: (prev_block(i), kcol)),
                    pl.BlockSpec((WINDOW, kvw), lambda i: (prev_block(i), vcol)),
                    pl.BlockSpec((WINDOW, kvw), lambda i: (i, kcol)),
                    pl.BlockSpec((WINDOW, kvw), lambda i: (i, vcol))]
    o_prompt = _attention(sinks, qkv, qkv, qkv, qkv, qkv, N_PROMPT // WINDOW, prompt_specs, nb, "attn_prompt")

    qkv_s = qkv[N_PROMPT:].reshape(DEC_BATCH, DEC_SEQ, D_MODEL + 2 * kvw)
    k_new = qkv_s[..., D_MODEL:D_MODEL + kvw]
    v_new = qkv_s[..., D_MODEL + kvw:]
    o_sample = _attn_sample(qkv_s[..., :D_MODEL], k_new, v_new, state_k, state_v, sinks)
    o_all = jnp.concatenate([o_prompt, o_sample], axis=0)
    x, xb = _mm_ln(o_all, w_o, j, x, lnp, 0, layer, BM, "attn_out")

    k_p = qkv[:N_PROMPT, D_MODEL:D_MODEL + kvw].reshape(BATCH, SEQ, N_KV_HEADS, HEAD_DIM)[:, -WINDOW:]
    v_p = qkv[:N_PROMPT, D_MODEL + kvw:].reshape(BATCH, SEQ, N_KV_HEADS, HEAD_DIM)[:, -WINDOW:]
    k_s = jnp.concatenate([state_k, k_new.reshape(DEC_BATCH, DEC_SEQ, N_KV_HEADS, HEAD_DIM)], axis=1)[:, -WINDOW:]
    v_s = jnp.concatenate([state_v, v_new.reshape(DEC_BATCH, DEC_SEQ, N_KV_HEADS, HEAD_DIM)], axis=1)[:, -WINDOW:]
    return x, xb, (k_p, v_p, k_s, v_s)


CONV_BM = 512
CONV_BC = 1024


def _conv_taps(z, z1, z2, gb, wc_ref):
    conv = wc_ref[0:1, :] * z2 + wc_ref[1:2, :] * z1 + wc_ref[2:3, :] * z
    return (gb * conv).astype(BF16)


def _conv_prompt_kernel(gb_ref, gc_ref, h_ref, wc_ref, y_ref, z_ref, carry_ref):
    @pl.when(pl.program_id(1) % (SEQ // CONV_BM) == 0)
    def _():
        carry_ref[...] = jnp.zeros_like(carry_ref)

    z = gc_ref[...] * h_ref[...]
    row = lax.broadcasted_iota(jnp.int32, z.shape, 0)
    c6 = carry_ref[6:7, :]
    c7 = carry_ref[7:8, :]
    z1 = jnp.where(row >= 1, pltpu.roll(z, 1, 0), c7)
    z2 = jnp.where(row >= 2, pltpu.roll(z, 2, 0), jnp.where(row == 1, c7, c6))
    y_ref[...] = _conv_taps(z, z1, z2, gb_ref[...], wc_ref)
    z_ref[...] = z
    carry_ref[...] = z[CONV_BM - 8:, :]


def _conv_sample_kernel(gb_ref, gc_ref, h_ref, wc_ref, pa_ref, pb_ref, y_ref, z_ref):
    z = gc_ref[...] * h_ref[...]
    t = lax.broadcasted_iota(jnp.int32, z.shape, 0) % DEC_SEQ
    z1 = jnp.where(t >= 1, pltpu.roll(z, 1, 0), pa_ref[...])
    z2 = jnp.where(t >= 2, pltpu.roll(z, 2, 0), pb_ref[...])
    y_ref[...] = _conv_taps(z, z1, z2, gb_ref[...], wc_ref)
    z_ref[...] = z


def _conv_mixer(xb, x, state, w_in, w_conv, w_out, j, lnp, layer):
    proj = _mm(xb, w_in, j, 1024, F32, "conv_in")
    ncb = D_MODEL // CONV_BC
    y_p, z_p = pl.pallas_call(
        _conv_prompt_kernel,
        grid=(ncb, N_PROMPT // CONV_BM),
        in_specs=[pl.BlockSpec((CONV_BM, CONV_BC), lambda c, i: (i, c)),
                  pl.BlockSpec((CONV_BM, CONV_BC), lambda c, i: (i, ncb + c)),
                  pl.BlockSpec((CONV_BM, CONV_BC), lambda c, i: (i, 2 * ncb + c)),
                  pl.BlockSpec((None, CONV_WIDTH, CONV_BC), lambda c, i: (j, 0, c))],
        out_specs=[pl.BlockSpec((CONV_BM, CONV_BC), lambda c, i: (i, c)),
                   pl.BlockSpec((CONV_BM, CONV_BC), lambda c, i: (i, c))],
        out_shape=[jax.ShapeDtypeStruct((N_PROMPT, D_MODEL), BF16), jax.ShapeDtypeStruct((N_PROMPT, D_MODEL), F32)],
        scratch_shapes=[pltpu.VMEM((8, CONV_BC), F32)],
        compiler_params=_params(("arbitrary", "arbitrary"), 40),
        name="conv_prompt",
    )(proj, proj, proj, w_conv)

    zeros = jnp.zeros((DEC_BATCH, DEC_SEQ, D_MODEL), F32)
    past_a = zeros.at[:, 0].set(state[:, 1]).reshape(N_SAMPLE, D_MODEL)
    past_b = zeros.at[:, 0].set(state[:, 0]).at[:, 1].set(state[:, 1]).reshape(N_SAMPLE, D_MODEL)
    srow = N_PROMPT // N_SAMPLE
    whole = lambda i: (0, 0)
    y_s, z_s = pl.pallas_call(
        _conv_sample_kernel,
        grid=(1,),
        in_specs=[pl.BlockSpec((N_SAMPLE, D_MODEL), lambda i: (srow, 0)),
                  pl.BlockSpec((N_SAMPLE, D_MODEL), lambda i: (srow, 1)),
                  pl.BlockSpec((N_SAMPLE, D_MODEL), lambda i: (srow, 2)),
                  pl.BlockSpec((None, CONV_WIDTH, D_MODEL), lambda i: (j, 0, 0)),
                  pl.BlockSpec((N_SAMPLE, D_MODEL), whole),
                  pl.BlockSpec((N_SAMPLE, D_MODEL), whole)],
        out_specs=[pl.BlockSpec((N_SAMPLE, D_MODEL), whole), pl.BlockSpec((N_SAMPLE, D_MODEL), whole)],
        out_shape=[jax.ShapeDtypeStruct((N_SAMPLE, D_MODEL), BF16), jax.ShapeDtypeStruct((N_SAMPLE, D_MODEL), F32)],
        compiler_params=_params(("arbitrary",), 32),
        name="conv_sample",
    )(proj, proj, proj, w_conv, past_a, past_b)

    y_all = jnp.concatenate([y_p, y_s], axis=0)
    x, xb = _mm_ln(y_all, w_out, j, x, lnp, 0, layer, BM, "conv_out")
    conv_p = z_p.reshape(BATCH, SEQ, D_MODEL)[:, -(CONV_WIDTH - 1):]
    conv_s = z_s.reshape(DEC_BATCH, DEC_SEQ, D_MODEL)[:, -(CONV_WIDTH - 1):]
    return x, xb, (conv_p, conv_s)


def _router_kernel(x_ref, w_ref, sel_ref, gate_ref):
    logits = jnp.dot(x_ref[...], w_ref[...], preferred_element_type=F32, precision=lax.Precision.HIGHEST)
    lane = lax.broadcasted_iota(jnp.int32, logits.shape, 1)
    neg = jnp.float32(-jnp.inf)
    l1 = jnp.where(lane < N_EXPERTS, logits, neg)
    m1 = jnp.max(l1, axis=-1, keepdims=True)
    i1 = jnp.min(jnp.where(l1 == m1, lane, 128), axis=-1, keepdims=True)
    l2 = jnp.where(lane == i1, neg, l1)
    m2 = jnp.max(l2, axis=-1, keepdims=True)
    i2 = jnp.min(jnp.where(l2 == m2, lane, 128), axis=-1, keepdims=True)
    e2 = jnp.exp(m2 - m1)
    den = 1.0 + e2
    g1 = 1.0 / den
    g2 = e2 / den
    sel_ref[...] = ((lane == i1) | (lane == i2)).astype(jnp.int32)
    gate_ref[...] = jnp.where(lane == i1, g1, jnp.where(lane == i2, g2, 0.0))


def _router(x, w_router_padded, layer):
    row = lambda i: (i, 0)
    return pl.pallas_call(
        _router_kernel,
        grid=(N_TILES,),
        in_specs=[pl.BlockSpec((BM, D_MODEL), row),
                  pl.BlockSpec((None, D_MODEL, 128), lambda i: (layer, 0, 0))],
        out_specs=[pl.BlockSpec((BM, 128), row), pl.BlockSpec((BM, 128), row)],
        out_shape=[jax.ShapeDtypeStruct((N_TOK, 128), jnp.int32), jax.ShapeDtypeStruct((N_TOK, 128), F32)],
        compiler_params=_params(("arbitrary",), 32),
        name="router",
    )(x, w_router_padded)


def _dispatch_kernel(lo_ref, hi_ref, d0_ref, d1_ref, g0_ref, g1_ref, x_ref, xs_ref, gs_ref, acc_ref, gacc_ref):
    i = pl.program_id(0)
    lo = lo_ref[i]
    hi = hi_ref[i]
    slot = i * MOE_SUB + lax.broadcasted_iota(jnp.int32, (MOE_SUB, TOK_CHUNK), 0)
    acc_ref[...] = jnp.zeros_like(acc_ref)
    gacc_ref[...] = jnp.zeros_like(gacc_ref)

    def chunk(c, width, x_rows):
        s = slot[:, :width]
        m0 = s == d0_ref[c][:, :width]
        m1 = s == d1_ref[c][:, :width]
        onehot = jnp.where(m0 | m1, 1.0, 0.0).astype(BF16)
        acc_ref[...] += _dot(onehot, x_rows)
        picked = jnp.where(m0, g0_ref[c][:, :width], 0.0) + jnp.where(m1, g1_ref[c][:, :width], 0.0)
        gacc_ref[...] += jnp.sum(picked, axis=1, keepdims=True)

    def body(c, carry):
        start = pl.multiple_of(c * TOK_CHUNK, TOK_CHUNK)
        chunk(c, TOK_CHUNK, x_ref[pl.ds(start, TOK_CHUNK), :])
        return carry

    lax.fori_loop(lo, jnp.minimum(hi, TOK_CHUNKS_FULL - 1) + 1, body, 0)

    @pl.when(hi == TOK_CHUNKS_FULL)
    def _():
        chunk(TOK_CHUNKS_FULL, TOK_TAIL, x_ref[TOK_CHUNKS_FULL * TOK_CHUNK:, :])

    xs_ref[...] = acc_ref[...].astype(BF16)
    gs_ref[...] = gacc_ref[...]


def _dispatch(xb, chunk_lo, chunk_hi, d0r, d1r, g0r, g1r):
    n_tiles = MOE_SLOTS // MOE_SUB
    whole3 = lambda i, lo, hi: (0, 0, 0)
    rows = pl.BlockSpec((TOK_CHUNKS, 1, TOK_CHUNK), whole3)
    grid_spec = pltpu.PrefetchScalarGridSpec(
        num_scalar_prefetch=2,
        grid=(n_tiles,),
        in_specs=[rows, rows, rows, rows,
                  pl.BlockSpec((N_TOK, D_MODEL), lambda i, lo, hi: (0, 0))],
        out_specs=[pl.BlockSpec((MOE_SUB, D_MODEL), lambda i, lo, hi: (i, 0)),
                   pl.BlockSpec((MOE_SUB, 1), lambda i, lo, hi: (i, 0))],
        scratch_shapes=[pltpu.VMEM((MOE_SUB, D_MODEL), F32), pltpu.VMEM((MOE_SUB, 1), F32)],
    )
    return pl.pallas_call(
        _dispatch_kernel,
        grid_spec=grid_spec,
        out_shape=[jax.ShapeDtypeStruct((MOE_SLOTS, D_MODEL), BF16), jax.ShapeDtypeStruct((MOE_SLOTS, 1), F32)],
        compiler_params=_params(("arbitrary",), 52),
        name="moe_dispatch",
    )(chunk_lo, chunk_hi, d0r, d1r, g0r, g1r, xb)


def _expert_changed(be_ref, i):
    return (i == 0) | (be_ref[i] != be_ref[jnp.maximum(i - 1, 0)])


def _for_live_rows(n_sub, live_subs, o_ref, compute):
    for k in range(n_sub + 1):
        @pl.when(live_subs == k)
        def _(k=k):
            dead = (n_sub - k) * MOE_SUB
            if k > 0:
                o_ref[dead:, :] = compute(slice(dead, n_sub * MOE_SUB))
            if k < n_sub:
                o_ref[:dead, :] = jnp.zeros((dead, o_ref.shape[1]), o_ref.dtype)


def _moe_gu_kernel(be_ref, ls_ref, nt_ref, x_ref, wg_ref, wu_ref, o_ref, wgb_ref, wub_ref):
    i = pl.program_id(1)

    @pl.when(_expert_changed(be_ref, i) & (i < nt_ref[0]))
    def _():
        wgb_ref[...] = wg_ref[...].astype(BF16)
        wub_ref[...] = wu_ref[...].astype(BF16)

    def compute(rows):
        x = x_ref[rows, :]
        gate = _dot(x, wgb_ref[...])
        up = _dot(x, wub_ref[...])
        return (jax.nn.silu(gate) * up).astype(o_ref.dtype)

    _for_live_rows(MOE_GU_BM // MOE_SUB, ls_ref[i], o_ref, compute)


def _live_tile(i, nt):
    return jnp.minimum(i, nt[0] - 1)


def _moe_gu(xs, w_gu, layer, block_e, live_subs, n_live):
    nt = D_FF // MOE_BN
    grid_spec = pltpu.PrefetchScalarGridSpec(
        num_scalar_prefetch=3,
        grid=(nt, MOE_SLOTS // MOE_GU_BM),
        in_specs=[pl.BlockSpec((MOE_GU_BM, D_MODEL), lambda j, i, be, ls, n: (_live_tile(i, n), 0)),
                  pl.BlockSpec((None, None, D_MODEL, MOE_BN),
                               lambda j, i, be, ls, n: (layer, be[_live_tile(i, n)], 0, j)),
                  pl.BlockSpec((None, None, D_MODEL, MOE_BN),
                               lambda j, i, be, ls, n: (layer, be[_live_tile(i, n)], 0, j + nt))],
        out_specs=pl.BlockSpec((MOE_GU_BM, MOE_BN), lambda j, i, be, ls, n: (i, j)),
        scratch_shapes=[pltpu.VMEM((D_MODEL, MOE_BN), BF16), pltpu.VMEM((D_MODEL, MOE_BN), BF16)],
    )
    return pl.pallas_call(
        _moe_gu_kernel,
        grid_spec=grid_spec,
        out_shape=jax.ShapeDtypeStruct((MOE_SLOTS, D_FF), BF16),
        compiler_params=_params(("arbitrary", "arbitrary"), 48),
        name="moe_gu",
    )(block_e, live_subs, n_live, xs, w_gu, w_gu)


def _moe_down_kernel(be_ref, ls_ref, nt_ref, h_ref, w_ref, gs_ref, o_ref, wb_ref):
    i = pl.program_id(1)

    @pl.when(_expert_changed(be_ref, i) & (i < nt_ref[0]))
    def _():
        wb_ref[...] = w_ref[...].astype(BF16)

    def compute(rows):
        return (_dot(h_ref[rows, :], wb_ref[...]) * gs_ref[rows, :]).astype(o_ref.dtype)

    _for_live_rows(MOE_DOWN_BM // MOE_SUB, ls_ref[i], o_ref, compute)


def _moe_down(h, w_down, layer, gs, block_e, live_subs, n_live):
    grid_spec = pltpu.PrefetchScalarGridSpec(
        num_scalar_prefetch=3,
        grid=(D_MODEL // MOE_BN, MOE_SLOTS // MOE_DOWN_BM),
        in_specs=[pl.BlockSpec((MOE_DOWN_BM, D_FF), lambda j, i, be, ls, n: (_live_tile(i, n), 0)),
                  pl.BlockSpec((None, None, D_FF, MOE_BN),
                               lambda j, i, be, ls, n: (layer, be[_live_tile(i, n)], 0, j)),
                  pl.BlockSpec((MOE_DOWN_BM, 1), lambda j, i, be, ls, n: (_live_tile(i, n), 0))],
        out_specs=pl.BlockSpec((MOE_DOWN_BM, MOE_BN), lambda j, i, be, ls, n: (i, j)),
        scratch_shapes=[pltpu.VMEM((D_FF, MOE_BN), BF16)],
    )
    return pl.pallas_call(
        _moe_down_kernel,
        grid_spec=grid_spec,
        out_shape=jax.ShapeDtypeStruct((MOE_SLOTS, D_MODEL), BF16),
        compiler_params=_params(("arbitrary", "arbitrary"), 52),
        name="moe_down",
    )(block_e, live_subs, n_live, h, w_down, gs)


def _combine_ln_kernel(ng_ref, ck_ref, d0_ref, d1_ref, x_ref, ys_ref, g_ref, b_ref, of_ref, ob_ref,
                       buf_ref, sem_ref, acc_ref):
    i = pl.program_id(0)
    n = ng_ref[i]

    def chunk_id(g, c):
        return ck_ref[i * COMBINE_KMAX + g * COMBINE_GROUP + c]

    def copies(g, which):
        out = []
        for c in range(COMBINE_GROUP):
            cid = chunk_id(g, c)
            start = pl.multiple_of(jnp.maximum(cid, 0) * MOE_SUB, MOE_SUB)
            out.append((cid >= 0, pltpu.make_async_copy(
                ys_ref.at[pl.ds(start, MOE_SUB)],
                buf_ref.at[which, pl.ds(c * MOE_SUB, MOE_SUB)],
                sem_ref.at[which])))
        return out

    def start_group(g, which):
        for present, copy in copies(g, which):
            pl.when(present)(copy.start)

    def wait_group(g, which):
        for present, copy in copies(g, which):
            pl.when(present)(copy.wait)

    @pl.when(i == 0)
    def _():
        buf_ref[...] = jnp.zeros_like(buf_ref)

    @pl.when(n > 0)
    def _():
        start_group(0, 0)

    acc_ref[...] = jnp.zeros_like(acc_ref)
    d0 = d0_ref[...]
    d1 = d1_ref[...]
    lane = lax.broadcasted_iota(jnp.int32, (BM, MOE_SUB), 1)

    def body(g, carry):
        which = g % 2
        wait_group(g, which)

        @pl.when(g + 1 < n)
        def _():
            start_group(g + 1, 1 - which)

        pieces = []
        for c in range(COMBINE_GROUP):
            slot = chunk_id(g, c) * MOE_SUB + lane
            pieces.append(jnp.where((d0 == slot) | (d1 == slot), 1.0, 0.0).astype(BF16))
        acc_ref[...] += _dot(jnp.concatenate(pieces, axis=1), buf_ref[which])
        return carry

    lax.fori_loop(0, n, body, 0)
    y = _layer_norm(DEEPNORM_ALPHA * x_ref[...] + acc_ref[...], g_ref[...], b_ref[...])
    of_ref[...] = y
    ob_ref[...] = y.astype(BF16)


def _combine_ln(x, ys, n_groups, chunk_ids, d0c, d1c, lnp, layer):
    row = lambda i, ng, ck: (i, 0)

    def ln_spec(kind):
        r = (2 + kind) * DEPTH + layer
        return pl.BlockSpec((None, 1, D_MODEL), lambda i, ng, ck: (r, 0, 0))

    grid_spec = pltpu.PrefetchScalarGridSpec(
        num_scalar_prefetch=2,
        grid=(N_TILES,),
        in_specs=[pl.BlockSpec((BM, 1), row),
                  pl.BlockSpec((BM, 1), row),
                  pl.BlockSpec((BM, D_MODEL), row),
                  pl.BlockSpec(memory_space=pl.ANY),
                  ln_spec(0), ln_spec(1)],
        out_specs=[pl.BlockSpec((BM, D_MODEL), row), pl.BlockSpec((BM, D_MODEL), row)],
        scratch_shapes=[pltpu.VMEM((2, COMBINE_GROUP * MOE_SUB, D_MODEL), BF16),
                        pltpu.SemaphoreType.DMA((2,)),
                        pltpu.VMEM((BM, D_MODEL), F32)],
    )
    return pl.pallas_call(
        _combine_ln_kernel,
        grid_spec=grid_spec,
        out_shape=[jax.ShapeDtypeStruct((N_TOK, D_MODEL), F32), jax.ShapeDtypeStruct((N_TOK, D_MODEL), BF16)],
        compiler_params=_params(("arbitrary",), 56),
        name="moe_combine_ln",
    )(n_groups, chunk_ids, d0c, d1c, x, ys, lnp, lnp)


def _token_rows(a, fill):
    pad = TOK_CHUNKS * TOK_CHUNK - N_TOK
    return jnp.pad(a, (0, pad), constant_values=fill).reshape(TOK_CHUNKS, 1, TOK_CHUNK)


def _moe_layer(x, xb, w_router_padded, w_gu, w_down, j, lnp, layer):
    sel, gates = _router(x, w_router_padded, j)
    sel = sel[:, :N_EXPERTS]
    gates = gates[:, :N_EXPERTS]

    cum = jnp.cumsum(sel, axis=0)
    rank = cum - sel
    counts = cum[-1]
    padded = (counts + MOE_PAD - 1) // MOE_PAD * MOE_PAD
    pad_end = jnp.cumsum(padded)
    first = pad_end - padded + (padded - counts) // MOE_SUB * MOE_SUB
    slot = first[None, :] + rank
    chosen = sel > 0
    d0 = jnp.min(jnp.where(chosen, slot, MOE_SLOTS), axis=1).astype(jnp.int32)
    d1 = jnp.max(jnp.where(chosen, slot, -1), axis=1).astype(jnp.int32)
    g0 = jnp.sum(jnp.where(chosen & (slot == d0[:, None]), gates, 0.0), axis=1)
    g1 = jnp.sum(jnp.where(chosen & (slot == d1[:, None]), gates, 0.0), axis=1)

    def tiles(rows_per_tile):
        start = jnp.arange(MOE_SLOTS // rows_per_tile, dtype=jnp.int32) * rows_per_tile
        e = jnp.minimum(jnp.sum(pad_end[None, :] <= start[:, None], axis=1), N_EXPERTS - 1).astype(jnp.int32)
        r0 = jnp.maximum(start - first[e], 0)
        valid = jnp.clip(jnp.minimum(first[e] + counts[e], start + rows_per_tile) - jnp.maximum(first[e], start),
                         0, rows_per_tile)
        return e, r0, valid

    def matmul_tiles(rows_per_tile):
        e, _, valid = tiles(rows_per_tile)
        live_subs = ((valid + MOE_SUB - 1) // MOE_SUB).astype(jnp.int32)
        n_live = (pad_end[-1:] // rows_per_tile).astype(jnp.int32)
        return e, live_subs, n_live

    sub_e, r0, valid = tiles(MOE_SUB)
    cum_rows = cum.T[sub_e]
    tok_lo = jnp.sum(cum_rows < (r0 + 1)[:, None], axis=1)
    tok_hi = jnp.sum(cum_rows < (r0 + valid)[:, None], axis=1)
    chunk_lo = jnp.where(valid > 0, tok_lo // TOK_CHUNK, 1).astype(jnp.int32)
    chunk_hi = jnp.where(valid > 0, tok_hi // TOK_CHUNK, 0).astype(jnp.int32)

    rank_edges = jnp.concatenate([jnp.zeros((1, N_EXPERTS), cum.dtype), cum[BM - 1::BM]], axis=0)
    lo = first[None, :] + rank_edges[:-1]
    hi = first[None, :] + rank_edges[1:]
    per_expert = COMBINE_KMAX // N_EXPERTS
    cand = (lo // MOE_SUB)[:, :, None] + jnp.arange(per_expert)[None, None, :]
    ok = (hi > lo)[:, :, None] & (cand <= ((hi - 1) // MOE_SUB)[:, :, None])
    cand = cand.reshape(N_TILES, COMBINE_KMAX)
    ok = ok.reshape(N_TILES, COMBINE_KMAX)
    order = jnp.argsort(jnp.logical_not(ok), axis=1, stable=True)
    chunk_ids = jnp.where(jnp.take_along_axis(ok, order, axis=1), jnp.take_along_axis(cand, order, axis=1), -1)
    chunk_ids = chunk_ids.astype(jnp.int32).reshape(-1)
    n_groups = ((jnp.sum(ok, axis=1) + COMBINE_GROUP - 1) // COMBINE_GROUP).astype(jnp.int32)

    xs, gs = _dispatch(xb, chunk_lo, chunk_hi, _token_rows(d0, -1), _token_rows(d1, -1),
                       _token_rows(g0, 0.0), _token_rows(g1, 0.0))
    h = _moe_gu(xs, w_gu, j, *matmul_tiles(MOE_GU_BM))
    ys = _moe_down(h, w_down, j, gs, *matmul_tiles(MOE_DOWN_BM))
    return _combine_ln(x, ys, n_groups, chunk_ids, d0[:, None], d1[:, None], lnp, layer)


def _block_diag_causal(w_s):
    small = jnp.tril(w_s[:, :DEC_SEQ, :DEC_SEQ])
    eye = jnp.eye(DEC_BATCH, dtype=w_s.dtype)
    return jnp.einsum("ab,gij->gaibj", eye, small).reshape(SGU_GROUPS, CHUNK, CHUNK)


def kernel(x_prompt, x_sample, state_swa_k, state_swa_v, state_conv, ln1_g, ln1_b, ln2_g, ln2_b, a_w_in, a_ln_g, a_ln_b, a_w_s, a_b_s, a_w_out, b_w_qkv, b_sinks, b_w_o, c_w_in, c_w_conv, c_w_out, f_w_gu, f_w_down, m_w_router, m_w_gu, m_w_down):
    x = jnp.concatenate([x_prompt.reshape(N_PROMPT, D_MODEL), x_sample.reshape(N_SAMPLE, D_MODEL)], axis=0)
    xb = x.astype(BF16)
    lnp = jnp.concatenate([ln1_g, ln1_b, ln2_g, ln2_b], axis=0).reshape(4 * DEPTH, 1, D_MODEL)
    a_lnp = jnp.concatenate([a_ln_g, a_ln_b], axis=0).reshape(-1, 1, D_MODEL)
    a_w_in_b, a_w_out_b = a_w_in.astype(BF16), a_w_out.astype(BF16)
    b_w_qkv_b, b_w_o_b = b_w_qkv.astype(BF16), b_w_o.astype(BF16)
    c_w_in_b, c_w_out_b = c_w_in.astype(BF16), c_w_out.astype(BF16)
    f_w_gu_b, f_w_down_b = f_w_gu.astype(BF16), f_w_down.astype(BF16)
    w_router_padded = jnp.pad(m_w_router, ((0, 0), (0, 0), (0, 128 - N_EXPERTS)))

    sgu_v, swa_states, conv_states = [], [], []
    for i in range(DEPTH):
        kind, j = i % N_MIXERS, i // N_MIXERS
        if kind == 0:
            z2 = _sgu_in(xb, a_w_in_b, a_lnp, j)
            sgu_v.append(z2[N_PROMPT:, D_MODEL:].reshape(DEC_BATCH, DEC_SEQ, D_MODEL))
            b_t = a_b_s[j].T
            b_t_sample = jnp.tile(b_t[:DEC_SEQ], (DEC_BATCH, 1))
            x, xb = _sgu_out(z2, a_w_s, _block_diag_causal(a_w_s[j]), b_t, b_t_sample, a_w_out_b, j, x, lnp, i)
        elif kind == 1:
            x, xb, st = _swa_mixer(xb, x, state_swa_k[j], state_swa_v[j], b_w_qkv_b, b_sinks[j], b_w_o_b, j, lnp, i)
            swa_states.append(st)
        else:
            x, xb, st = _conv_mixer(xb, x, state_conv[j], c_w_in_b, c_w_conv, c_w_out_b, j, lnp, i)
            conv_states.append(st)
        if i % 2 == 0:
            h = _dense_gu(xb, f_w_gu_b, i // 2)
            x, xb = _mm_ln(h, f_w_down_b, i // 2, x, lnp, 1, i, 320, "dense_down")
        else:
            x, xb = _moe_layer(x, xb, w_router_padded, m_w_gu, m_w_down, i // 2, lnp, i)

    y_prompt = x[:N_PROMPT].reshape(BATCH, SEQ, D_MODEL)
    y_sample = x[N_PROMPT:].reshape(DEC_BATCH, DEC_SEQ, D_MODEL)
    return (y_prompt, y_sample, jnp.stack(sgu_v),
            jnp.stack([s[0] for s in swa_states]), jnp.stack([s[1] for s in swa_states]),
            jnp.stack([s[2] for s in swa_states]), jnp.stack([s[3] for s in swa_states]),
            jnp.stack([s[0] for s in conv_states]), jnp.stack([s[1] for s in conv_states]))
```

```python
import functools

import numpy as np
import jax
import jax.numpy as jnp
from jax import lax
from jax.experimental import pallas as pl
from jax.experimental.pallas import tpu as pltpu

D_MODEL = 2048
BATCH = 2
SEQ = 4096
DEPTH = 4
DEC_BATCH = 32
DEC_SEQ = 4
N_MIXERS = 3
CHUNK = 128
SGU_GROUPS = 8
SGU_GROUP_DIM = D_MODEL // SGU_GROUPS
HEAD_DIM = 64
N_HEADS = D_MODEL // HEAD_DIM
N_KV_HEADS = N_HEADS // 8
GQA_GROUP = N_HEADS // N_KV_HEADS
WINDOW = 128
CONV_WIDTH = 3
D_FF = 5632
N_EXPERTS = 8
LN_EPS = 1e-5
DEEPNORM_ALPHA = (2.0 * DEPTH) ** 0.25
NEG_INF = -1e30

N_PROMPT = BATCH * SEQ
N_SAMPLE = DEC_BATCH * DEC_SEQ
N_TOK = N_PROMPT + N_SAMPLE
BM = 640
N_TILES = N_TOK // BM
SGU_BM = 128
SGU_TILES = N_TOK // SGU_BM
CHUNKS_PER_TILE = SGU_BM // CHUNK

MOE_SUB = 256
MOE_PAD = 4 * MOE_SUB
MOE_SLOTS = (-(-(2 * N_TOK) // MOE_PAD) + N_EXPERTS) * MOE_PAD
MOE_GU_BM = MOE_PAD
MOE_DOWN_BM = 2 * MOE_SUB
MOE_BN = 512
COMBINE_GROUP = 4
TOK_CHUNK = 512
TOK_CHUNKS_FULL = N_TOK // TOK_CHUNK
TOK_TAIL = N_TOK - TOK_CHUNKS_FULL * TOK_CHUNK
TOK_CHUNKS = TOK_CHUNKS_FULL + 1
COMBINE_KMAX = N_EXPERTS * ((MOE_SUB - 1 + BM - 1) // MOE_SUB + 1)

F32 = jnp.float32
BF16 = jnp.bfloat16
MIB = 1024 * 1024


def _params(semantics, vmem_mib):
    return pltpu.CompilerParams(dimension_semantics=semantics, vmem_limit_bytes=vmem_mib * MIB)


def _layer_norm(y, g, b):
    mu = jnp.mean(y, axis=-1, keepdims=True)
    d = y - mu
    var = jnp.mean(d * d, axis=-1, keepdims=True)
    return d * lax.rsqrt(var + LN_EPS) * g + b


def _gelu(x):
    return 0.5 * x * (1.0 + lax.erf(x * np.float32(np.sqrt(0.5))))


def _dot(a, b):
    return jnp.dot(a, b, preferred_element_type=F32)


def _ln_specs(norm, layer, ngrid):
    def spec(kind):
        r = (2 * norm + kind) * DEPTH + layer
        if ngrid == 1:
            return pl.BlockSpec((None, 1, D_MODEL), lambda i: (r, 0, 0))
        return pl.BlockSpec((None, 1, D_MODEL), lambda j, i: (r, 0, 0))
    return [spec(0), spec(1)]


def _mm_kernel(x_ref, w_ref, o_ref):
    o_ref[...] = _dot(x_ref[...], w_ref[...]).astype(o_ref.dtype)


def _mm(x, w, layer, bn, out_dtype, name):
    m, k = x.shape
    n = w.shape[-1]
    return pl.pallas_call(
        _mm_kernel,
        grid=(n // bn, m // BM),
        in_specs=[pl.BlockSpec((BM, k), lambda j, i: (i, 0)),
                  pl.BlockSpec((None, k, bn), lambda j, i: (layer, 0, j))],
        out_specs=pl.BlockSpec((BM, bn), lambda j, i: (i, j)),
        out_shape=jax.ShapeDtypeStruct((m, n), out_dtype),
        compiler_params=_params(("arbitrary", "arbitrary"), 48),
        name=name,
    )(x, w)


def _mm_ln_kernel(x_ref, w_ref, r_ref, g_ref, b_ref, of_ref, ob_ref):
    bm = x_ref.shape[0]
    group = bm // 2
    for start in range(0, bm, group):
        rows = slice(start, start + group)
        f = _dot(x_ref[rows, :], w_ref[...])
        y = _layer_norm(DEEPNORM_ALPHA * r_ref[rows, :] + f, g_ref[...], b_ref[...])
        of_ref[rows, :] = y
        ob_ref[rows, :] = y.astype(BF16)


def _mm_ln(x, w, w_layer, resid, lnp, norm, layer, bm, name):
    m, k = x.shape
    row = lambda i: (i, 0)
    return pl.pallas_call(
        _mm_ln_kernel,
        grid=(m // bm,),
        in_specs=[pl.BlockSpec((bm, k), row),
                  pl.BlockSpec((None, k, D_MODEL), lambda i: (w_layer, 0, 0), pipeline_mode=pl.Buffered(1)),
                  pl.BlockSpec((bm, D_MODEL), row)] + _ln_specs(norm, layer, 1),
        out_specs=[pl.BlockSpec((bm, D_MODEL), row), pl.BlockSpec((bm, D_MODEL), row)],
        out_shape=[jax.ShapeDtypeStruct((m, D_MODEL), F32), jax.ShapeDtypeStruct((m, D_MODEL), BF16)],
        compiler_params=_params(("arbitrary",), 56),
        name=name,
    )(x, w, resid, lnp, lnp)


def _sgu_in_kernel(x_ref, w_ref, g_ref, b_ref, o_ref):
    half = BM // 2

    @pl.when(pl.program_id(0) == 0)
    def _():
        for rows in (slice(0, half), slice(half, BM)):
            o_ref[rows, :] = _gelu(_dot(x_ref[rows, :], w_ref[...]))

    @pl.when(pl.program_id(0) == 1)
    def _():
        for rows in (slice(0, half), slice(half, BM)):
            o_ref[rows, :] = _layer_norm(_gelu(_dot(x_ref[rows, :], w_ref[...])), g_ref[...], b_ref[...])


def _sgu_in(xb, w_in, a_lnp, j):
    return pl.pallas_call(
        _sgu_in_kernel,
        grid=(2, N_TILES),
        in_specs=[pl.BlockSpec((BM, D_MODEL), lambda c, i: (i, 0)),
                  pl.BlockSpec((None, D_MODEL, D_MODEL), lambda c, i: (j, 0, c)),
                  pl.BlockSpec((None, 1, D_MODEL), lambda c, i: (j, 0, 0)),
                  pl.BlockSpec((None, 1, D_MODEL), lambda c, i: (2 + j, 0, 0))],
        out_specs=pl.BlockSpec((BM, D_MODEL), lambda c, i: (i, c)),
        out_shape=jax.ShapeDtypeStruct((N_TOK, 2 * D_MODEL), F32),
        compiler_params=_params(("arbitrary", "arbitrary"), 56),
        name="sgu_in",
    )(xb, w_in, a_lnp, a_lnp)


def _sgu_out_kernel(u_ref, v_ref, wp_ref, ws_ref, bp_ref, bs_ref, wo_ref, r_ref, g_ref, b_ref,
                    of_ref, ob_ref, gated_ref):
    is_sample_tile = pl.program_id(0) == SGU_TILES - 1
    row = lax.broadcasted_iota(jnp.int32, (CHUNK, CHUNK), 0)
    col = lax.broadcasted_iota(jnp.int32, (CHUNK, CHUNK), 1)
    causal = row >= col
    for grp in range(SGU_GROUPS):
        cols = slice(grp * SGU_GROUP_DIM, (grp + 1) * SGU_GROUP_DIM)
        w_prompt = jnp.where(causal, wp_ref[grp], 0.0)
        bias_prompt = bp_ref[:, grp:grp + 1]
        w_last = jnp.where(is_sample_tile, ws_ref[grp], w_prompt).astype(BF16)
        bias_last = jnp.where(is_sample_tile, bs_ref[:, grp:grp + 1], bias_prompt)
        w_prompt = w_prompt.astype(BF16)
        for c in range(CHUNKS_PER_TILE):
            rows = slice(c * CHUNK, (c + 1) * CHUNK)
            last = c == CHUNKS_PER_TILE - 1
            mixed = _dot(w_last if last else w_prompt, v_ref[rows, cols].astype(BF16))
            mixed = mixed + (bias_last if last else bias_prompt)
            gated_ref[rows, cols] = (u_ref[rows, cols] * mixed).astype(BF16)
    f = _dot(gated_ref[...], wo_ref[...])
    y = _layer_norm(DEEPNORM_ALPHA * r_ref[...] + f, g_ref[...], b_ref[...])
    of_ref[...] = y
    ob_ref[...] = y.astype(BF16)


def _sgu_out(z2, w_s, w_s_sample, b_t, b_t_sample, w_out, j, resid, lnp, layer):
    row = lambda i: (i, 0)
    const2 = lambda i: (0, 0)
    const3 = lambda i: (0, 0, 0)
    return pl.pallas_call(
        _sgu_out_kernel,
        grid=(SGU_TILES,),
        in_specs=[pl.BlockSpec((SGU_BM, D_MODEL), lambda i: (i, 0)),
                  pl.BlockSpec((SGU_BM, D_MODEL), lambda i: (i, 1)),
                  pl.BlockSpec((None, SGU_GROUPS, CHUNK, CHUNK), lambda i: (j, 0, 0, 0)),
                  pl.BlockSpec((SGU_GROUPS, CHUNK, CHUNK), const3),
                  pl.BlockSpec((CHUNK, SGU_GROUPS), const2),
                  pl.BlockSpec((CHUNK, SGU_GROUPS), const2),
                  pl.BlockSpec((None, D_MODEL, D_MODEL), lambda i: (j, 0, 0), pipeline_mode=pl.Buffered(1)),
                  pl.BlockSpec((SGU_BM, D_MODEL), row)] + _ln_specs(0, layer, 1),
        out_specs=[pl.BlockSpec((SGU_BM, D_MODEL), row), pl.BlockSpec((SGU_BM, D_MODEL), row)],
        out_shape=[jax.ShapeDtypeStruct((N_TOK, D_MODEL), F32), jax.ShapeDtypeStruct((N_TOK, D_MODEL), BF16)],
        scratch_shapes=[pltpu.VMEM((SGU_BM, D_MODEL), BF16)],
        compiler_params=_params(("arbitrary",), 40),
        name="sgu_out",
    )(z2, z2, w_s, w_s_sample, b_t, b_t_sample, w_out, resid, lnp, lnp)


def _gu_kernel(x_ref, wg_ref, wu_ref, o_ref):
    half = BM // 2
    for rows in (slice(0, half), slice(half, BM)):
        x = x_ref[rows, :]
        gate = _dot(x, wg_ref[...])
        up = _dot(x, wu_ref[...])
        o_ref[rows, :] = (jax.nn.silu(gate) * up).astype(o_ref.dtype)


def _dense_gu(xb, w_gu, layer):
    bn = 1408
    nt = D_FF // bn
    return pl.pallas_call(
        _gu_kernel,
        grid=(nt, N_TILES),
        in_specs=[pl.BlockSpec((BM, D_MODEL), lambda j, i: (i, 0)),
                  pl.BlockSpec((None, D_MODEL, bn), lambda j, i: (layer, 0, j)),
                  pl.BlockSpec((None, D_MODEL, bn), lambda j, i: (layer, 0, j + nt))],
        out_specs=pl.BlockSpec((BM, bn), lambda j, i: (i, j)),
        out_shape=jax.ShapeDtypeStruct((N_TOK, D_FF), BF16),
        compiler_params=_params(("arbitrary", "arbitrary"), 56),
        name="dense_gu",
    )(xb, w_gu, w_gu)


def _attn_kernel(sink_ref, q_ref, kp_ref, vp_ref, kc_ref, vc_ref, o_ref, *, blocks_per_seq):
    has_prev = (pl.program_id(0) % blocks_per_seq) != 0
    qi = lax.broadcasted_iota(jnp.int32, (WINDOW, 2 * WINDOW), 0)
    kj = lax.broadcasted_iota(jnp.int32, (WINDOW, 2 * WINDOW), 1)
    delta_i = WINDOW + qi - kj
    valid = (delta_i >= 0) & (delta_i <= WINDOW) & (has_prev | (kj >= WINDOW))
    delta = delta_i.astype(F32)
    low_half = lax.broadcasted_iota(jnp.int32, (2 * WINDOW, 128), 1) < HEAD_DIM

    kk = jnp.concatenate([kp_ref[...], kc_ref[...]], axis=0)
    vv = jnp.concatenate([vp_ref[...], vc_ref[...]], axis=0)

    def halves(slab, kv):
        if kv % 2 == 0:
            lo = jnp.where(low_half, slab, 0.0)
            hi = jnp.where(low_half, 0.0, pltpu.roll(slab, HEAD_DIM, 1))
        else:
            lo = jnp.where(low_half, pltpu.roll(slab, HEAD_DIM, 1), 0.0)
            hi = jnp.where(low_half, 0.0, slab)
        return lo.astype(BF16), hi.astype(BF16)

    for kv in range(N_KV_HEADS):
        lanes = slice((kv // 2) * 128, (kv // 2 + 1) * 128)
        k_halves = halves(kk[:, lanes], kv)
        v_halves = halves(vv[:, lanes], kv)
        for pair in range(GQA_GROUP // 2):
            qcols = slice((kv * 4 + pair) * 128, (kv * 4 + pair + 1) * 128)
            q2 = q_ref[:, qcols].astype(BF16)
            out = None
            for half in range(2):
                h = kv * GQA_GROUP + 2 * pair + half
                slope = float(2.0 ** (-8.0 * (h + 1) / N_HEADS))
                s = lax.dot_general(q2, k_halves[half], (((1,), (1,)), ((), ())),
                                    preferred_element_type=F32) * (HEAD_DIM ** -0.5)
                s = s - slope * delta
                s = jnp.where(valid, s, NEG_INF)
                sink = sink_ref[h]
                mx = jnp.maximum(jnp.max(s, axis=-1, keepdims=True), sink)
                p = jnp.exp(s - mx)
                den = jnp.sum(p, axis=-1, keepdims=True) + jnp.exp(sink - mx)
                p = p / den
                o = _dot(p.astype(BF16), v_halves[half])
                out = o if out is None else out + o
            o_ref[:, qcols] = out.astype(o_ref.dtype)


def _attention(sinks, q, kp, vp, kc, vc, n_blocks, specs, blocks_per_seq, name):
    return pl.pallas_call(
        functools.partial(_attn_kernel, blocks_per_seq=blocks_per_seq),
        grid=(n_blocks,),
        in_specs=[pl.BlockSpec(memory_space=pltpu.SMEM)] + specs,
        out_specs=pl.BlockSpec((WINDOW, D_MODEL), lambda i: (i, 0)),
        out_shape=jax.ShapeDtypeStruct((n_blocks * WINDOW, D_MODEL), BF16),
        compiler_params=_params(("arbitrary",), 32),
        name=name,
    )(sinks, q, kp, vp, kc, vc)


SAMPLE_ROWS = GQA_GROUP * DEC_SEQ
SAMPLE_KEYS = 2 * WINDOW


def _attn_sample_kernel(q_ref, k_ref, v_ref, sink_ref, slope_ref, o_ref):
    step = lax.broadcasted_iota(jnp.int32, (SAMPLE_ROWS, SAMPLE_KEYS), 0) % DEC_SEQ
    kj = lax.broadcasted_iota(jnp.int32, (SAMPLE_ROWS, SAMPLE_KEYS), 1)
    delta_i = WINDOW + step - kj
    valid = (delta_i >= 0) & (delta_i <= WINDOW)
    delta = delta_i.astype(F32)
    for kv in range(N_KV_HEADS):
        lanes = slice((kv // 2) * 128, (kv // 2 + 1) * 128)
        s = lax.dot_general(q_ref[kv].astype(BF16), k_ref[:, lanes].astype(BF16), (((1,), (1,)), ((), ())),
                            preferred_element_type=F32) * (HEAD_DIM ** -0.5)
        s = s - slope_ref[kv] * delta
        s = jnp.where(valid, s, NEG_INF)
        sink = sink_ref[kv]
        mx = jnp.maximum(jnp.max(s, axis=-1, keepdims=True), sink)
        p = jnp.exp(s - mx)
        den = jnp.sum(p, axis=-1, keepdims=True) + jnp.exp(sink - mx)
        p = p / den
        o_ref[kv] = _dot(p.astype(BF16), v_ref[:, lanes].astype(BF16)).astype(o_ref.dtype)


def _attn_sample(q_s, k_new, v_new, state_k, state_v, sinks):
    kvw = N_KV_HEADS * HEAD_DIM
    q = q_s.reshape(DEC_BATCH, DEC_SEQ, N_KV_HEADS, GQA_GROUP, HEAD_DIM).transpose(0, 2, 3, 1, 4)
    q = q.reshape(DEC_BATCH, N_KV_HEADS, SAMPLE_ROWS, HEAD_DIM)
    zero = jnp.zeros_like(q)
    odd = (jnp.arange(N_KV_HEADS) % 2 == 1)[None, :, None, None]
    q = jnp.where(odd, jnp.concatenate([zero, q], axis=-1), jnp.concatenate([q, zero], axis=-1))

    def keys(state, new):
        pad = jnp.zeros((DEC_BATCH, SAMPLE_KEYS - WINDOW - DEC_SEQ, kvw), F32)
        return jnp.concatenate([state.reshape(DEC_BATCH, WINDOW, kvw), new, pad], axis=1)

    head = np.arange(N_KV_HEADS)[:, None] * GQA_GROUP + np.arange(SAMPLE_ROWS)[None, :] // DEC_SEQ
    slopes = jnp.asarray(np.exp2(-8.0 * (head + 1) / N_HEADS), F32)[:, :, None]
    sink_rows = sinks[head][:, :, None]
    seq3 = lambda i: (i, 0, 0)
    whole = lambda i: (0, 0, 0)
    out = pl.pallas_call(
        _attn_sample_kernel,
        grid=(DEC_BATCH,),
        in_specs=[pl.BlockSpec((None, N_KV_HEADS, SAMPLE_ROWS, 128), lambda i: (i, 0, 0, 0)),
                  pl.BlockSpec((None, SAMPLE_KEYS, kvw), seq3),
                  pl.BlockSpec((None, SAMPLE_KEYS, kvw), seq3),
                  pl.BlockSpec((N_KV_HEADS, SAMPLE_ROWS, 1), whole),
                  pl.BlockSpec((N_KV_HEADS, SAMPLE_ROWS, 1), whole)],
        out_specs=pl.BlockSpec((None, N_KV_HEADS, SAMPLE_ROWS, 128), lambda i: (i, 0, 0, 0)),
        out_shape=jax.ShapeDtypeStruct((DEC_BATCH, N_KV_HEADS, SAMPLE_ROWS, 128), BF16),
        compiler_params=_params(("arbitrary",), 16),
        name="attn_sample",
    )(q, keys(state_k, k_new), keys(state_v, v_new), sink_rows, slopes)
    out = jnp.where(odd, out[..., HEAD_DIM:], out[..., :HEAD_DIM])
    out = out.reshape(DEC_BATCH, N_KV_HEADS, GQA_GROUP, DEC_SEQ, HEAD_DIM).transpose(0, 3, 1, 2, 4)
    return out.reshape(N_SAMPLE, D_MODEL)


def _swa_mixer(xb, x, state_k, state_v, w_qkv, sinks, w_o, j, lnp, layer):
    kvw = N_KV_HEADS * HEAD_DIM
    qkv = _mm(xb, w_qkv, j, 1280, F32, "qkv")
    kcol, vcol = D_MODEL // kvw, D_MODEL // kvw + 1
    nb = SEQ // WINDOW

    def prev_block(i):
        return jnp.maximum(i - 1, 0)

    prompt_specs = [pl.BlockSpec((WINDOW, D_MODEL), lambda i: (i, 0)),
                    pl.BlockSpec((WINDOW, kvw), lambda i: (prev_block(i), kcol)),
                    pl.BlockSpec((WINDOW, kvw), lambda i: (prev_block(i), vcol)),
                    pl.BlockSpec((WINDOW, kvw), lambda i: (i, kcol)),
                    pl.BlockSpec((WINDOW, kvw), lambda i: (i, vcol))]
    o_prompt = _attention(sinks, qkv, qkv, qkv, qkv, qkv, N_PROMPT // WINDOW, prompt_specs, nb, "attn_prompt")

    qkv_s = qkv[N_PROMPT:].reshape(DEC_BATCH, DEC_SEQ, D_MODEL + 2 * kvw)
    k_new = qkv_s[..., D_MODEL:D_MODEL + kvw]
    v_new = qkv_s[..., D_MODEL + kvw:]
    o_sample = _attn_sample(qkv_s[..., :D_MODEL], k_new, v_new, state_k, state_v, sinks)
    o_all = jnp.concatenate([o_prompt, o_sample], axis=0)
    x, xb = _mm_ln(o_all, w_o, j, x, lnp, 0, layer, BM, "attn_out")

    k_p = qkv[:N_PROMPT, D_MODEL:D_MODEL + kvw].reshape(BATCH, SEQ, N_KV_HEADS, HEAD_DIM)[:, -WINDOW:]
    v_p = qkv[:N_PROMPT, D_MODEL + kvw:].reshape(BATCH, SEQ, N_KV_HEADS, HEAD_DIM)[:, -WINDOW:]
    k_s = jnp.concatenate([state_k, k_new.reshape(DEC_BATCH, DEC_SEQ, N_KV_HEADS, HEAD_DIM)], axis=1)[:, -WINDOW:]
    v_s = jnp.concatenate([state_v, v_new.reshape(DEC_BATCH, DEC_SEQ, N_KV_HEADS, HEAD_DIM)], axis=1)[:, -WINDOW:]
    return x, xb, (k_p, v_p, k_s, v_s)


CONV_BM = 512
CONV_BC = 1024


def _conv_taps(z, z1, z2, gb, wc_ref):
    conv = wc_ref[0:1, :] * z2 + wc_ref[1:2, :] * z1 + wc_ref[2:3, :] * z
    return (gb * conv).astype(BF16)


def _conv_prompt_kernel(gb_ref, gc_ref, h_ref, wc_ref, y_ref, z_ref, carry_ref):
    @pl.when(pl.program_id(1) % (SEQ // CONV_BM) == 0)
    def _():
        carry_ref[...] = jnp.zeros_like(carry_ref)

    z = gc_ref[...] * h_ref[...]
    row = lax.broadcasted_iota(jnp.int32, z.shape, 0)
    c6 = carry_ref[6:7, :]
    c7 = carry_ref[7:8, :]
    z1 = jnp.where(row >= 1, pltpu.roll(z, 1, 0), c7)
    z2 = jnp.where(row >= 2, pltpu.roll(z, 2, 0), jnp.where(row == 1, c7, c6))
    y_ref[...] = _conv_taps(z, z1, z2, gb_ref[...], wc_ref)
    z_ref[...] = z
    carry_ref[...] = z[CONV_BM - 8:, :]


def _conv_sample_kernel(gb_ref, gc_ref, h_ref, wc_ref, pa_ref, pb_ref, y_ref, z_ref):
    z = gc_ref[...] * h_ref[...]
    t = lax.broadcasted_iota(jnp.int32, z.shape, 0) % DEC_SEQ
    z1 = jnp.where(t >= 1, pltpu.roll(z, 1, 0), pa_ref[...])
    z2 = jnp.where(t >= 2, pltpu.roll(z, 2, 0), pb_ref[...])
    y_ref[...] = _conv_taps(z, z1, z2, gb_ref[...], wc_ref)
    z_ref[...] = z


def _conv_mixer(xb, x, state, w_in, w_conv, w_out, j, lnp, layer):
    proj = _mm(xb, w_in, j, 1024, F32, "conv_in")
    ncb = D_MODEL // CONV_BC
    y_p, z_p = pl.pallas_call(
        _conv_prompt_kernel,
        grid=(ncb, N_PROMPT // CONV_BM),
        in_specs=[pl.BlockSpec((CONV_BM, CONV_BC), lambda c, i: (i, c)),
                  pl.BlockSpec((CONV_BM, CONV_BC), lambda c, i: (i, ncb + c)),
                  pl.BlockSpec((CONV_BM, CONV_BC), lambda c, i: (i, 2 * ncb + c)),
                  pl.BlockSpec((None, CONV_WIDTH, CONV_BC), lambda c, i: (j, 0, c))],
        out_specs=[pl.BlockSpec((CONV_BM, CONV_BC), lambda c, i: (i, c)),
                   pl.BlockSpec((CONV_BM, CONV_BC), lambda c, i: (i, c))],
        out_shape=[jax.ShapeDtypeStruct((N_PROMPT, D_MODEL), BF16), jax.ShapeDtypeStruct((N_PROMPT, D_MODEL), F32)],
        scratch_shapes=[pltpu.VMEM((8, CONV_BC), F32)],
        compiler_params=_params(("arbitrary", "arbitrary"), 40),
        name="conv_prompt",
    )(proj, proj, proj, w_conv)

    zeros = jnp.zeros((DEC_BATCH, DEC_SEQ, D_MODEL), F32)
    past_a = zeros.at[:, 0].set(state[:, 1]).reshape(N_SAMPLE, D_MODEL)
    past_b = zeros.at[:, 0].set(state[:, 0]).at[:, 1].set(state[:, 1]).reshape(N_SAMPLE, D_MODEL)
    srow = N_PROMPT // N_SAMPLE
    whole = lambda i: (0, 0)
    y_s, z_s = pl.pallas_call(
        _conv_sample_kernel,
        grid=(1,),
        in_specs=[pl.BlockSpec((N_SAMPLE, D_MODEL), lambda i: (srow, 0)),
                  pl.BlockSpec((N_SAMPLE, D_MODEL), lambda i: (srow, 1)),
                  pl.BlockSpec((N_SAMPLE, D_MODEL), lambda i: (srow, 2)),
                  pl.BlockSpec((None, CONV_WIDTH, D_MODEL), lambda i: (j, 0, 0)),
                  pl.BlockSpec((N_SAMPLE, D_MODEL), whole),
                  pl.BlockSpec((N_SAMPLE, D_MODEL), whole)],
        out_specs=[pl.BlockSpec((N_SAMPLE, D_MODEL), whole), pl.BlockSpec((N_SAMPLE, D_MODEL), whole)],
        out_shape=[jax.ShapeDtypeStruct((N_SAMPLE, D_MODEL), BF16), jax.ShapeDtypeStruct((N_SAMPLE, D_MODEL), F32)],
        compiler_params=_params(("arbitrary",), 32),
        name="conv_sample",
    )(proj, proj, proj, w_conv, past_a, past_b)

    y_all = jnp.concatenate([y_p, y_s], axis=0)
    x, xb = _mm_ln(y_all, w_out, j, x, lnp, 0, layer, BM, "conv_out")
    conv_p = z_p.reshape(BATCH, SEQ, D_MODEL)[:, -(CONV_WIDTH - 1):]
    conv_s = z_s.reshape(DEC_BATCH, DEC_SEQ, D_MODEL)[:, -(CONV_WIDTH - 1):]
    return x, xb, (conv_p, conv_s)


def _router_kernel(x_ref, w_ref, sel_ref, gate_ref):
    logits = jnp.dot(x_ref[...], w_ref[...], preferred_element_type=F32, precision=lax.Precision.HIGHEST)
    lane = lax.broadcasted_iota(jnp.int32, logits.shape, 1)
    neg = jnp.float32(-jnp.inf)
    l1 = jnp.where(lane < N_EXPERTS, logits, neg)
    m1 = jnp.max(l1, axis=-1, keepdims=True)
    i1 = jnp.min(jnp.where(l1 == m1, lane, 128), axis=-1, keepdims=True)
    l2 = jnp.where(lane == i1, neg, l1)
    m2 = jnp.max(l2, axis=-1, keepdims=True)
    i2 = jnp.min(jnp.where(l2 == m2, lane, 128), axis=-1, keepdims=True)
    e2 = jnp.exp(m2 - m1)
    den = 1.0 + e2
    g1 = 1.0 / den
    g2 = e2 / den
    sel_ref[...] = ((lane == i1) | (lane == i2)).astype(jnp.int32)
    gate_ref[...] = jnp.where(lane == i1, g1, jnp.where(lane == i2, g2, 0.0))


def _router(x, w_router_padded, layer):
    row = lambda i: (i, 0)
    return pl.pallas_call(
        _router_kernel,
        grid=(N_TILES,),
        in_specs=[pl.BlockSpec((BM, D_MODEL), row),
                  pl.BlockSpec((None, D_MODEL, 128), lambda i: (layer, 0, 0))],
        out_specs=[pl.BlockSpec((BM, 128), row), pl.BlockSpec((BM, 128), row)],
        out_shape=[jax.ShapeDtypeStruct((N_TOK, 128), jnp.int32), jax.ShapeDtypeStruct((N_TOK, 128), F32)],
        compiler_params=_params(("arbitrary",), 32),
        name="router",
    )(x, w_router_padded)


def _dispatch_kernel(lo_ref, hi_ref, d0_ref, d1_ref, g0_ref, g1_ref, x_ref, xs_ref, gs_ref, acc_ref, gacc_ref):
    i = pl.program_id(0)
    lo = lo_ref[i]
    hi = hi_ref[i]
    slot = i * MOE_SUB + lax.broadcasted_iota(jnp.int32, (MOE_SUB, TOK_CHUNK), 0)
    acc_ref[...] = jnp.zeros_like(acc_ref)
    gacc_ref[...] = jnp.zeros_like(gacc_ref)

    def chunk(c, width, x_rows):
        s = slot[:, :width]
        m0 = s == d0_ref[c][:, :width]
        m1 = s == d1_ref[c][:, :width]
        onehot = jnp.where(m0 | m1, 1.0, 0.0).astype(BF16)
        acc_ref[...] += _dot(onehot, x_rows)
        picked = jnp.where(m0, g0_ref[c][:, :width], 0.0) + jnp.where(m1, g1_ref[c][:, :width], 0.0)
        gacc_ref[...] += jnp.sum(picked, axis=1, keepdims=True)

    def body(c, carry):
        start = pl.multiple_of(c * TOK_CHUNK, TOK_CHUNK)
        chunk(c, TOK_CHUNK, x_ref[pl.ds(start, TOK_CHUNK), :])
        return carry

    lax.fori_loop(lo, jnp.minimum(hi, TOK_CHUNKS_FULL - 1) + 1, body, 0)

    @pl.when(hi == TOK_CHUNKS_FULL)
    def _():
        chunk(TOK_CHUNKS_FULL, TOK_TAIL, x_ref[TOK_CHUNKS_FULL * TOK_CHUNK:, :])

    xs_ref[...] = acc_ref[...].astype(BF16)
    gs_ref[...] = gacc_ref[...]


def _dispatch(xb, chunk_lo, chunk_hi, d0r, d1r, g0r, g1r):
    n_tiles = MOE_SLOTS // MOE_SUB
    whole3 = lambda i, lo, hi: (0, 0, 0)
    rows = pl.BlockSpec((TOK_CHUNKS, 1, TOK_CHUNK), whole3)
    grid_spec = pltpu.PrefetchScalarGridSpec(
        num_scalar_prefetch=2,
        grid=(n_tiles,),
        in_specs=[rows, rows, rows, rows,
                  pl.BlockSpec((N_TOK, D_MODEL), lambda i, lo, hi: (0, 0))],
        out_specs=[pl.BlockSpec((MOE_SUB, D_MODEL), lambda i, lo, hi: (i, 0)),
                   pl.BlockSpec((MOE_SUB, 1), lambda i, lo, hi: (i, 0))],
        scratch_shapes=[pltpu.VMEM((MOE_SUB, D_MODEL), F32), pltpu.VMEM((MOE_SUB, 1), F32)],
    )
    return pl.pallas_call(
        _dispatch_kernel,
        grid_spec=grid_spec,
        out_shape=[jax.ShapeDtypeStruct((MOE_SLOTS, D_MODEL), BF16), jax.ShapeDtypeStruct((MOE_SLOTS, 1), F32)],
        compiler_params=_params(("arbitrary",), 52),
        name="moe_dispatch",
    )(chunk_lo, chunk_hi, d0r, d1r, g0r, g1r, xb)


def _for_expert_tile(n_sub, live_subs, fresh, o_ref, compute, load_weights, kept_weights):
    total = n_sub * MOE_SUB

    def run(k, weights):
        dead = (n_sub - k) * MOE_SUB
        o_ref[dead:, :] = compute(slice(dead, total), weights())
        if dead:
            o_ref[:dead, :] = jnp.zeros((dead, o_ref.shape[1]), o_ref.dtype)

    for k in range(1, n_sub + 1):
        pl.when((fresh == 1) & (live_subs == k))(functools.partial(run, k, load_weights))
    pl.when((fresh == 0) & (live_subs == n_sub))(functools.partial(run, n_sub, kept_weights))

    @pl.when(live_subs == 0)
    def _():
        o_ref[...] = jnp.zeros_like(o_ref)


def _moe_gu_kernel(be_ref, ls_ref, fr_ref, nt_ref, x_ref, wg_ref, wu_ref, o_ref, wgb_ref, wub_ref):
    i = pl.program_id(1)

    def load_weights():
        wg = wg_ref[...].astype(BF16)
        wu = wu_ref[...].astype(BF16)
        wgb_ref[...] = wg
        wub_ref[...] = wu
        return wg, wu

    def kept_weights():
        return wgb_ref[...], wub_ref[...]

    def compute(rows, weights):
        x = x_ref[rows, :]
        gate = _dot(x, weights[0])
        up = _dot(x, weights[1])
        return (jax.nn.silu(gate) * up).astype(o_ref.dtype)

    _for_expert_tile(MOE_GU_BM // MOE_SUB, ls_ref[i], fr_ref[i], o_ref, compute, load_weights, kept_weights)


def _live_tile(i, nt):
    return jnp.minimum(i, nt[0] - 1)


def _moe_gu(xs, w_gu, layer, block_e, live_subs, fresh, n_live):
    nt = D_FF // MOE_BN
    grid_spec = pltpu.PrefetchScalarGridSpec(
        num_scalar_prefetch=4,
        grid=(nt, MOE_SLOTS // MOE_GU_BM),
        in_specs=[pl.BlockSpec((MOE_GU_BM, D_MODEL), lambda j, i, be, ls, fr, n: (_live_tile(i, n), 0)),
                  pl.BlockSpec((None, None, D_MODEL, MOE_BN),
                               lambda j, i, be, ls, fr, n: (layer, be[_live_tile(i, n)], 0, j)),
                  pl.BlockSpec((None, None, D_MODEL, MOE_BN),
                               lambda j, i, be, ls, fr, n: (layer, be[_live_tile(i, n)], 0, j + nt))],
        out_specs=pl.BlockSpec((MOE_GU_BM, MOE_BN), lambda j, i, be, ls, fr, n: (i, j)),
        scratch_shapes=[pltpu.VMEM((D_MODEL, MOE_BN), BF16), pltpu.VMEM((D_MODEL, MOE_BN), BF16)],
    )
    return pl.pallas_call(
        _moe_gu_kernel,
        grid_spec=grid_spec,
        out_shape=jax.ShapeDtypeStruct((MOE_SLOTS, D_FF), BF16),
        compiler_params=_params(("arbitrary", "arbitrary"), 48),
        name="moe_gu",
    )(block_e, live_subs, fresh, n_live, xs, w_gu, w_gu)


def _moe_down_kernel(be_ref, ls_ref, fr_ref, nt_ref, h_ref, w_ref, gs_ref, o_ref, wb_ref):
    i = pl.program_id(1)

    def load_weights():
        w = w_ref[...].astype(BF16)
        wb_ref[...] = w
        return w

    def kept_weights():
        return wb_ref[...]

    def compute(rows, w):
        return (_dot(h_ref[rows, :], w) * gs_ref[rows, :]).astype(o_ref.dtype)

    _for_expert_tile(MOE_DOWN_BM // MOE_SUB, ls_ref[i], fr_ref[i], o_ref, compute, load_weights, kept_weights)


def _moe_down(h, w_down, layer, gs, block_e, live_subs, fresh, n_live):
    grid_spec = pltpu.PrefetchScalarGridSpec(
        num_scalar_prefetch=4,
        grid=(D_MODEL // MOE_BN, MOE_SLOTS // MOE_DOWN_BM),
        in_specs=[pl.BlockSpec((MOE_DOWN_BM, D_FF), lambda j, i, be, ls, fr, n: (_live_tile(i, n), 0)),
                  pl.BlockSpec((None, None, D_FF, MOE_BN),
                               lambda j, i, be, ls, fr, n: (layer, be[_live_tile(i, n)], 0, j)),
                  pl.BlockSpec((MOE_DOWN_BM, 1), lambda j, i, be, ls, fr, n: (_live_tile(i, n), 0))],
        out_specs=pl.BlockSpec((MOE_DOWN_BM, MOE_BN), lambda j, i, be, ls, fr, n: (i, j)),
        scratch_shapes=[pltpu.VMEM((D_FF, MOE_BN), BF16)],
    )
    return pl.pallas_call(
        _moe_down_kernel,
        grid_spec=grid_spec,
        out_shape=jax.ShapeDtypeStruct((MOE_SLOTS, D_MODEL), BF16),
        compiler_params=_params(("arbitrary", "arbitrary"), 52),
        name="moe_down",
    )(block_e, live_subs, fresh, n_live, h, w_down, gs)


def _combine_ln_kernel(ng_ref, ck_ref, d0_ref, d1_ref, x_ref, ys_ref, g_ref, b_ref, of_ref, ob_ref,
                       buf_ref, sem_ref, acc_ref):
    i = pl.program_id(0)
    n = ng_ref[i]

    def chunk_id(g, c):
        return ck_ref[i * COMBINE_KMAX + g * COMBINE_GROUP + c]

    def copies(g, which):
        out = []
        for c in range(COMBINE_GROUP):
            cid = chunk_id(g, c)
            start = pl.multiple_of(jnp.maximum(cid, 0) * MOE_SUB, MOE_SUB)
            out.append((cid >= 0, pltpu.make_async_copy(
                ys_ref.at[pl.ds(start, MOE_SUB)],
                buf_ref.at[which, pl.ds(c * MOE_SUB, MOE_SUB)],
                sem_ref.at[which])))
        return out

    def start_group(g, which):
        for present, copy in copies(g, which):
            pl.when(present)(copy.start)

    def wait_group(g, which):
        for present, copy in copies(g, which):
            pl.when(present)(copy.wait)

    @pl.when(i == 0)
    def _():
        buf_ref[...] = jnp.zeros_like(buf_ref)

    @pl.when(n > 0)
    def _():
        start_group(0, 0)

    acc_ref[...] = jnp.zeros_like(acc_ref)
    d0 = d0_ref[...]
    d1 = d1_ref[...]
    lane = lax.broadcasted_iota(jnp.int32, (BM, MOE_SUB), 1)

    def body(g, carry):
        which = g % 2
        wait_group(g, which)

        @pl.when(g + 1 < n)
        def _():
            start_group(g + 1, 1 - which)

        pieces = []
        for c in range(COMBINE_GROUP):
            slot = chunk_id(g, c) * MOE_SUB + lane
            pieces.append(jnp.where((d0 == slot) | (d1 == slot), 1.0, 0.0).astype(BF16))
        acc_ref[...] += _dot(jnp.concatenate(pieces, axis=1), buf_ref[which])
        return carry

    lax.fori_loop(0, n, body, 0)
    y = _layer_norm(DEEPNORM_ALPHA * x_ref[...] + acc_ref[...], g_ref[...], b_ref[...])
    of_ref[...] = y
    ob_ref[...] = y.astype(BF16)


def _combine_ln(x, ys, n_groups, chunk_ids, d0c, d1c, lnp, layer):
    row = lambda i, ng, ck: (i, 0)

    def ln_spec(kind):
        r = (2 + kind) * DEPTH + layer
        return pl.BlockSpec((None, 1, D_MODEL), lambda i, ng, ck: (r, 0, 0))

    grid_spec = pltpu.PrefetchScalarGridSpec(
        num_scalar_prefetch=2,
        grid=(N_TILES,),
        in_specs=[pl.BlockSpec((BM, 1), row),
                  pl.BlockSpec((BM, 1), row),
                  pl.BlockSpec((BM, D_MODEL), row),
                  pl.BlockSpec(memory_space=pl.ANY),
                  ln_spec(0), ln_spec(1)],
        out_specs=[pl.BlockSpec((BM, D_MODEL), row), pl.BlockSpec((BM, D_MODEL), row)],
        scratch_shapes=[pltpu.VMEM((2, COMBINE_GROUP * MOE_SUB, D_MODEL), BF16),
                        pltpu.SemaphoreType.DMA((2,)),
                        pltpu.VMEM((BM, D_MODEL), F32)],
    )
    return pl.pallas_call(
        _combine_ln_kernel,
        grid_spec=grid_spec,
        out_shape=[jax.ShapeDtypeStruct((N_TOK, D_MODEL), F32), jax.ShapeDtypeStruct((N_TOK, D_MODEL), BF16)],
        compiler_params=_params(("arbitrary",), 56),
        name="moe_combine_ln",
    )(n_groups, chunk_ids, d0c, d1c, x, ys, lnp, lnp)


def _token_rows(a, fill):
    pad = TOK_CHUNKS * TOK_CHUNK - N_TOK
    return jnp.pad(a, (0, pad), constant_values=fill).reshape(TOK_CHUNKS, 1, TOK_CHUNK)


def _moe_layer(x, xb, w_router_padded, w_gu, w_down, j, lnp, layer):
    sel, gates = _router(x, w_router_padded, j)
    sel = sel[:, :N_EXPERTS]
    gates = gates[:, :N_EXPERTS]

    cum = jnp.cumsum(sel, axis=0)
    rank = cum - sel
    counts = cum[-1]
    padded = (counts + MOE_PAD - 1) // MOE_PAD * MOE_PAD
    pad_end = jnp.cumsum(padded)
    first = pad_end - padded + (padded - counts) // MOE_SUB * MOE_SUB
    slot = first[None, :] + rank
    chosen = sel > 0
    d0 = jnp.min(jnp.where(chosen, slot, MOE_SLOTS), axis=1).astype(jnp.int32)
    d1 = jnp.max(jnp.where(chosen, slot, -1), axis=1).astype(jnp.int32)
    g0 = jnp.sum(jnp.where(chosen & (slot == d0[:, None]), gates, 0.0), axis=1)
    g1 = jnp.sum(jnp.where(chosen & (slot == d1[:, None]), gates, 0.0), axis=1)

    def tiles(rows_per_tile):
        start = jnp.arange(MOE_SLOTS // rows_per_tile, dtype=jnp.int32) * rows_per_tile
        e = jnp.minimum(jnp.sum(pad_end[None, :] <= start[:, None], axis=1), N_EXPERTS - 1).astype(jnp.int32)
        r0 = jnp.maximum(start - first[e], 0)
        valid = jnp.clip(jnp.minimum(first[e] + counts[e], start + rows_per_tile) - jnp.maximum(first[e], start),
                         0, rows_per_tile)
        return e, r0, valid

    def matmul_tiles(rows_per_tile):
        e, _, valid = tiles(rows_per_tile)
        live_subs = ((valid + MOE_SUB - 1) // MOE_SUB).astype(jnp.int32)
        opens = jnp.concatenate([jnp.ones((1,), bool), (e[1:] != e[:-1]) | (live_subs[:-1] == 0)])
        fresh = ((live_subs > 0) & opens).astype(jnp.int32)
        n_live = (pad_end[-1:] // rows_per_tile).astype(jnp.int32)
        return e, live_subs, fresh, n_live

    sub_e, r0, valid = tiles(MOE_SUB)
    cum_rows = cum.T[sub_e]
    tok_lo = jnp.sum(cum_rows < (r0 + 1)[:, None], axis=1)
    tok_hi = jnp.sum(cum_rows < (r0 + valid)[:, None], axis=1)
    chunk_lo = jnp.where(valid > 0, tok_lo // TOK_CHUNK, 1).astype(jnp.int32)
    chunk_hi = jnp.where(valid > 0, tok_hi // TOK_CHUNK, 0).astype(jnp.int32)

    rank_edges = jnp.concatenate([jnp.zeros((1, N_EXPERTS), cum.dtype), cum[BM - 1::BM]], axis=0)
    lo = first[None, :] + rank_edges[:-1]
    hi = first[None, :] + rank_edges[1:]
    per_expert = COMBINE_KMAX // N_EXPERTS
    cand = (lo // MOE_SUB)[:, :, None] + jnp.arange(per_expert)[None, None, :]
    ok = (hi > lo)[:, :, None] & (cand <= ((hi - 1) // MOE_SUB)[:, :, None])
    cand = cand.reshape(N_TILES, COMBINE_KMAX)
    ok = ok.reshape(N_TILES, COMBINE_KMAX)
    order = jnp.argsort(jnp.logical_not(ok), axis=1, stable=True)
    chunk_ids = jnp.where(jnp.take_along_axis(ok, order, axis=1), jnp.take_along_axis(cand, order, axis=1), -1)
    chunk_ids = chunk_ids.astype(jnp.int32).reshape(-1)
    n_groups = ((jnp.sum(ok, axis=1) + COMBINE_GROUP - 1) // COMBINE_GROUP).astype(jnp.int32)

    xs, gs = _dispatch(xb, chunk_lo, chunk_hi, _token_rows(d0, -1), _token_rows(d1, -1),
                       _token_rows(g0, 0.0), _token_rows(g1, 0.0))
    h = _moe_gu(xs, w_gu, j, *matmul_tiles(MOE_GU_BM))
    ys = _moe_down(h, w_down, j, gs, *matmul_tiles(MOE_DOWN_BM))
    return _combine_ln(x, ys, n_groups, chunk_ids, d0[:, None], d1[:, None], lnp, layer)


def _block_diag_causal(w_s):
    small = jnp.tril(w_s[:, :DEC_SEQ, :DEC_SEQ])
    eye = jnp.eye(DEC_BATCH, dtype=w_s.dtype)
    return jnp.einsum("ab,gij->gaibj", eye, small).reshape(SGU_GROUPS, CHUNK, CHUNK)


def kernel(x_prompt, x_sample, state_swa_k, state_swa_v, state_conv, ln1_g, ln1_b, ln2_g, ln2_b, a_w_in, a_ln_g, a_ln_b, a_w_s, a_b_s, a_w_out, b_w_qkv, b_sinks, b_w_o, c_w_in, c_w_conv, c_w_out, f_w_gu, f_w_down, m_w_router, m_w_gu, m_w_down):
    x = jnp.concatenate([x_prompt.reshape(N_PROMPT, D_MODEL), x_sample.reshape(N_SAMPLE, D_MODEL)], axis=0)
    xb = x.astype(BF16)
    lnp = jnp.concatenate([ln1_g, ln1_b, ln2_g, ln2_b], axis=0).reshape(4 * DEPTH, 1, D_MODEL)
    a_lnp = jnp.concatenate([a_ln_g, a_ln_b], axis=0).reshape(-1, 1, D_MODEL)
    a_w_in_b, a_w_out_b = a_w_in.astype(BF16), a_w_out.astype(BF16)
    b_w_qkv_b, b_w_o_b = b_w_qkv.astype(BF16), b_w_o.astype(BF16)
    c_w_in_b, c_w_out_b = c_w_in.astype(BF16), c_w_out.astype(BF16)
    f_w_gu_b, f_w_down_b = f_w_gu.astype(BF16), f_w_down.astype(BF16)
    w_router_padded = jnp.pad(m_w_router, ((0, 0), (0, 0), (0, 128 - N_EXPERTS)))

    sgu_v, swa_states, conv_states = [], [], []
    for i in range(DEPTH):
        kind, j = i % N_MIXERS, i // N_MIXERS
        if kind == 0:
            z2 = _sgu_in(xb, a_w_in_b, a_lnp, j)
            sgu_v.append(z2[N_PROMPT:, D_MODEL:].reshape(DEC_BATCH, DEC_SEQ, D_MODEL))
            b_t = a_b_s[j].T
            b_t_sample = jnp.tile(b_t[:DEC_SEQ], (DEC_BATCH, 1))
            x, xb = _sgu_out(z2, a_w_s, _block_diag_causal(a_w_s[j]), b_t, b_t_sample, a_w_out_b, j, x, lnp, i)
        elif kind == 1:
            x, xb, st = _swa_mixer(xb, x, state_swa_k[j], state_swa_v[j], b_w_qkv_b, b_sinks[j], b_w_o_b, j, lnp, i)
            swa_states.append(st)
        else:
            x, xb, st = _conv_mixer(xb, x, state_conv[j], c_w_in_b, c_w_conv, c_w_out_b, j, lnp, i)
            conv_states.append(st)
        if i % 2 == 0:
            h = _dense_gu(xb, f_w_gu_b, i // 2)
            x, xb = _mm_ln(h, f_w_down_b, i // 2, x, lnp, 1, i, 320, "dense_down")
        else:
            x, xb = _moe_layer(x, xb, w_router_padded, m_w_gu, m_w_down, i // 2, lnp, i)

    y_prompt = x[:N_PROMPT].reshape(BATCH, SEQ, D_MODEL)
    y_sample = x[N_PROMPT:].reshape(DEC_BATCH, DEC_SEQ, D_MODEL)
    return (y_prompt, y_sample, jnp.stack(sgu_v),
            jnp.stack([s[0] for s in swa_states]), jnp.stack([s[1] for s in swa_states]),
            jnp.stack([s[2] for s in swa_states]), jnp.stack([s[3] for s in swa_states]),
            jnp.stack([s[0] for s in conv_states]), jnp.stack([s[1] for s in conv_states]))
```

```python
import functools

import numpy as np
import jax
import jax.numpy as jnp
from jax import lax
from jax.experimental import pallas as pl
from jax.experimental.pallas import tpu as pltpu

D_MODEL = 2048
BATCH = 2
SEQ = 4096
DEPTH = 4
DEC_BATCH = 32
DEC_SEQ = 4
N_MIXERS = 3
CHUNK = 128
SGU_GROUPS = 8
SGU_GROUP_DIM = D_MODEL // SGU_GROUPS
HEAD_DIM = 64
N_HEADS = D_MODEL // HEAD_DIM
N_KV_HEADS = N_HEADS // 8
GQA_GROUP = N_HEADS // N_KV_HEADS
WINDOW = 128
CONV_WIDTH = 3
D_FF = 5632
N_EXPERTS = 8
LN_EPS = 1e-5
DEEPNORM_ALPHA = (2.0 * DEPTH) ** 0.25
NEG_INF = -1e30

N_PROMPT = BATCH * SEQ
N_SAMPLE = DEC_BATCH * DEC_SEQ
N_TOK = N_PROMPT + N_SAMPLE
BM = 640
N_TILES = N_TOK // BM
SGU_BM = 128
SGU_TILES = N_TOK // SGU_BM

MOE_SUB = 256
MOE_PAD = 4 * MOE_SUB
MOE_SLOTS = (-(-(2 * N_TOK) // MOE_PAD) + N_EXPERTS) * MOE_PAD
MOE_GU_BM = MOE_PAD
MOE_DOWN_BM = 2 * MOE_SUB
MOE_BN = 512
COMBINE_GROUP = 4
TOK_CHUNK = 512
TOK_CHUNKS_FULL = N_TOK // TOK_CHUNK
TOK_TAIL = N_TOK - TOK_CHUNKS_FULL * TOK_CHUNK
TOK_CHUNKS = TOK_CHUNKS_FULL + 1
COMBINE_KMAX = N_EXPERTS * ((MOE_SUB - 1 + BM - 1) // MOE_SUB + 1)

F32 = jnp.float32
BF16 = jnp.bfloat16
MIB = 1024 * 1024


def _params(semantics, vmem_mib):
    return pltpu.CompilerParams(dimension_semantics=semantics, vmem_limit_bytes=vmem_mib * MIB)


def _layer_norm(y, g, b):
    mu = jnp.mean(y, axis=-1, keepdims=True)
    d = y - mu
    var = jnp.mean(d * d, axis=-1, keepdims=True)
    return d * lax.rsqrt(var + LN_EPS) * g + b


def _gelu(x):
    return 0.5 * x * (1.0 + lax.erf(x * np.float32(np.sqrt(0.5))))


def _dot(a, b):
    return jnp.dot(a, b, preferred_element_type=F32)


def _ln_specs(norm, layer, ngrid):
    def spec(kind):
        r = (2 * norm + kind) * DEPTH + layer
        if ngrid == 1:
            return pl.BlockSpec((None, 1, D_MODEL), lambda i: (r, 0, 0))
        return pl.BlockSpec((None, 1, D_MODEL), lambda j, i: (r, 0, 0))
    return [spec(0), spec(1)]


def _mm_kernel(x_ref, w_ref, o_ref):
    o_ref[...] = _dot(x_ref[...], w_ref[...]).astype(o_ref.dtype)


def _mm(x, w, layer, bn, out_dtype, name):
    m, k = x.shape
    n = w.shape[-1]
    return pl.pallas_call(
        _mm_kernel,
        grid=(n // bn, m // BM),
        in_specs=[pl.BlockSpec((BM, k), lambda j, i: (i, 0)),
                  pl.BlockSpec((None, k, bn), lambda j, i: (layer, 0, j))],
        out_specs=pl.BlockSpec((BM, bn), lambda j, i: (i, j)),
        out_shape=jax.ShapeDtypeStruct((m, n), out_dtype),
        compiler_params=_params(("arbitrary", "arbitrary"), 48),
        name=name,
    )(x, w)


def _mm_ln_kernel(x_ref, w_ref, r_ref, g_ref, b_ref, of_ref, ob_ref):
    bm = x_ref.shape[0]
    group = bm // 4 if x_ref.shape[1] <= D_MODEL else bm // 2
    for start in range(0, bm, group):
        rows = slice(start, start + group)
        f = _dot(x_ref[rows, :], w_ref[...])
        y = _layer_norm(DEEPNORM_ALPHA * r_ref[rows, :] + f, g_ref[...], b_ref[...])
        of_ref[rows, :] = y
        ob_ref[rows, :] = y.astype(BF16)


def _mm_ln(x, w, w_layer, resid, lnp, norm, layer, bm, name):
    m, k = x.shape
    row = lambda i: (i, 0)
    return pl.pallas_call(
        _mm_ln_kernel,
        grid=(m // bm,),
        in_specs=[pl.BlockSpec((bm, k), row),
                  pl.BlockSpec((None, k, D_MODEL), lambda i: (w_layer, 0, 0), pipeline_mode=pl.Buffered(1)),
                  pl.BlockSpec((bm, D_MODEL), row)] + _ln_specs(norm, layer, 1),
        out_specs=[pl.BlockSpec((bm, D_MODEL), row), pl.BlockSpec((bm, D_MODEL), row)],
        out_shape=[jax.ShapeDtypeStruct((m, D_MODEL), F32), jax.ShapeDtypeStruct((m, D_MODEL), BF16)],
        compiler_params=_params(("arbitrary",), 56),
        name=name,
    )(x, w, resid, lnp, lnp)


def _sgu_in_kernel(x_ref, w_ref, g_ref, b_ref, o_ref):
    half = BM // 2

    @pl.when(pl.program_id(0) == 0)
    def _():
        for rows in (slice(0, half), slice(half, BM)):
            o_ref[rows, :] = _gelu(_dot(x_ref[rows, :], w_ref[...]))

    @pl.when(pl.program_id(0) == 1)
    def _():
        for rows in (slice(0, half), slice(half, BM)):
            o_ref[rows, :] = _layer_norm(_gelu(_dot(x_ref[rows, :], w_ref[...])), g_ref[...], b_ref[...])


def _sgu_in(xb, w_in, a_lnp, j):
    return pl.pallas_call(
        _sgu_in_kernel,
        grid=(2, N_TILES),
        in_specs=[pl.BlockSpec((BM, D_MODEL), lambda c, i: (i, 0)),
                  pl.BlockSpec((None, D_MODEL, D_MODEL), lambda c, i: (j, 0, c)),
                  pl.BlockSpec((None, 1, D_MODEL), lambda c, i: (j, 0, 0)),
                  pl.BlockSpec((None, 1, D_MODEL), lambda c, i: (2 + j, 0, 0))],
        out_specs=pl.BlockSpec((BM, D_MODEL), lambda c, i: (i, c)),
        out_shape=jax.ShapeDtypeStruct((N_TOK, 2 * D_MODEL), F32),
        compiler_params=_params(("arbitrary", "arbitrary"), 56),
        name="sgu_in",
    )(xb, w_in, a_lnp, a_lnp)


def _sgu_out_kernel(u_ref, v_ref, wp_ref, ws_ref, bp_ref, bs_ref, wo_ref, r_ref, g_ref, b_ref, of_ref, ob_ref):
    is_sample_tile = pl.program_id(0) == SGU_TILES - 1
    row = lax.broadcasted_iota(jnp.int32, (CHUNK, CHUNK), 0)
    col = lax.broadcasted_iota(jnp.int32, (CHUNK, CHUNK), 1)
    causal = row >= col
    groups = range(SGU_GROUPS)
    cols = [slice(grp * SGU_GROUP_DIM, (grp + 1) * SGU_GROUP_DIM) for grp in groups]
    w = [jnp.where(is_sample_tile, ws_ref[grp], jnp.where(causal, wp_ref[grp], 0.0)).astype(BF16) for grp in groups]
    bias = [jnp.where(is_sample_tile, bs_ref[:, grp:grp + 1], bp_ref[:, grp:grp + 1]) for grp in groups]
    mixed = [_dot(w[grp], v_ref[:, cols[grp]].astype(BF16)) + bias[grp] for grp in groups]
    gated = jnp.concatenate([(u_ref[:, cols[grp]] * mixed[grp]).astype(BF16) for grp in groups], axis=1)
    f = _dot(gated, wo_ref[...])
    y = _layer_norm(DEEPNORM_ALPHA * r_ref[...] + f, g_ref[...], b_ref[...])
    of_ref[...] = y
    ob_ref[...] = y.astype(BF16)


def _sgu_out(z2, w_s, w_s_sample, b_t, b_t_sample, w_out, j, resid, lnp, layer):
    row = lambda i: (i, 0)
    const2 = lambda i: (0, 0)
    const3 = lambda i: (0, 0, 0)
    return pl.pallas_call(
        _sgu_out_kernel,
        grid=(SGU_TILES,),
        in_specs=[pl.BlockSpec((SGU_BM, D_MODEL), lambda i: (i, 0)),
                  pl.BlockSpec((SGU_BM, D_MODEL), lambda i: (i, 1)),
                  pl.BlockSpec((None, SGU_GROUPS, CHUNK, CHUNK), lambda i: (j, 0, 0, 0)),
                  pl.BlockSpec((SGU_GROUPS, CHUNK, CHUNK), const3),
                  pl.BlockSpec((CHUNK, SGU_GROUPS), const2),
                  pl.BlockSpec((CHUNK, SGU_GROUPS), const2),
                  pl.BlockSpec((None, D_MODEL, D_MODEL), lambda i: (j, 0, 0), pipeline_mode=pl.Buffered(1)),
                  pl.BlockSpec((SGU_BM, D_MODEL), row)] + _ln_specs(0, layer, 1),
        out_specs=[pl.BlockSpec((SGU_BM, D_MODEL), row), pl.BlockSpec((SGU_BM, D_MODEL), row)],
        out_shape=[jax.ShapeDtypeStruct((N_TOK, D_MODEL), F32), jax.ShapeDtypeStruct((N_TOK, D_MODEL), BF16)],
        compiler_params=_params(("arbitrary",), 40),
        name="sgu_out",
    )(z2, z2, w_s, w_s_sample, b_t, b_t_sample, w_out, resid, lnp, lnp)


def _gu_kernel(x_ref, wg_ref, wu_ref, o_ref):
    half = BM // 2
    for rows in (slice(0, half), slice(half, BM)):
        x = x_ref[rows, :]
        gate = _dot(x, wg_ref[...])
        up = _dot(x, wu_ref[...])
        o_ref[rows, :] = (jax.nn.silu(gate) * up).astype(o_ref.dtype)


def _dense_gu(xb, w_gu, layer):
    bn = 1408
    nt = D_FF // bn
    return pl.pallas_call(
        _gu_kernel,
        grid=(nt, N_TILES),
        in_specs=[pl.BlockSpec((BM, D_MODEL), lambda j, i: (i, 0)),
                  pl.BlockSpec((None, D_MODEL, bn), lambda j, i: (layer, 0, j)),
                  pl.BlockSpec((None, D_MODEL, bn), lambda j, i: (layer, 0, j + nt))],
        out_specs=pl.BlockSpec((BM, bn), lambda j, i: (i, j)),
        out_shape=jax.ShapeDtypeStruct((N_TOK, D_FF), BF16),
        compiler_params=_params(("arbitrary", "arbitrary"), 56),
        name="dense_gu",
    )(xb, w_gu, w_gu)


def _attn_kernel(sink_ref, q_ref, kp_ref, vp_ref, kc_ref, vc_ref, o_ref, *, blocks_per_seq):
    has_prev = (pl.program_id(0) % blocks_per_seq) != 0
    qi = lax.broadcasted_iota(jnp.int32, (WINDOW, 2 * WINDOW), 0)
    kj = lax.broadcasted_iota(jnp.int32, (WINDOW, 2 * WINDOW), 1)
    delta_i = WINDOW + qi - kj
    valid = (delta_i >= 0) & (delta_i <= WINDOW) & (has_prev | (kj >= WINDOW))
    delta = delta_i.astype(F32)
    low_half = lax.broadcasted_iota(jnp.int32, (2 * WINDOW, 128), 1) < HEAD_DIM

    kk = jnp.concatenate([kp_ref[...], kc_ref[...]], axis=0)
    vv = jnp.concatenate([vp_ref[...], vc_ref[...]], axis=0)

    def halves(slab, kv):
        if kv % 2 == 0:
            lo = jnp.where(low_half, slab, 0.0)
            hi = jnp.where(low_half, 0.0, pltpu.roll(slab, HEAD_DIM, 1))
        else:
            lo = jnp.where(low_half, pltpu.roll(slab, HEAD_DIM, 1), 0.0)
            hi = jnp.where(low_half, 0.0, slab)
        return lo.astype(BF16), hi.astype(BF16)

    for kv in range(N_KV_HEADS):
        lanes = slice((kv // 2) * 128, (kv // 2 + 1) * 128)
        k_both = jnp.concatenate(halves(kk[:, lanes], kv), axis=0)
        v_both = jnp.concatenate(halves(vv[:, lanes], kv), axis=0)
        pairs = range(GQA_GROUP // 2)
        heads = range(GQA_GROUP)
        qcols = [slice((kv * 4 + pair) * 128, (kv * 4 + pair + 1) * 128) for pair in pairs]
        scores = [lax.dot_general(q_ref[:, qcols[pair]].astype(BF16), k_both, (((1,), (1,)), ((), ())),
                                  preferred_element_type=F32) * (HEAD_DIM ** -0.5) for pair in pairs]
        slopes = [float(2.0 ** (-8.0 * (kv * GQA_GROUP + g + 1) / N_HEADS)) for g in heads]
        sinks = [sink_ref[kv * GQA_GROUP + g] for g in heads]
        s = [scores[g // 2][:, (g % 2) * 2 * WINDOW:(g % 2 + 1) * 2 * WINDOW] - slopes[g] * delta for g in heads]
        s = [jnp.where(valid, s[g], NEG_INF) for g in heads]
        mx = [jnp.maximum(jnp.max(s[g], axis=-1, keepdims=True), sinks[g]) for g in heads]
        p = [jnp.exp(s[g] - mx[g]) for g in heads]
        den = [jnp.sum(p[g], axis=-1, keepdims=True) + jnp.exp(sinks[g] - mx[g]) for g in heads]
        p = [(p[g] / den[g]).astype(BF16) for g in heads]
        for pair in pairs:
            both = jnp.concatenate([p[2 * pair], p[2 * pair + 1]], axis=1)
            o_ref[:, qcols[pair]] = _dot(both, v_both).astype(o_ref.dtype)


def _attention(sinks, q, kp, vp, kc, vc, n_blocks, specs, blocks_per_seq, name):
    return pl.pallas_call(
        functools.partial(_attn_kernel, blocks_per_seq=blocks_per_seq),
        grid=(n_blocks,),
        in_specs=[pl.BlockSpec(memory_space=pltpu.SMEM)] + specs,
        out_specs=pl.BlockSpec((WINDOW, D_MODEL), lambda i: (i, 0)),
        out_shape=jax.ShapeDtypeStruct((n_blocks * WINDOW, D_MODEL), BF16),
        compiler_params=_params(("arbitrary",), 32),
        name=name,
    )(sinks, q, kp, vp, kc, vc)


SAMPLE_ROWS = GQA_GROUP * DEC_SEQ
SAMPLE_KEYS = 2 * WINDOW


def _attn_sample_kernel(q_ref, k_ref, v_ref, sink_ref, slope_ref, o_ref):
    step = lax.broadcasted_iota(jnp.int32, (SAMPLE_ROWS, SAMPLE_KEYS), 0) % DEC_SEQ
    kj = lax.broadcasted_iota(jnp.int32, (SAMPLE_ROWS, SAMPLE_KEYS), 1)
    delta_i = WINDOW + step - kj
    valid = (delta_i >= 0) & (delta_i <= WINDOW)
    delta = delta_i.astype(F32)
    kvs = range(N_KV_HEADS)
    lanes = [slice((kv // 2) * 128, (kv // 2 + 1) * 128) for kv in kvs]
    s = [lax.dot_general(q_ref[kv].astype(BF16), k_ref[:, lanes[kv]].astype(BF16), (((1,), (1,)), ((), ())),
                         preferred_element_type=F32) * (HEAD_DIM ** -0.5) for kv in kvs]
    s = [jnp.where(valid, s[kv] - slope_ref[kv] * delta, NEG_INF) for kv in kvs]
    mx = [jnp.maximum(jnp.max(s[kv], axis=-1, keepdims=True), sink_ref[kv]) for kv in kvs]
    p = [jnp.exp(s[kv] - mx[kv]) for kv in kvs]
    den = [jnp.sum(p[kv], axis=-1, keepdims=True) + jnp.exp(sink_ref[kv] - mx[kv]) for kv in kvs]
    p = [(p[kv] / den[kv]).astype(BF16) for kv in kvs]
    for kv in kvs:
        o_ref[kv] = _dot(p[kv], v_ref[:, lanes[kv]].astype(BF16)).astype(o_ref.dtype)


def _attn_sample(q_s, k_new, v_new, state_k, state_v, sinks):
    kvw = N_KV_HEADS * HEAD_DIM
    q = q_s.reshape(DEC_BATCH, DEC_SEQ, N_KV_HEADS, GQA_GROUP, HEAD_DIM).transpose(0, 2, 3, 1, 4)
    q = q.reshape(DEC_BATCH, N_KV_HEADS, SAMPLE_ROWS, HEAD_DIM)
    zero = jnp.zeros_like(q)
    odd = (jnp.arange(N_KV_HEADS) % 2 == 1)[None, :, None, None]
    q = jnp.where(odd, jnp.concatenate([zero, q], axis=-1), jnp.concatenate([q, zero], axis=-1))

    def keys(state, new):
        pad = jnp.zeros((DEC_BATCH, SAMPLE_KEYS - WINDOW - DEC_SEQ, kvw), F32)
        return jnp.concatenate([state.reshape(DEC_BATCH, WINDOW, kvw), new, pad], axis=1)

    head = np.arange(N_KV_HEADS)[:, None] * GQA_GROUP + np.arange(SAMPLE_ROWS)[None, :] // DEC_SEQ
    slopes = jnp.asarray(np.exp2(-8.0 * (head + 1) / N_HEADS), F32)[:, :, None]
    sink_rows = sinks[head][:, :, None]
    seq3 = lambda i: (i, 0, 0)
    whole = lambda i: (0, 0, 0)
    out = pl.pallas_call(
        _attn_sample_kernel,
        grid=(DEC_BATCH,),
        in_specs=[pl.BlockSpec((None, N_KV_HEADS, SAMPLE_ROWS, 128), lambda i: (i, 0, 0, 0)),
                  pl.BlockSpec((None, SAMPLE_KEYS, kvw), seq3),
                  pl.BlockSpec((None, SAMPLE_KEYS, kvw), seq3),
                  pl.BlockSpec((N_KV_HEADS, SAMPLE_ROWS, 1), whole),
                  pl.BlockSpec((N_KV_HEADS, SAMPLE_ROWS, 1), whole)],
        out_specs=pl.BlockSpec((None, N_KV_HEADS, SAMPLE_ROWS, 128), lambda i: (i, 0, 0, 0)),
        out_shape=jax.ShapeDtypeStruct((DEC_BATCH, N_KV_HEADS, SAMPLE_ROWS, 128), BF16),
        compiler_params=_params(("arbitrary",), 16),
        name="attn_sample",
    )(q, keys(state_k, k_new), keys(state_v, v_new), sink_rows, slopes)
    out = jnp.where(odd, out[..., HEAD_DIM:], out[..., :HEAD_DIM])
    out = out.reshape(DEC_BATCH, N_KV_HEADS, GQA_GROUP, DEC_SEQ, HEAD_DIM).transpose(0, 3, 1, 2, 4)
    return out.reshape(N_SAMPLE, D_MODEL)


def _swa_mixer(xb, x, state_k, state_v, w_qkv, sinks, w_o, j, lnp, layer):
    kvw = N_KV_HEADS * HEAD_DIM
    qkv = _mm(xb, w_qkv, j, 1280, F32, "qkv")
    kcol, vcol = D_MODEL // kvw, D_MODEL // kvw + 1
    nb = SEQ // WINDOW

    def prev_block(i):
        return jnp.maximum(i - 1, 0)

    prompt_specs = [pl.BlockSpec((WINDOW, D_MODEL), lambda i: (i, 0)),
                    pl.BlockSpec((WINDOW, kvw), lambda i: (prev_block(i), kcol)),
                    pl.BlockSpec((WINDOW, kvw), lambda i: (prev_block(i), vcol)),
                    pl.BlockSpec((WINDOW, kvw), lambda i: (i, kcol)),
                    pl.BlockSpec((WINDOW, kvw), lambda i: (i, vcol))]
    o_prompt = _attention(sinks, qkv, qkv, qkv, qkv, qkv, N_PROMPT // WINDOW, prompt_specs, nb, "attn_prompt")

    qkv_s = qkv[N_PROMPT:].reshape(DEC_BATCH, DEC_SEQ, D_MODEL + 2 * kvw)
    k_new = qkv_s[..., D_MODEL:D_MODEL + kvw]
    v_new = qkv_s[..., D_MODEL + kvw:]
    o_sample = _attn_sample(qkv_s[..., :D_MODEL], k_new, v_new, state_k, state_v, sinks)
    o_all = jnp.concatenate([o_prompt, o_sample], axis=0)
    x, xb = _mm_ln(o_all, w_o, j, x, lnp, 0, layer, BM, "attn_out")

    k_p = qkv[:N_PROMPT, D_MODEL:D_MODEL + kvw].reshape(BATCH, SEQ, N_KV_HEADS, HEAD_DIM)[:, -WINDOW:]
    v_p = qkv[:N_PROMPT, D_MODEL + kvw:].reshape(BATCH, SEQ, N_KV_HEADS, HEAD_DIM)[:, -WINDOW:]
    k_s = jnp.concatenate([state_k, k_new.reshape(DEC_BATCH, DEC_SEQ, N_KV_HEADS, HEAD_DIM)], axis=1)[:, -WINDOW:]
    v_s = jnp.concatenate([state_v, v_new.reshape(DEC_BATCH, DEC_SEQ, N_KV_HEADS, HEAD_DIM)], axis=1)[:, -WINDOW:]
    return x, xb, (k_p, v_p, k_s, v_s)


CONV_BM = 512
CONV_BC = 1024


def _conv_taps(z, z1, z2, gb, wc_ref):
    conv = wc_ref[0:1, :] * z2 + wc_ref[1:2, :] * z1 + wc_ref[2:3, :] * z
    return (gb * conv).astype(BF16)


def _conv_prompt_kernel(gb_ref, gc_ref, h_ref, wc_ref, y_ref, z_ref, carry_ref):
    @pl.when(pl.program_id(1) % (SEQ // CONV_BM) == 0)
    def _():
        carry_ref[...] = jnp.zeros_like(carry_ref)

    z = gc_ref[...] * h_ref[...]
    row = lax.broadcasted_iota(jnp.int32, z.shape, 0)
    c6 = carry_ref[6:7, :]
    c7 = carry_ref[7:8, :]
    z1 = jnp.where(row >= 1, pltpu.roll(z, 1, 0), c7)
    z2 = jnp.where(row >= 2, pltpu.roll(z, 2, 0), jnp.where(row == 1, c7, c6))
    y_ref[...] = _conv_taps(z, z1, z2, gb_ref[...], wc_ref)
    z_ref[...] = z
    carry_ref[...] = z[CONV_BM - 8:, :]


def _conv_sample_kernel(gb_ref, gc_ref, h_ref, wc_ref, pa_ref, pb_ref, y_ref, z_ref):
    z = gc_ref[...] * h_ref[...]
    t = lax.broadcasted_iota(jnp.int32, z.shape, 0) % DEC_SEQ
    z1 = jnp.where(t >= 1, pltpu.roll(z, 1, 0), pa_ref[...])
    z2 = jnp.where(t >= 2, pltpu.roll(z, 2, 0), pb_ref[...])
    y_ref[...] = _conv_taps(z, z1, z2, gb_ref[...], wc_ref)
    z_ref[...] = z


def _conv_mixer(xb, x, state, w_in, w_conv, w_out, j, lnp, layer):
    proj = _mm(xb, w_in, j, 1024, F32, "conv_in")
    ncb = D_MODEL // CONV_BC
    y_p, z_p = pl.pallas_call(
        _conv_prompt_kernel,
        grid=(ncb, N_PROMPT // CONV_BM),
        in_specs=[pl.BlockSpec((CONV_BM, CONV_BC), lambda c, i: (i, c)),
                  pl.BlockSpec((CONV_BM, CONV_BC), lambda c, i: (i, ncb + c)),
                  pl.BlockSpec((CONV_BM, CONV_BC), lambda c, i: (i, 2 * ncb + c)),
                  pl.BlockSpec((None, CONV_WIDTH, CONV_BC), lambda c, i: (j, 0, c))],
        out_specs=[pl.BlockSpec((CONV_BM, CONV_BC), lambda c, i: (i, c)),
                   pl.BlockSpec((CONV_BM, CONV_BC), lambda c, i: (i, c))],
        out_shape=[jax.ShapeDtypeStruct((N_PROMPT, D_MODEL), BF16), jax.ShapeDtypeStruct((N_PROMPT, D_MODEL), F32)],
        scratch_shapes=[pltpu.VMEM((8, CONV_BC), F32)],
        compiler_params=_params(("arbitrary", "arbitrary"), 40),
        name="conv_prompt",
    )(proj, proj, proj, w_conv)

    zeros = jnp.zeros((DEC_BATCH, DEC_SEQ, D_MODEL), F32)
    past_a = zeros.at[:, 0].set(state[:, 1]).reshape(N_SAMPLE, D_MODEL)
    past_b = zeros.at[:, 0].set(state[:, 0]).at[:, 1].set(state[:, 1]).reshape(N_SAMPLE, D_MODEL)
    srow = N_PROMPT // N_SAMPLE
    whole = lambda i: (0, 0)
    y_s, z_s = pl.pallas_call(
        _conv_sample_kernel,
        grid=(1,),
        in_specs=[pl.BlockSpec((N_SAMPLE, D_MODEL), lambda i: (srow, 0)),
                  pl.BlockSpec((N_SAMPLE, D_MODEL), lambda i: (srow, 1)),
                  pl.BlockSpec((N_SAMPLE, D_MODEL), lambda i: (srow, 2)),
                  pl.BlockSpec((None, CONV_WIDTH, D_MODEL), lambda i: (j, 0, 0)),
                  pl.BlockSpec((N_SAMPLE, D_MODEL), whole),
                  pl.BlockSpec((N_SAMPLE, D_MODEL), whole)],
        out_specs=[pl.BlockSpec((N_SAMPLE, D_MODEL), whole), pl.BlockSpec((N_SAMPLE, D_MODEL), whole)],
        out_shape=[jax.ShapeDtypeStruct((N_SAMPLE, D_MODEL), BF16), jax.ShapeDtypeStruct((N_SAMPLE, D_MODEL), F32)],
        compiler_params=_params(("arbitrary",), 32),
        name="conv_sample",
    )(proj, proj, proj, w_conv, past_a, past_b)

    y_all = jnp.concatenate([y_p, y_s], axis=0)
    x, xb = _mm_ln(y_all, w_out, j, x, lnp, 0, layer, BM, "conv_out")
    conv_p = z_p.reshape(BATCH, SEQ, D_MODEL)[:, -(CONV_WIDTH - 1):]
    conv_s = z_s.reshape(DEC_BATCH, DEC_SEQ, D_MODEL)[:, -(CONV_WIDTH - 1):]
    return x, xb, (conv_p, conv_s)


def _router_kernel(x_ref, w_ref, sel_ref, gate_ref):
    logits = jnp.dot(x_ref[...], w_ref[...], preferred_element_type=F32, precision=lax.Precision.HIGHEST)
    lane = lax.broadcasted_iota(jnp.int32, logits.shape, 1)
    neg = jnp.float32(-jnp.inf)
    l1 = jnp.where(lane < N_EXPERTS, logits, neg)
    m1 = jnp.max(l1, axis=-1, keepdims=True)
    i1 = jnp.min(jnp.where(l1 == m1, lane, 128), axis=-1, keepdims=True)
    l2 = jnp.where(lane == i1, neg, l1)
    m2 = jnp.max(l2, axis=-1, keepdims=True)
    i2 = jnp.min(jnp.where(l2 == m2, lane, 128), axis=-1, keepdims=True)
    e2 = jnp.exp(m2 - m1)
    den = 1.0 + e2
    g1 = 1.0 / den
    g2 = e2 / den
    sel_ref[...] = ((lane == i1) | (lane == i2)).astype(jnp.int32)
    gate_ref[...] = jnp.where(lane == i1, g1, jnp.where(lane == i2, g2, 0.0))


def _router(x, w_router_padded, layer):
    row = lambda i: (i, 0)
    return pl.pallas_call(
        _router_kernel,
        grid=(N_TILES,),
        in_specs=[pl.BlockSpec((BM, D_MODEL), row),
                  pl.BlockSpec((None, D_MODEL, 128), lambda i: (layer, 0, 0))],
        out_specs=[pl.BlockSpec((BM, 128), row), pl.BlockSpec((BM, 128), row)],
        out_shape=[jax.ShapeDtypeStruct((N_TOK, 128), jnp.int32), jax.ShapeDtypeStruct((N_TOK, 128), F32)],
        compiler_params=_params(("arbitrary",), 32),
        name="router",
    )(x, w_router_padded)


def _dispatch_kernel(lo_ref, hi_ref, d0_ref, d1_ref, g0_ref, g1_ref, x_ref, xs_ref, gs_ref, acc_ref, gacc_ref):
    i = pl.program_id(0)
    lo = lo_ref[i]
    hi = hi_ref[i]
    slot = i * MOE_SUB + lax.broadcasted_iota(jnp.int32, (MOE_SUB, TOK_CHUNK), 0)
    acc_ref[...] = jnp.zeros_like(acc_ref)
    gacc_ref[...] = jnp.zeros_like(gacc_ref)

    def chunk(c, width, x_rows):
        s = slot[:, :width]
        m0 = s == d0_ref[c][:, :width]
        m1 = s == d1_ref[c][:, :width]
        onehot = jnp.where(m0 | m1, 1.0, 0.0).astype(BF16)
        acc_ref[...] += _dot(onehot, x_rows)
        picked = jnp.where(m0, g0_ref[c][:, :width], 0.0) + jnp.where(m1, g1_ref[c][:, :width], 0.0)
        gacc_ref[...] += jnp.sum(picked, axis=1, keepdims=True)

    def body(c, carry):
        start = pl.multiple_of(c * TOK_CHUNK, TOK_CHUNK)
        chunk(c, TOK_CHUNK, x_ref[pl.ds(start, TOK_CHUNK), :])
        return carry

    lax.fori_loop(lo, jnp.minimum(hi, TOK_CHUNKS_FULL - 1) + 1, body, 0)

    @pl.when(hi == TOK_CHUNKS_FULL)
    def _():
        chunk(TOK_CHUNKS_FULL, TOK_TAIL, x_ref[TOK_CHUNKS_FULL * TOK_CHUNK:, :])

    xs_ref[...] = acc_ref[...].astype(BF16)
    gs_ref[...] = gacc_ref[...]


def _dispatch(xb, chunk_lo, chunk_hi, d0r, d1r, g0r, g1r):
    n_tiles = MOE_SLOTS // MOE_SUB
    whole3 = lambda i, lo, hi: (0, 0, 0)
    rows = pl.BlockSpec((TOK_CHUNKS, 1, TOK_CHUNK), whole3)
    grid_spec = pltpu.PrefetchScalarGridSpec(
        num_scalar_prefetch=2,
        grid=(n_tiles,),
        in_specs=[rows, rows, rows, rows,
                  pl.BlockSpec((N_TOK, D_MODEL), lambda i, lo, hi: (0, 0))],
        out_specs=[pl.BlockSpec((MOE_SUB, D_MODEL), lambda i, lo, hi: (i, 0)),
                   pl.BlockSpec((MOE_SUB, 1), lambda i, lo, hi: (i, 0))],
        scratch_shapes=[pltpu.VMEM((MOE_SUB, D_MODEL), F32), pltpu.VMEM((MOE_SUB, 1), F32)],
    )
    return pl.pallas_call(
        _dispatch_kernel,
        grid_spec=grid_spec,
        out_shape=[jax.ShapeDtypeStruct((MOE_SLOTS, D_MODEL), BF16), jax.ShapeDtypeStruct((MOE_SLOTS, 1), F32)],
        compiler_params=_params(("arbitrary",), 52),
        name="moe_dispatch",
    )(chunk_lo, chunk_hi, d0r, d1r, g0r, g1r, xb)


def _for_expert_tile(n_sub, live_subs, fresh, o_ref, compute, load_weights, kept_weights):
    total = n_sub * MOE_SUB

    def run(k, weights):
        dead = (n_sub - k) * MOE_SUB
        o_ref[dead:, :] = compute(slice(dead, total), weights())
        if dead:
            o_ref[:dead, :] = jnp.zeros((dead, o_ref.shape[1]), o_ref.dtype)

    for k in range(1, n_sub + 1):
        pl.when((fresh == 1) & (live_subs == k))(functools.partial(run, k, load_weights))
    pl.when((fresh == 0) & (live_subs == n_sub))(functools.partial(run, n_sub, kept_weights))

    @pl.when(live_subs == 0)
    def _():
        o_ref[...] = jnp.zeros_like(o_ref)


def _moe_gu_kernel(be_ref, ls_ref, fr_ref, nt_ref, x_ref, wg_ref, wu_ref, o_ref, wgb_ref, wub_ref):
    i = pl.program_id(1)

    def load_weights():
        wg = wg_ref[...].astype(BF16)
        wu = wu_ref[...].astype(BF16)
        wgb_ref[...] = wg
        wub_ref[...] = wu
        return wg, wu

    def kept_weights():
        return wgb_ref[...], wub_ref[...]

    def compute(rows, weights):
        x = x_ref[rows, :]
        gate = _dot(x, weights[0])
        up = _dot(x, weights[1])
        return (jax.nn.silu(gate) * up).astype(o_ref.dtype)

    _for_expert_tile(MOE_GU_BM // MOE_SUB, ls_ref[i], fr_ref[i], o_ref, compute, load_weights, kept_weights)


def _live_tile(i, nt):
    return jnp.minimum(i, nt[0] - 1)


def _moe_gu(xs, w_gu, layer, block_e, live_subs, fresh, n_live):
    nt = D_FF // MOE_BN
    grid_spec = pltpu.PrefetchScalarGridSpec(
        num_scalar_prefetch=4,
        grid=(nt, MOE_SLOTS // MOE_GU_BM),
        in_specs=[pl.BlockSpec((MOE_GU_BM, D_MODEL), lambda j, i, be, ls, fr, n: (_live_tile(i, n), 0)),
                  pl.BlockSpec((None, None, D_MODEL, MOE_BN),
                               lambda j, i, be, ls, fr, n: (layer, be[_live_tile(i, n)], 0, j)),
                  pl.BlockSpec((None, None, D_MODEL, MOE_BN),
                               lambda j, i, be, ls, fr, n: (layer, be[_live_tile(i, n)], 0, j + nt))],
        out_specs=pl.BlockSpec((MOE_GU_BM, MOE_BN), lambda j, i, be, ls, fr, n: (i, j)),
        scratch_shapes=[pltpu.VMEM((D_MODEL, MOE_BN), BF16), pltpu.VMEM((D_MODEL, MOE_BN), BF16)],
    )
    return pl.pallas_call(
        _moe_gu_kernel,
        grid_spec=grid_spec,
        out_shape=jax.ShapeDtypeStruct((MOE_SLOTS, D_FF), BF16),
        compiler_params=_params(("arbitrary", "arbitrary"), 48),
        name="moe_gu",
    )(block_e, live_subs, fresh, n_live, xs, w_gu, w_gu)


def _moe_down_kernel(be_ref, ls_ref, fr_ref, nt_ref, h_ref, w_ref, gs_ref, o_ref, wb_ref):
    i = pl.program_id(1)

    def load_weights():
        w = w_ref[...].astype(BF16)
        wb_ref[...] = w
        return w

    def kept_weights():
        return wb_ref[...]

    def compute(rows, w):
        return (_dot(h_ref[rows, :], w) * gs_ref[rows, :]).astype(o_ref.dtype)

    _for_expert_tile(MOE_DOWN_BM // MOE_SUB, ls_ref[i], fr_ref[i], o_ref, compute, load_weights, kept_weights)


def _moe_down(h, w_down, layer, gs, block_e, live_subs, fresh, n_live):
    grid_spec = pltpu.PrefetchScalarGridSpec(
        num_scalar_prefetch=4,
        grid=(D_MODEL // MOE_BN, MOE_SLOTS // MOE_DOWN_BM),
        in_specs=[pl.BlockSpec((MOE_DOWN_BM, D_FF), lambda j, i, be, ls, fr, n: (_live_tile(i, n), 0)),
                  pl.BlockSpec((None, None, D_FF, MOE_BN),
                               lambda j, i, be, ls, fr, n: (layer, be[_live_tile(i, n)], 0, j)),
                  pl.BlockSpec((MOE_DOWN_BM, 1), lambda j, i, be, ls, fr, n: (_live_tile(i, n), 0))],
        out_specs=pl.BlockSpec((MOE_DOWN_BM, MOE_BN), lambda j, i, be, ls, fr, n: (i, j)),
        scratch_shapes=[pltpu.VMEM((D_FF, MOE_BN), BF16)],
    )
    return pl.pallas_call(
        _moe_down_kernel,
        grid_spec=grid_spec,
        out_shape=jax.ShapeDtypeStruct((MOE_SLOTS, D_MODEL), BF16),
        compiler_params=_params(("arbitrary", "arbitrary"), 52),
        name="moe_down",
    )(block_e, live_subs, fresh, n_live, h, w_down, gs)


def _combine_ln_kernel(ng_ref, ck_ref, d0_ref, d1_ref, x_ref, ys_ref, g_ref, b_ref, of_ref, ob_ref,
                       buf_ref, sem_ref, acc_ref):
    i = pl.program_id(0)
    n = ng_ref[i]

    def chunk_id(g, c):
        return ck_ref[i * COMBINE_KMAX + g * COMBINE_GROUP + c]

    def copies(g, which):
        out = []
        for c in range(COMBINE_GROUP):
            cid = chunk_id(g, c)
            start = pl.multiple_of(jnp.maximum(cid, 0) * MOE_SUB, MOE_SUB)
            out.append((cid >= 0, pltpu.make_async_copy(
                ys_ref.at[pl.ds(start, MOE_SUB)],
                buf_ref.at[which, pl.ds(c * MOE_SUB, MOE_SUB)],
                sem_ref.at[which])))
        return out

    def start_group(g, which):
        for present, copy in copies(g, which):
            pl.when(present)(copy.start)

    def wait_group(g, which):
        for present, copy in copies(g, which):
            pl.when(present)(copy.wait)

    @pl.when(i == 0)
    def _():
        buf_ref[...] = jnp.zeros_like(buf_ref)

    @pl.when(n > 0)
    def _():
        start_group(0, 0)

    acc_ref[...] = jnp.zeros_like(acc_ref)
    d0 = d0_ref[...]
    d1 = d1_ref[...]
    lane = lax.broadcasted_iota(jnp.int32, (BM, MOE_SUB), 1)

    def body(g, carry):
        which = g % 2
        wait_group(g, which)

        @pl.when(g + 1 < n)
        def _():
            start_group(g + 1, 1 - which)

        pieces = []
        for c in range(COMBINE_GROUP):
            slot = chunk_id(g, c) * MOE_SUB + lane
            pieces.append(jnp.where((d0 == slot) | (d1 == slot), 1.0, 0.0).astype(BF16))
        acc_ref[...] += _dot(jnp.concatenate(pieces, axis=1), buf_ref[which])
        return carry

    lax.fori_loop(0, n, body, 0)
    y = _layer_norm(DEEPNORM_ALPHA * x_ref[...] + acc_ref[...], g_ref[...], b_ref[...])
    of_ref[...] = y
    ob_ref[...] = y.astype(BF16)


def _combine_ln(x, ys, n_groups, chunk_ids, d0c, d1c, lnp, layer):
    row = lambda i, ng, ck: (i, 0)

    def ln_spec(kind):
        r = (2 + kind) * DEPTH + layer
        return pl.BlockSpec((None, 1, D_MODEL), lambda i, ng, ck: (r, 0, 0))

    grid_spec = pltpu.PrefetchScalarGridSpec(
        num_scalar_prefetch=2,
        grid=(N_TILES,),
        in_specs=[pl.BlockSpec((BM, 1), row),
                  pl.BlockSpec((BM, 1), row),
                  pl.BlockSpec((BM, D_MODEL), row),
                  pl.BlockSpec(memory_space=pl.ANY),
                  ln_spec(0), ln_spec(1)],
        out_specs=[pl.BlockSpec((BM, D_MODEL), row), pl.BlockSpec((BM, D_MODEL), row)],
        scratch_shapes=[pltpu.VMEM((2, COMBINE_GROUP * MOE_SUB, D_MODEL), BF16),
                        pltpu.SemaphoreType.DMA((2,)),
                        pltpu.VMEM((BM, D_MODEL), F32)],
    )
    return pl.pallas_call(
        _combine_ln_kernel,
        grid_spec=grid_spec,
        out_shape=[jax.ShapeDtypeStruct((N_TOK, D_MODEL), F32), jax.ShapeDtypeStruct((N_TOK, D_MODEL), BF16)],
        compiler_params=_params(("arbitrary",), 56),
        name="moe_combine_ln",
    )(n_groups, chunk_ids, d0c, d1c, x, ys, lnp, lnp)


def _token_rows(a, fill):
    pad = TOK_CHUNKS * TOK_CHUNK - N_TOK
    return jnp.pad(a, (0, pad), constant_values=fill).reshape(TOK_CHUNKS, 1, TOK_CHUNK)


def _moe_layer(x, xb, w_router_padded, w_gu, w_down, j, lnp, layer):
    sel, gates = _router(x, w_router_padded, j)
    sel = sel[:, :N_EXPERTS]
    gates = gates[:, :N_EXPERTS]

    cum = jnp.cumsum(sel, axis=0)
    rank = cum - sel
    counts = cum[-1]
    padded = (counts + MOE_PAD - 1) // MOE_PAD * MOE_PAD
    pad_end = jnp.cumsum(padded)
    first = pad_end - padded + (padded - counts) // MOE_SUB * MOE_SUB
    slot = first[None, :] + rank
    chosen = sel > 0
    d0 = jnp.min(jnp.where(chosen, slot, MOE_SLOTS), axis=1).astype(jnp.int32)
    d1 = jnp.max(jnp.where(chosen, slot, -1), axis=1).astype(jnp.int32)
    g0 = jnp.sum(jnp.where(chosen & (slot == d0[:, None]), gates, 0.0), axis=1)
    g1 = jnp.sum(jnp.where(chosen & (slot == d1[:, None]), gates, 0.0), axis=1)

    def tiles(rows_per_tile):
        start = jnp.arange(MOE_SLOTS // rows_per_tile, dtype=jnp.int32) * rows_per_tile
        e = jnp.minimum(jnp.sum(pad_end[None, :] <= start[:, None], axis=1), N_EXPERTS - 1).astype(jnp.int32)
        r0 = jnp.maximum(start - first[e], 0)
        valid = jnp.clip(jnp.minimum(first[e] + counts[e], start + rows_per_tile) - jnp.maximum(first[e], start),
                         0, rows_per_tile)
        return e, r0, valid

    def matmul_tiles(rows_per_tile):
        e, _, valid = tiles(rows_per_tile)
        live_subs = ((valid + MOE_SUB - 1) // MOE_SUB).astype(jnp.int32)
        opens = jnp.concatenate([jnp.ones((1,), bool), (e[1:] != e[:-1]) | (live_subs[:-1] == 0)])
        fresh = ((live_subs > 0) & opens).astype(jnp.int32)
        n_live = (pad_end[-1:] // rows_per_tile).astype(jnp.int32)
        return e, live_subs, fresh, n_live

    sub_e, r0, valid = tiles(MOE_SUB)
    cum_rows = cum.T[sub_e]
    tok_lo = jnp.sum(cum_rows < (r0 + 1)[:, None], axis=1)
    tok_hi = jnp.sum(cum_rows < (r0 + valid)[:, None], axis=1)
    chunk_lo = jnp.where(valid > 0, tok_lo // TOK_CHUNK, 1).astype(jnp.int32)
    chunk_hi = jnp.where(valid > 0, tok_hi // TOK_CHUNK, 0).astype(jnp.int32)

    rank_edges = jnp.concatenate([jnp.zeros((1, N_EXPERTS), cum.dtype), cum[BM - 1::BM]], axis=0)
    lo = first[None, :] + rank_edges[:-1]
    hi = first[None, :] + rank_edges[1:]
    per_expert = COMBINE_KMAX // N_EXPERTS
    cand = (lo // MOE_SUB)[:, :, None] + jnp.arange(per_expert)[None, None, :]
    ok = (hi > lo)[:, :, None] & (cand <= ((hi - 1) // MOE_SUB)[:, :, None])
    cand = cand.reshape(N_TILES, COMBINE_KMAX)
    ok = ok.reshape(N_TILES, COMBINE_KMAX)
    order = jnp.argsort(jnp.logical_not(ok), axis=1, stable=True)
    chunk_ids = jnp.where(jnp.take_along_axis(ok, order, axis=1), jnp.take_along_axis(cand, order, axis=1), -1)
    chunk_ids = chunk_ids.astype(jnp.int32).reshape(-1)
    n_groups = ((jnp.sum(ok, axis=1) + COMBINE_GROUP - 1) // COMBINE_GROUP).astype(jnp.int32)

    xs, gs = _dispatch(xb, chunk_lo, chunk_hi, _token_rows(d0, -1), _token_rows(d1, -1),
                       _token_rows(g0, 0.0), _token_rows(g1, 0.0))
    h = _moe_gu(xs, w_gu, j, *matmul_tiles(MOE_GU_BM))
    ys = _moe_down(h, w_down, j, gs, *matmul_tiles(MOE_DOWN_BM))
    return _combine_ln(x, ys, n_groups, chunk_ids, d0[:, None], d1[:, None], lnp, layer)


def _block_diag_causal(w_s):
    small = jnp.tril(w_s[:, :DEC_SEQ, :DEC_SEQ])
    eye = jnp.eye(DEC_BATCH, dtype=w_s.dtype)
    return jnp.einsum("ab,gij->gaibj", eye, small).reshape(SGU_GROUPS, CHUNK, CHUNK)


def kernel(x_prompt, x_sample, state_swa_k, state_swa_v, state_conv, ln1_g, ln1_b, ln2_g, ln2_b, a_w_in, a_ln_g, a_ln_b, a_w_s, a_b_s, a_w_out, b_w_qkv, b_sinks, b_w_o, c_w_in, c_w_conv, c_w_out, f_w_gu, f_w_down, m_w_router, m_w_gu, m_w_down):
    x = jnp.concatenate([x_prompt.reshape(N_PROMPT, D_MODEL), x_sample.reshape(N_SAMPLE, D_MODEL)], axis=0)
    xb = x.astype(BF16)
    lnp = jnp.concatenate([ln1_g, ln1_b, ln2_g, ln2_b], axis=0).reshape(4 * DEPTH, 1, D_MODEL)
    a_lnp = jnp.concatenate([a_ln_g, a_ln_b], axis=0).reshape(-1, 1, D_MODEL)
    a_w_in_b, a_w_out_b = a_w_in.astype(BF16), a_w_out.astype(BF16)
    b_w_qkv_b, b_w_o_b = b_w_qkv.astype(BF16), b_w_o.astype(BF16)
    c_w_in_b, c_w_out_b = c_w_in.astype(BF16), c_w_out.astype(BF16)
    f_w_gu_b, f_w_down_b = f_w_gu.astype(BF16), f_w_down.astype(BF16)
    w_router_padded = jnp.pad(m_w_router, ((0, 0), (0, 0), (0, 128 - N_EXPERTS)))

    sgu_v, swa_states, conv_states = [], [], []
    for i in range(DEPTH):
        kind, j = i % N_MIXERS, i // N_MIXERS
        if kind == 0:
            z2 = _sgu_in(xb, a_w_in_b, a_lnp, j)
            sgu_v.append(z2[N_PROMPT:, D_MODEL:].reshape(DEC_BATCH, DEC_SEQ, D_MODEL))
            b_t = a_b_s[j].T
            b_t_sample = jnp.tile(b_t[:DEC_SEQ], (DEC_BATCH, 1))
            x, xb = _sgu_out(z2, a_w_s, _block_diag_causal(a_w_s[j]), b_t, b_t_sample, a_w_out_b, j, x, lnp, i)
        elif kind == 1:
            x, xb, st = _swa_mixer(xb, x, state_swa_k[j], state_swa_v[j], b_w_qkv_b, b_sinks[j], b_w_o_b, j, lnp, i)
            swa_states.append(st)
        else:
            x, xb, st = _conv_mixer(xb, x, state_conv[j], c_w_in_b, c_w_conv, c_w_out_b, j, lnp, i)
            conv_states.append(st)
        if i % 2 == 0:
            h = _dense_gu(xb, f_w_gu_b, i // 2)
            x, xb = _mm_ln(h, f_w_down_b, i // 2, x, lnp, 1, i, 320, "dense_down")
        else:
            x, xb = _moe_layer(x, xb, w_router_padded, m_w_gu, m_w_down, i // 2, lnp, i)

    y_prompt = x[:N_PROMPT].reshape(BATCH, SEQ, D_MODEL)
    y_sample = x[N_PROMPT:].reshape(DEC_BATCH, DEC_SEQ, D_MODEL)
    return (y_prompt, y_sample, jnp.stack(sgu_v),
            jnp.stack([s[0] for s in swa_states]), jnp.stack([s[1] for s in swa_states]),
            jnp.stack([s[2] for s in swa_states]), jnp.stack([s[3] for s in swa_states]),
            jnp.stack([s[0] for s in conv_states]), jnp.stack([s[1] for s in conv_states]))
```

```python
import functools

import numpy as np
import jax
import jax.numpy as jnp
from jax import lax
from jax.experimental import pallas as pl
from jax.experimental.pallas import tpu as pltpu

D_MODEL = 2048
BATCH = 2
SEQ = 4096
DEPTH = 4
DEC_BATCH = 32
DEC_SEQ = 4
N_MIXERS = 3
CHUNK = 128
SGU_GROUPS = 8
SGU_GROUP_DIM = D_MODEL // SGU_GROUPS
HEAD_DIM = 64
N_HEADS = D_MODEL // HEAD_DIM
N_KV_HEADS = N_HEADS // 8
GQA_GROUP = N_HEADS // N_KV_HEADS
WINDOW = 128
CONV_WIDTH = 3
D_FF = 5632
N_EXPERTS = 8
LN_EPS = 1e-5
DEEPNORM_ALPHA = (2.0 * DEPTH) ** 0.25
NEG_INF = -1e30

N_PROMPT = BATCH * SEQ
N_SAMPLE = DEC_BATCH * DEC_SEQ
N_TOK = N_PROMPT + N_SAMPLE
BM = 640
N_TILES = N_TOK // BM
SGU_BM = 128
SGU_TILES = N_TOK // SGU_BM

MOE_SUB = 256
MOE_PAD = 4 * MOE_SUB
MOE_SLOTS = (-(-(2 * N_TOK) // MOE_PAD) + N_EXPERTS) * MOE_PAD
MOE_GU_BM = MOE_PAD
MOE_DOWN_BM = 2 * MOE_SUB
MOE_BN = 512
COMBINE_GROUP = 4
TOK_CHUNK = 512
TOK_CHUNKS_FULL = N_TOK // TOK_CHUNK
TOK_TAIL = N_TOK - TOK_CHUNKS_FULL * TOK_CHUNK
TOK_CHUNKS = TOK_CHUNKS_FULL + 1
COMBINE_KMAX = N_EXPERTS * ((MOE_SUB - 1 + BM - 1) // MOE_SUB + 1)

F32 = jnp.float32
BF16 = jnp.bfloat16
MIB = 1024 * 1024


def _params(semantics, vmem_mib):
    return pltpu.CompilerParams(dimension_semantics=semantics, vmem_limit_bytes=vmem_mib * MIB)


def _layer_norm(y, g, b):
    mu = jnp.mean(y, axis=-1, keepdims=True)
    d = y - mu
    var = jnp.mean(d * d, axis=-1, keepdims=True)
    return d * lax.rsqrt(var + LN_EPS) * g + b


def _gelu(x):
    return 0.5 * x * (1.0 + lax.erf(x * np.float32(np.sqrt(0.5))))


def _dot(a, b):
    return jnp.dot(a, b, preferred_element_type=F32)


def _ln_specs(norm, layer, ngrid):
    def spec(kind):
        r = (2 * norm + kind) * DEPTH + layer
        if ngrid == 1:
            return pl.BlockSpec((None, 1, D_MODEL), lambda i: (r, 0, 0))
        return pl.BlockSpec((None, 1, D_MODEL), lambda j, i: (r, 0, 0))
    return [spec(0), spec(1)]


def _mm_kernel(x_ref, w_ref, o_ref):
    o_ref[...] = _dot(x_ref[...], w_ref[...]).astype(o_ref.dtype)


def _mm(x, w, layer, bn, out_dtype, name):
    m, k = x.shape
    n = w.shape[-1]
    return pl.pallas_call(
        _mm_kernel,
        grid=(n // bn, m // BM),
        in_specs=[pl.BlockSpec((BM, k), lambda j, i: (i, 0)),
                  pl.BlockSpec((None, k, bn), lambda j, i: (layer, 0, j))],
        out_specs=pl.BlockSpec((BM, bn), lambda j, i: (i, j)),
        out_shape=jax.ShapeDtypeStruct((m, n), out_dtype),
        compiler_params=_params(("arbitrary", "arbitrary"), 48),
        name=name,
    )(x, w)


def _mm_ln_kernel(x_ref, w_ref, r_ref, g_ref, b_ref, of_ref, ob_ref):
    bm = x_ref.shape[0]
    group = bm // 4 if x_ref.shape[1] <= D_MODEL else bm // 2
    for start in range(0, bm, group):
        rows = slice(start, start + group)
        f = _dot(x_ref[rows, :], w_ref[...])
        y = _layer_norm(DEEPNORM_ALPHA * r_ref[rows, :] + f, g_ref[...], b_ref[...])
        of_ref[rows, :] = y
        ob_ref[rows, :] = y.astype(BF16)


def _mm_ln(x, w, w_layer, resid, lnp, norm, layer, bm, name):
    m, k = x.shape
    row = lambda i: (i, 0)
    return pl.pallas_call(
        _mm_ln_kernel,
        grid=(m // bm,),
        in_specs=[pl.BlockSpec((bm, k), row),
                  pl.BlockSpec((None, k, D_MODEL), lambda i: (w_layer, 0, 0), pipeline_mode=pl.Buffered(1)),
                  pl.BlockSpec((bm, D_MODEL), row)] + _ln_specs(norm, layer, 1),
        out_specs=[pl.BlockSpec((bm, D_MODEL), row), pl.BlockSpec((bm, D_MODEL), row)],
        out_shape=[jax.ShapeDtypeStruct((m, D_MODEL), F32), jax.ShapeDtypeStruct((m, D_MODEL), BF16)],
        compiler_params=_params(("arbitrary",), 56),
        name=name,
    )(x, w, resid, lnp, lnp)


def _sgu_in_kernel(x_ref, w_ref, g_ref, b_ref, o_ref):
    half = BM // 2

    @pl.when(pl.program_id(0) == 0)
    def _():
        for rows in (slice(0, half), slice(half, BM)):
            o_ref[rows, :] = _gelu(_dot(x_ref[rows, :], w_ref[...]))

    @pl.when(pl.program_id(0) == 1)
    def _():
        for rows in (slice(0, half), slice(half, BM)):
            o_ref[rows, :] = _layer_norm(_gelu(_dot(x_ref[rows, :], w_ref[...])), g_ref[...], b_ref[...])


def _sgu_in(xb, w_in, a_lnp, j):
    return pl.pallas_call(
        _sgu_in_kernel,
        grid=(2, N_TILES),
        in_specs=[pl.BlockSpec((BM, D_MODEL), lambda c, i: (i, 0)),
                  pl.BlockSpec((None, D_MODEL, D_MODEL), lambda c, i: (j, 0, c)),
                  pl.BlockSpec((None, 1, D_MODEL), lambda c, i: (j, 0, 0)),
                  pl.BlockSpec((None, 1, D_MODEL), lambda c, i: (2 + j, 0, 0))],
        out_specs=pl.BlockSpec((BM, D_MODEL), lambda c, i: (i, c)),
        out_shape=jax.ShapeDtypeStruct((N_TOK, 2 * D_MODEL), F32),
        compiler_params=_params(("arbitrary", "arbitrary"), 56),
        name="sgu_in",
    )(xb, w_in, a_lnp, a_lnp)


def _sgu_out_kernel(u_ref, v_ref, wp_ref, ws_ref, bp_ref, bs_ref, wo_ref, r_ref, g_ref, b_ref, of_ref, ob_ref):
    is_sample_tile = pl.program_id(0) == SGU_TILES - 1
    row = lax.broadcasted_iota(jnp.int32, (CHUNK, CHUNK), 0)
    col = lax.broadcasted_iota(jnp.int32, (CHUNK, CHUNK), 1)
    causal = row >= col
    groups = range(SGU_GROUPS)
    cols = [slice(grp * SGU_GROUP_DIM, (grp + 1) * SGU_GROUP_DIM) for grp in groups]
    w = [jnp.where(is_sample_tile, ws_ref[grp], jnp.where(causal, wp_ref[grp], 0.0)).astype(BF16) for grp in groups]
    bias = [jnp.where(is_sample_tile, bs_ref[:, grp:grp + 1], bp_ref[:, grp:grp + 1]) for grp in groups]
    mixed = [_dot(w[grp], v_ref[:, cols[grp]].astype(BF16)) + bias[grp] for grp in groups]
    gated = jnp.concatenate([(u_ref[:, cols[grp]] * mixed[grp]).astype(BF16) for grp in groups], axis=1)
    f = _dot(gated, wo_ref[...])
    y = _layer_norm(DEEPNORM_ALPHA * r_ref[...] + f, g_ref[...], b_ref[...])
    of_ref[...] = y
    ob_ref[...] = y.astype(BF16)


def _sgu_out(z2, w_s, w_s_sample, b_t, b_t_sample, w_out, j, resid, lnp, layer):
    row = lambda i: (i, 0)
    const2 = lambda i: (0, 0)
    const3 = lambda i: (0, 0, 0)
    return pl.pallas_call(
        _sgu_out_kernel,
        grid=(SGU_TILES,),
        in_specs=[pl.BlockSpec((SGU_BM, D_MODEL), lambda i: (i, 0)),
                  pl.BlockSpec((SGU_BM, D_MODEL), lambda i: (i, 1)),
                  pl.BlockSpec((None, SGU_GROUPS, CHUNK, CHUNK), lambda i: (j, 0, 0, 0)),
                  pl.BlockSpec((SGU_GROUPS, CHUNK, CHUNK), const3),
                  pl.BlockSpec((CHUNK, SGU_GROUPS), const2),
                  pl.BlockSpec((CHUNK, SGU_GROUPS), const2),
                  pl.BlockSpec((None, D_MODEL, D_MODEL), lambda i: (j, 0, 0), pipeline_mode=pl.Buffered(1)),
                  pl.BlockSpec((SGU_BM, D_MODEL), row)] + _ln_specs(0, layer, 1),
        out_specs=[pl.BlockSpec((SGU_BM, D_MODEL), row), pl.BlockSpec((SGU_BM, D_MODEL), row)],
        out_shape=[jax.ShapeDtypeStruct((N_TOK, D_MODEL), F32), jax.ShapeDtypeStruct((N_TOK, D_MODEL), BF16)],
        compiler_params=_params(("arbitrary",), 40),
        name="sgu_out",
    )(z2, z2, w_s, w_s_sample, b_t, b_t_sample, w_out, resid, lnp, lnp)


def _gu_kernel(x_ref, wg_ref, wu_ref, o_ref):
    half = BM // 2
    for rows in (slice(0, half), slice(half, BM)):
        x = x_ref[rows, :]
        gate = _dot(x, wg_ref[...])
        up = _dot(x, wu_ref[...])
        o_ref[rows, :] = (jax.nn.silu(gate) * up).astype(o_ref.dtype)


def _dense_gu(xb, w_gu, layer):
    bn = 1408
    nt = D_FF // bn
    return pl.pallas_call(
        _gu_kernel,
        grid=(nt, N_TILES),
        in_specs=[pl.BlockSpec((BM, D_MODEL), lambda j, i: (i, 0)),
                  pl.BlockSpec((None, D_MODEL, bn), lambda j, i: (layer, 0, j)),
                  pl.BlockSpec((None, D_MODEL, bn), lambda j, i: (layer, 0, j + nt))],
        out_specs=pl.BlockSpec((BM, bn), lambda j, i: (i, j)),
        out_shape=jax.ShapeDtypeStruct((N_TOK, D_FF), BF16),
        compiler_params=_params(("arbitrary", "arbitrary"), 56),
        name="dense_gu",
    )(xb, w_gu, w_gu)


def _attn_kernel(sink_ref, q_ref, kp_ref, vp_ref, kc_ref, vc_ref, o_ref, *, blocks_per_seq):
    has_prev = (pl.program_id(0) % blocks_per_seq) != 0
    qi = lax.broadcasted_iota(jnp.int32, (WINDOW, 2 * WINDOW), 0)
    kj = lax.broadcasted_iota(jnp.int32, (WINDOW, 2 * WINDOW), 1)
    delta_i = WINDOW + qi - kj
    valid = (delta_i >= 0) & (delta_i <= WINDOW) & (has_prev | (kj >= WINDOW))
    delta = delta_i.astype(F32)
    low_half = lax.broadcasted_iota(jnp.int32, (2 * WINDOW, 128), 1) < HEAD_DIM

    kk = jnp.concatenate([kp_ref[...], kc_ref[...]], axis=0)
    vv = jnp.concatenate([vp_ref[...], vc_ref[...]], axis=0)

    def halves(slab, kv):
        if kv % 2 == 0:
            lo = jnp.where(low_half, slab, 0.0)
            hi = jnp.where(low_half, 0.0, pltpu.roll(slab, HEAD_DIM, 1))
        else:
            lo = jnp.where(low_half, pltpu.roll(slab, HEAD_DIM, 1), 0.0)
            hi = jnp.where(low_half, 0.0, slab)
        return lo.astype(BF16), hi.astype(BF16)

    for kv in range(N_KV_HEADS):
        lanes = slice((kv // 2) * 128, (kv // 2 + 1) * 128)
        k_both = jnp.concatenate(halves(kk[:, lanes], kv), axis=0)
        v_both = jnp.concatenate(halves(vv[:, lanes], kv), axis=0)
        pairs = range(GQA_GROUP // 2)
        heads = range(GQA_GROUP)
        qcols = [slice((kv * 4 + pair) * 128, (kv * 4 + pair + 1) * 128) for pair in pairs]
        scores = [lax.dot_general(q_ref[:, qcols[pair]].astype(BF16), k_both, (((1,), (1,)), ((), ())),
                                  preferred_element_type=F32) * (HEAD_DIM ** -0.5) for pair in pairs]
        slopes = [float(2.0 ** (-8.0 * (kv * GQA_GROUP + g + 1) / N_HEADS)) for g in heads]
        sinks = [sink_ref[kv * GQA_GROUP + g] for g in heads]
        s = [scores[g // 2][:, (g % 2) * 2 * WINDOW:(g % 2 + 1) * 2 * WINDOW] - slopes[g] * delta for g in heads]
        s = [jnp.where(valid, s[g], NEG_INF) for g in heads]
        mx = [jnp.maximum(jnp.max(s[g], axis=-1, keepdims=True), sinks[g]) for g in heads]
        p = [jnp.exp(s[g] - mx[g]) for g in heads]
        den = [jnp.sum(p[g], axis=-1, keepdims=True) + jnp.exp(sinks[g] - mx[g]) for g in heads]
        p = [(p[g] / den[g]).astype(BF16) for g in heads]
        for pair in pairs:
            both = jnp.concatenate([p[2 * pair], p[2 * pair + 1]], axis=1)
            o_ref[:, qcols[pair]] = _dot(both, v_both).astype(o_ref.dtype)


def _attention(sinks, q, kp, vp, kc, vc, n_blocks, specs, blocks_per_seq, name):
    return pl.pallas_call(
        functools.partial(_attn_kernel, blocks_per_seq=blocks_per_seq),
        grid=(n_blocks,),
        in_specs=[pl.BlockSpec(memory_space=pltpu.SMEM)] + specs,
        out_specs=pl.BlockSpec((WINDOW, D_MODEL), lambda i: (i, 0)),
        out_shape=jax.ShapeDtypeStruct((n_blocks * WINDOW, D_MODEL), BF16),
        compiler_params=_params(("arbitrary",), 32),
        name=name,
    )(sinks, q, kp, vp, kc, vc)


SAMPLE_ROWS = GQA_GROUP * DEC_SEQ
SAMPLE_KEYS = 2 * WINDOW


def _attn_sample_kernel(q_ref, k_ref, v_ref, sink_ref, slope_ref, o_ref):
    step = lax.broadcasted_iota(jnp.int32, (SAMPLE_ROWS, SAMPLE_KEYS), 0) % DEC_SEQ
    kj = lax.broadcasted_iota(jnp.int32, (SAMPLE_ROWS, SAMPLE_KEYS), 1)
    delta_i = WINDOW + step - kj
    valid = (delta_i >= 0) & (delta_i <= WINDOW)
    delta = delta_i.astype(F32)
    kvs = range(N_KV_HEADS)
    lanes = [slice((kv // 2) * 128, (kv // 2 + 1) * 128) for kv in kvs]
    s = [lax.dot_general(q_ref[kv].astype(BF16), k_ref[:, lanes[kv]].astype(BF16), (((1,), (1,)), ((), ())),
                         preferred_element_type=F32) * (HEAD_DIM ** -0.5) for kv in kvs]
    s = [jnp.where(valid, s[kv] - slope_ref[kv] * delta, NEG_INF) for kv in kvs]
    mx = [jnp.maximum(jnp.max(s[kv], axis=-1, keepdims=True), sink_ref[kv]) for kv in kvs]
    p = [jnp.exp(s[kv] - mx[kv]) for kv in kvs]
    den = [jnp.sum(p[kv], axis=-1, keepdims=True) + jnp.exp(sink_ref[kv] - mx[kv]) for kv in kvs]
    p = [(p[kv] / den[kv]).astype(BF16) for kv in kvs]
    for kv in kvs:
        o_ref[kv] = _dot(p[kv], v_ref[:, lanes[kv]].astype(BF16)).astype(o_ref.dtype)


def _attn_sample(q_s, k_new, v_new, state_k, state_v, sinks):
    kvw = N_KV_HEADS * HEAD_DIM
    q = q_s.reshape(DEC_BATCH, DEC_SEQ, N_KV_HEADS, GQA_GROUP, HEAD_DIM).transpose(0, 2, 3, 1, 4)
    q = q.reshape(DEC_BATCH, N_KV_HEADS, SAMPLE_ROWS, HEAD_DIM)
    zero = jnp.zeros_like(q)
    odd = (jnp.arange(N_KV_HEADS) % 2 == 1)[None, :, None, None]
    q = jnp.where(odd, jnp.concatenate([zero, q], axis=-1), jnp.concatenate([q, zero], axis=-1))

    def keys(state, new):
        pad = jnp.zeros((DEC_BATCH, SAMPLE_KEYS - WINDOW - DEC_SEQ, kvw), F32)
        return jnp.concatenate([state.reshape(DEC_BATCH, WINDOW, kvw), new, pad], axis=1)

    head = np.arange(N_KV_HEADS)[:, None] * GQA_GROUP + np.arange(SAMPLE_ROWS)[None, :] // DEC_SEQ
    slopes = jnp.asarray(np.exp2(-8.0 * (head + 1) / N_HEADS), F32)[:, :, None]
    sink_rows = sinks[head][:, :, None]
    seq3 = lambda i: (i, 0, 0)
    whole = lambda i: (0, 0, 0)
    out = pl.pallas_call(
        _attn_sample_kernel,
        grid=(DEC_BATCH,),
        in_specs=[pl.BlockSpec((None, N_KV_HEADS, SAMPLE_ROWS, 128), lambda i: (i, 0, 0, 0)),
                  pl.BlockSpec((None, SAMPLE_KEYS, kvw), seq3),
                  pl.BlockSpec((None, SAMPLE_KEYS, kvw), seq3),
                  pl.BlockSpec((N_KV_HEADS, SAMPLE_ROWS, 1), whole),
                  pl.BlockSpec((N_KV_HEADS, SAMPLE_ROWS, 1), whole)],
        out_specs=pl.BlockSpec((None, N_KV_HEADS, SAMPLE_ROWS, 128), lambda i: (i, 0, 0, 0)),
        out_shape=jax.ShapeDtypeStruct((DEC_BATCH, N_KV_HEADS, SAMPLE_ROWS, 128), BF16),
        compiler_params=_params(("arbitrary",), 16),
        name="attn_sample",
    )(q, keys(state_k, k_new), keys(state_v, v_new), sink_rows, slopes)
    out = jnp.where(odd, out[..., HEAD_DIM:], out[..., :HEAD_DIM])
    out = out.reshape(DEC_BATCH, N_KV_HEADS, GQA_GROUP, DEC_SEQ, HEAD_DIM).transpose(0, 3, 1, 2, 4)
    return out.reshape(N_SAMPLE, D_MODEL)


def _swa_mixer(xb, x, state_k, state_v, w_qkv, sinks, w_o, j, lnp, layer):
    kvw = N_KV_HEADS * HEAD_DIM
    qkv = _mm(xb, w_qkv, j, 1280, F32, "qkv")
    kcol, vcol = D_MODEL // kvw, D_MODEL // kvw + 1
    nb = SEQ // WINDOW

    def prev_block(i):
        return jnp.maximum(i - 1, 0)

    prompt_specs = [pl.BlockSpec((WINDOW, D_MODEL), lambda i: (i, 0)),
                    pl.BlockSpec((WINDOW, kvw), lambda i: (prev_block(i), kcol)),
                    pl.BlockSpec((WINDOW, kvw), lambda i: (prev_block(i), vcol)),
                    pl.BlockSpec((WINDOW, kvw), lambda i: (i, kcol)),
                    pl.BlockSpec((WINDOW, kvw), lambda i: (i, vcol))]
    o_prompt = _attention(sinks, qkv, qkv, qkv, qkv, qkv, N_PROMPT // WINDOW, prompt_specs, nb, "attn_prompt")

    qkv_s = qkv[N_PROMPT:].reshape(DEC_BATCH, DEC_SEQ, D_MODEL + 2 * kvw)
    k_new = qkv_s[..., D_MODEL:D_MODEL + kvw]
    v_new = qkv_s[..., D_MODEL + kvw:]
    o_sample = _attn_sample(qkv_s[..., :D_MODEL], k_new, v_new, state_k, state_v, sinks)
    o_all = jnp.concatenate([o_prompt, o_sample], axis=0)
    x, xb = _mm_ln(o_all, w_o, j, x, lnp, 0, layer, BM, "attn_out")

    k_p = qkv[:N_PROMPT, D_MODEL:D_MODEL + kvw].reshape(BATCH, SEQ, N_KV_HEADS, HEAD_DIM)[:, -WINDOW:]
    v_p = qkv[:N_PROMPT, D_MODEL + kvw:].reshape(BATCH, SEQ, N_KV_HEADS, HEAD_DIM)[:, -WINDOW:]
    k_s = jnp.concatenate([state_k, k_new.reshape(DEC_BATCH, DEC_SEQ, N_KV_HEADS, HEAD_DIM)], axis=1)[:, -WINDOW:]
    v_s = jnp.concatenate([state_v, v_new.reshape(DEC_BATCH, DEC_SEQ, N_KV_HEADS, HEAD_DIM)], axis=1)[:, -WINDOW:]
    return x, xb, (k_p, v_p, k_s, v_s)


CONV_BC = 512
SAMPLE_ROW0 = N_PROMPT - (N_TILES - 1) * BM


def _conv_taps(z, z1, z2, gb, taps_ref):
    conv = taps_ref[0:1, :] * z2 + taps_ref[1:2, :] * z1 + taps_ref[2:3, :] * z
    return (gb * conv).astype(BF16)


def _conv_kernel(x_ref, wb_ref, wc_ref, wh_ref, taps_ref, pa_ref, pb_ref, y_ref, z_ref, carry_ref):
    i = pl.program_id(1)

    @pl.when(i == 0)
    def _():
        carry_ref[...] = jnp.zeros_like(carry_ref)

    x = x_ref[...]
    gate_b = _dot(x, wb_ref[...])
    z = _dot(x, wc_ref[...]) * _dot(x, wh_ref[...])
    local = lax.broadcasted_iota(jnp.int32, z.shape, 0)
    pos = (i * BM + local) % SEQ
    c6 = carry_ref[6:7, :]
    c7 = carry_ref[7:8, :]
    back1 = jnp.where(local >= 1, pltpu.roll(z, 1, 0), c7)
    back2 = jnp.where(local >= 2, pltpu.roll(z, 2, 0), jnp.where(local == 1, c7, c6))
    z1 = jnp.where(pos >= 1, back1, 0.0)
    z2 = jnp.where(pos >= 2, back2, 0.0)
    y_ref[...] = _conv_taps(z, z1, z2, gate_b, taps_ref)
    z_ref[...] = z
    carry_ref[...] = z[BM - 8:, :]

    @pl.when(i == N_TILES - 1)
    def _():
        zs = z[SAMPLE_ROW0:, :]
        t = lax.broadcasted_iota(jnp.int32, zs.shape, 0) % DEC_SEQ
        z1s = jnp.where(t >= 1, pltpu.roll(zs, 1, 0), pa_ref[...])
        z2s = jnp.where(t >= 2, pltpu.roll(zs, 2, 0), pb_ref[...])
        y_ref[SAMPLE_ROW0:, :] = _conv_taps(zs, z1s, z2s, gate_b[SAMPLE_ROW0:, :], taps_ref)


def _conv_mixer(xb, x, state, w_in, w_conv, w_out, j, lnp, layer):
    zeros = jnp.zeros((DEC_BATCH, DEC_SEQ, D_MODEL), F32)
    past_a = zeros.at[:, 0].set(state[:, 1]).reshape(N_SAMPLE, D_MODEL)
    past_b = zeros.at[:, 0].set(state[:, 0]).at[:, 1].set(state[:, 1]).reshape(N_SAMPLE, D_MODEL)
    ncb = D_MODEL // CONV_BC
    tile = lambda c, i: (i, c)
    y, z = pl.pallas_call(
        _conv_kernel,
        grid=(ncb, N_TILES),
        in_specs=[pl.BlockSpec((BM, D_MODEL), lambda c, i: (i, 0)),
                  pl.BlockSpec((None, D_MODEL, CONV_BC), lambda c, i: (j, 0, c)),
                  pl.BlockSpec((None, D_MODEL, CONV_BC), lambda c, i: (j, 0, ncb + c)),
                  pl.BlockSpec((None, D_MODEL, CONV_BC), lambda c, i: (j, 0, 2 * ncb + c)),
                  pl.BlockSpec((None, CONV_WIDTH, CONV_BC), lambda c, i: (j, 0, c)),
                  pl.BlockSpec((N_SAMPLE, CONV_BC), lambda c, i: (0, c)),
                  pl.BlockSpec((N_SAMPLE, CONV_BC), lambda c, i: (0, c))],
        out_specs=[pl.BlockSpec((BM, CONV_BC), tile), pl.BlockSpec((BM, CONV_BC), tile)],
        out_shape=[jax.ShapeDtypeStruct((N_TOK, D_MODEL), BF16), jax.ShapeDtypeStruct((N_TOK, D_MODEL), F32)],
        scratch_shapes=[pltpu.VMEM((8, CONV_BC), F32)],
        compiler_params=_params(("arbitrary", "arbitrary"), 40),
        name="conv_in",
    )(xb, w_in, w_in, w_in, w_conv, past_a, past_b)
    x, xb = _mm_ln(y, w_out, j, x, lnp, 0, layer, BM, "conv_out")
    conv_p = z[:N_PROMPT].reshape(BATCH, SEQ, D_MODEL)[:, -(CONV_WIDTH - 1):]
    conv_s = z[N_PROMPT:].reshape(DEC_BATCH, DEC_SEQ, D_MODEL)[:, -(CONV_WIDTH - 1):]
    return x, xb, (conv_p, conv_s)


def _router_kernel(x_ref, w_ref, sel_ref, gate_ref):
    logits = jnp.dot(x_ref[...], w_ref[...], preferred_element_type=F32, precision=lax.Precision.HIGHEST)
    lane = lax.broadcasted_iota(jnp.int32, logits.shape, 1)
    neg = jnp.float32(-jnp.inf)
    l1 = jnp.where(lane < N_EXPERTS, logits, neg)
    m1 = jnp.max(l1, axis=-1, keepdims=True)
    i1 = jnp.min(jnp.where(l1 == m1, lane, 128), axis=-1, keepdims=True)
    l2 = jnp.where(lane == i1, neg, l1)
    m2 = jnp.max(l2, axis=-1, keepdims=True)
    i2 = jnp.min(jnp.where(l2 == m2, lane, 128), axis=-1, keepdims=True)
    e2 = jnp.exp(m2 - m1)
    den = 1.0 + e2
    g1 = 1.0 / den
    g2 = e2 / den
    sel_ref[...] = ((lane == i1) | (lane == i2)).astype(jnp.int32)
    gate_ref[...] = jnp.where(lane == i1, g1, jnp.where(lane == i2, g2, 0.0))


def _router(x, w_router_padded, layer):
    row = lambda i: (i, 0)
    return pl.pallas_call(
        _router_kernel,
        grid=(N_TILES,),
        in_specs=[pl.BlockSpec((BM, D_MODEL), row),
                  pl.BlockSpec((None, D_MODEL, 128), lambda i: (layer, 0, 0))],
        out_specs=[pl.BlockSpec((BM, 128), row), pl.BlockSpec((BM, 128), row)],
        out_shape=[jax.ShapeDtypeStruct((N_TOK, 128), jnp.int32), jax.ShapeDtypeStruct((N_TOK, 128), F32)],
        compiler_params=_params(("arbitrary",), 32),
        name="router",
    )(x, w_router_padded)


def _dispatch_kernel(lo_ref, hi_ref, d0_ref, d1_ref, g0_ref, g1_ref, x_ref, xs_ref, gs_ref, acc_ref, gacc_ref):
    i = pl.program_id(0)
    lo = lo_ref[i]
    hi = hi_ref[i]
    slot = i * MOE_SUB + lax.broadcasted_iota(jnp.int32, (MOE_SUB, TOK_CHUNK), 0)
    acc_ref[...] = jnp.zeros_like(acc_ref)
    gacc_ref[...] = jnp.zeros_like(gacc_ref)

    def chunk(c, width, x_rows):
        s = slot[:, :width]
        m0 = s == d0_ref[c][:, :width]
        m1 = s == d1_ref[c][:, :width]
        onehot = jnp.where(m0 | m1, 1.0, 0.0).astype(BF16)
        acc_ref[...] += _dot(onehot, x_rows)
        picked = jnp.where(m0, g0_ref[c][:, :width], 0.0) + jnp.where(m1, g1_ref[c][:, :width], 0.0)
        gacc_ref[...] += jnp.sum(picked, axis=1, keepdims=True)

    def body(c, carry):
        start = pl.multiple_of(c * TOK_CHUNK, TOK_CHUNK)
        chunk(c, TOK_CHUNK, x_ref[pl.ds(start, TOK_CHUNK), :])
        return carry

    lax.fori_loop(lo, jnp.minimum(hi, TOK_CHUNKS_FULL - 1) + 1, body, 0)

    @pl.when(hi == TOK_CHUNKS_FULL)
    def _():
        chunk(TOK_CHUNKS_FULL, TOK_TAIL, x_ref[TOK_CHUNKS_FULL * TOK_CHUNK:, :])

    xs_ref[...] = acc_ref[...].astype(BF16)
    gs_ref[...] = gacc_ref[...]


def _dispatch(xb, chunk_lo, chunk_hi, d0r, d1r, g0r, g1r):
    n_tiles = MOE_SLOTS // MOE_SUB
    whole3 = lambda i, lo, hi: (0, 0, 0)
    rows = pl.BlockSpec((TOK_CHUNKS, 1, TOK_CHUNK), whole3)
    grid_spec = pltpu.PrefetchScalarGridSpec(
        num_scalar_prefetch=2,
        grid=(n_tiles,),
        in_specs=[rows, rows, rows, rows,
                  pl.BlockSpec((N_TOK, D_MODEL), lambda i, lo, hi: (0, 0))],
        out_specs=[pl.BlockSpec((MOE_SUB, D_MODEL), lambda i, lo, hi: (i, 0)),
                   pl.BlockSpec((MOE_SUB, 1), lambda i, lo, hi: (i, 0))],
        scratch_shapes=[pltpu.VMEM((MOE_SUB, D_MODEL), F32), pltpu.VMEM((MOE_SUB, 1), F32)],
    )
    return pl.pallas_call(
        _dispatch_kernel,
        grid_spec=grid_spec,
        out_shape=[jax.ShapeDtypeStruct((MOE_SLOTS, D_MODEL), BF16), jax.ShapeDtypeStruct((MOE_SLOTS, 1), F32)],
        compiler_params=_params(("arbitrary",), 52),
        name="moe_dispatch",
    )(chunk_lo, chunk_hi, d0r, d1r, g0r, g1r, xb)


def _for_expert_tile(n_sub, live_subs, fresh, o_ref, compute, load_weights, kept_weights):
    total = n_sub * MOE_SUB

    def run(k, weights):
        dead = (n_sub - k) * MOE_SUB
        o_ref[dead:, :] = compute(slice(dead, total), weights())
        if dead:
            o_ref[:dead, :] = jnp.zeros((dead, o_ref.shape[1]), o_ref.dtype)

    for k in range(1, n_sub + 1):
        pl.when((fresh == 1) & (live_subs == k))(functools.partial(run, k, load_weights))
    pl.when((fresh == 0) & (live_subs == n_sub))(functools.partial(run, n_sub, kept_weights))

    @pl.when(live_subs == 0)
    def _():
        o_ref[...] = jnp.zeros_like(o_ref)


def _moe_gu_kernel(be_ref, ls_ref, fr_ref, nt_ref, x_ref, wg_ref, wu_ref, o_ref, wgb_ref, wub_ref):
    i = pl.program_id(1)

    def load_weights():
        wg = wg_ref[...].astype(BF16)
        wu = wu_ref[...].astype(BF16)
        wgb_ref[...] = wg
        wub_ref[...] = wu
        return wg, wu

    def kept_weights():
        return wgb_ref[...], wub_ref[...]

    def compute(rows, weights):
        x = x_ref[rows, :]
        gate = _dot(x, weights[0])
        up = _dot(x, weights[1])
        return (jax.nn.silu(gate) * up).astype(o_ref.dtype)

    _for_expert_tile(MOE_GU_BM // MOE_SUB, ls_ref[i], fr_ref[i], o_ref, compute, load_weights, kept_weights)


def _live_tile(i, nt):
    return jnp.minimum(i, nt[0] - 1)


def _moe_gu(xs, w_gu, layer, block_e, live_subs, fresh, n_live):
    nt = D_FF // MOE_BN
    grid_spec = pltpu.PrefetchScalarGridSpec(
        num_scalar_prefetch=4,
        grid=(nt, MOE_SLOTS // MOE_GU_BM),
        in_specs=[pl.BlockSpec((MOE_GU_BM, D_MODEL), lambda j, i, be, ls, fr, n: (_live_tile(i, n), 0)),
                  pl.BlockSpec((None, None, D_MODEL, MOE_BN),
                               lambda j, i, be, ls, fr, n: (layer, be[_live_tile(i, n)], 0, j)),
                  pl.BlockSpec((None, None, D_MODEL, MOE_BN),
                               lambda j, i, be, ls, fr, n: (layer, be[_live_tile(i, n)], 0, j + nt))],
        out_specs=pl.BlockSpec((MOE_GU_BM, MOE_BN), lambda j, i, be, ls, fr, n: (i, j)),
        scratch_shapes=[pltpu.VMEM((D_MODEL, MOE_BN), BF16), pltpu.VMEM((D_MODEL, MOE_BN), BF16)],
    )
    return pl.pallas_call(
        _moe_gu_kernel,
        grid_spec=grid_spec,
        out_shape=jax.ShapeDtypeStruct((MOE_SLOTS, D_FF), BF16),
        compiler_params=_params(("arbitrary", "arbitrary"), 48),
        name="moe_gu",
    )(block_e, live_subs, fresh, n_live, xs, w_gu, w_gu)


def _moe_down_kernel(be_ref, ls_ref, fr_ref, nt_ref, h_ref, w_ref, gs_ref, o_ref, wb_ref):
    i = pl.program_id(1)

    def load_weights():
        w = w_ref[...].astype(BF16)
        wb_ref[...] = w
        return w

    def kept_weights():
        return wb_ref[...]

    def compute(rows, w):
        return (_dot(h_ref[rows, :], w) * gs_ref[rows, :]).astype(o_ref.dtype)

    _for_expert_tile(MOE_DOWN_BM // MOE_SUB, ls_ref[i], fr_ref[i], o_ref, compute, load_weights, kept_weights)


def _moe_down(h, w_down, layer, gs, block_e, live_subs, fresh, n_live):
    grid_spec = pltpu.PrefetchScalarGridSpec(
        num_scalar_prefetch=4,
        grid=(D_MODEL // MOE_BN, MOE_SLOTS // MOE_DOWN_BM),
        in_specs=[pl.BlockSpec((MOE_DOWN_BM, D_FF), lambda j, i, be, ls, fr, n: (_live_tile(i, n), 0)),
                  pl.BlockSpec((None, None, D_FF, MOE_BN),
                               lambda j, i, be, ls, fr, n: (layer, be[_live_tile(i, n)], 0, j)),
                  pl.BlockSpec((MOE_DOWN_BM, 1), lambda j, i, be, ls, fr, n: (_live_tile(i, n), 0))],
        out_specs=pl.BlockSpec((MOE_DOWN_BM, MOE_BN), lambda j, i, be, ls, fr, n: (i, j)),
        scratch_shapes=[pltpu.VMEM((D_FF, MOE_BN), BF16)],
    )
    return pl.pallas_call(
        _moe_down_kernel,
        grid_spec=grid_spec,
        out_shape=jax.ShapeDtypeStruct((MOE_SLOTS, D_MODEL), BF16),
        compiler_params=_params(("arbitrary", "arbitrary"), 52),
        name="moe_down",
    )(block_e, live_subs, fresh, n_live, h, w_down, gs)


def _combine_ln_kernel(ng_ref, ck_ref, d0_ref, d1_ref, x_ref, ys_ref, g_ref, b_ref, of_ref, ob_ref,
                       buf_ref, sem_ref, acc_ref):
    i = pl.program_id(0)
    n = ng_ref[i]

    def chunk_id(g, c):
        return ck_ref[i * COMBINE_KMAX + g * COMBINE_GROUP + c]

    def copies(g, which):
        out = []
        for c in range(COMBINE_GROUP):
            cid = chunk_id(g, c)
            start = pl.multiple_of(jnp.maximum(cid, 0) * MOE_SUB, MOE_SUB)
            out.append((cid >= 0, pltpu.make_async_copy(
                ys_ref.at[pl.ds(start, MOE_SUB)],
                buf_ref.at[which, pl.ds(c * MOE_SUB, MOE_SUB)],
                sem_ref.at[which])))
        return out

    def start_group(g, which):
        for present, copy in copies(g, which):
            pl.when(present)(copy.start)

    def wait_group(g, which):
        for present, copy in copies(g, which):
            pl.when(present)(copy.wait)

    @pl.when(i == 0)
    def _():
        buf_ref[...] = jnp.zeros_like(buf_ref)

    @pl.when(n > 0)
    def _():
        start_group(0, 0)

    acc_ref[...] = jnp.zeros_like(acc_ref)
    d0 = d0_ref[...]
    d1 = d1_ref[...]
    lane = lax.broadcasted_iota(jnp.int32, (BM, MOE_SUB), 1)

    def body(g, carry):
        which = g % 2
        wait_group(g, which)

        @pl.when(g + 1 < n)
        def _():
            start_group(g + 1, 1 - which)

        pieces = []
        for c in range(COMBINE_GROUP):
            slot = chunk_id(g, c) * MOE_SUB + lane
            pieces.append(jnp.where((d0 == slot) | (d1 == slot), 1.0, 0.0).astype(BF16))
        acc_ref[...] += _dot(jnp.concatenate(pieces, axis=1), buf_ref[which])
        return carry

    lax.fori_loop(0, n, body, 0)
    y = _layer_norm(DEEPNORM_ALPHA * x_ref[...] + acc_ref[...], g_ref[...], b_ref[...])
    of_ref[...] = y
    ob_ref[...] = y.astype(BF16)


def _combine_ln(x, ys, n_groups, chunk_ids, d0c, d1c, lnp, layer):
    row = lambda i, ng, ck: (i, 0)

    def ln_spec(kind):
        r = (2 + kind) * DEPTH + layer
        return pl.BlockSpec((None, 1, D_MODEL), lambda i, ng, ck: (r, 0, 0))

    grid_spec = pltpu.PrefetchScalarGridSpec(
        num_scalar_prefetch=2,
        grid=(N_TILES,),
        in_specs=[pl.BlockSpec((BM, 1), row),
                  pl.BlockSpec((BM, 1), row),
                  pl.BlockSpec((BM, D_MODEL), row),
                  pl.BlockSpec(memory_space=pl.ANY),
                  ln_spec(0), ln_spec(1)],
        out_specs=[pl.BlockSpec((BM, D_MODEL), row), pl.BlockSpec((BM, D_MODEL), row)],
        scratch_shapes=[pltpu.VMEM((2, COMBINE_GROUP * MOE_SUB, D_MODEL), BF16),
                        pltpu.SemaphoreType.DMA((2,)),
                        pltpu.VMEM((BM, D_MODEL), F32)],
    )
    return pl.pallas_call(
        _combine_ln_kernel,
        grid_spec=grid_spec,
        out_shape=[jax.ShapeDtypeStruct((N_TOK, D_MODEL), F32), jax.ShapeDtypeStruct((N_TOK, D_MODEL), BF16)],
        compiler_params=_params(("arbitrary",), 56),
        name="moe_combine_ln",
    )(n_groups, chunk_ids, d0c, d1c, x, ys, lnp, lnp)


def _token_rows(a, fill):
    pad = TOK_CHUNKS * TOK_CHUNK - N_TOK
    return jnp.pad(a, (0, pad), constant_values=fill).reshape(TOK_CHUNKS, 1, TOK_CHUNK)


def _moe_layer(x, xb, w_router_padded, w_gu, w_down, j, lnp, layer):
    sel, gates = _router(x, w_router_padded, j)
    sel = sel[:, :N_EXPERTS]
    gates = gates[:, :N_EXPERTS]

    cum = jnp.cumsum(sel, axis=0)
    rank = cum - sel
    counts = cum[-1]
    padded = (counts + MOE_PAD - 1) // MOE_PAD * MOE_PAD
    pad_end = jnp.cumsum(padded)
    first = pad_end - padded + (padded - counts) // MOE_SUB * MOE_SUB
    slot = first[None, :] + rank
    chosen = sel > 0
    d0 = jnp.min(jnp.where(chosen, slot, MOE_SLOTS), axis=1).astype(jnp.int32)
    d1 = jnp.max(jnp.where(chosen, slot, -1), axis=1).astype(jnp.int32)
    g0 = jnp.sum(jnp.where(chosen & (slot == d0[:, None]), gates, 0.0), axis=1)
    g1 = jnp.sum(jnp.where(chosen & (slot == d1[:, None]), gates, 0.0), axis=1)

    def tiles(rows_per_tile):
        start = jnp.arange(MOE_SLOTS // rows_per_tile, dtype=jnp.int32) * rows_per_tile
        e = jnp.minimum(jnp.sum(pad_end[None, :] <= start[:, None], axis=1), N_EXPERTS - 1).astype(jnp.int32)
        r0 = jnp.maximum(start - first[e], 0)
        valid = jnp.clip(jnp.minimum(first[e] + counts[e], start + rows_per_tile) - jnp.maximum(first[e], start),
                         0, rows_per_tile)
        return e, r0, valid

    def matmul_tiles(rows_per_tile):
        e, _, valid = tiles(rows_per_tile)
        live_subs = ((valid + MOE_SUB - 1) // MOE_SUB).astype(jnp.int32)
        opens = jnp.concatenate([jnp.ones((1,), bool), (e[1:] != e[:-1]) | (live_subs[:-1] == 0)])
        fresh = ((live_subs > 0) & opens).astype(jnp.int32)
        n_live = (pad_end[-1:] // rows_per_tile).astype(jnp.int32)
        return e, live_subs, fresh, n_live

    sub_e, r0, valid = tiles(MOE_SUB)
    cum_rows = cum.T[sub_e]
    tok_lo = jnp.sum(cum_rows < (r0 + 1)[:, None], axis=1)
    tok_hi = jnp.sum(cum_rows < (r0 + valid)[:, None], axis=1)
    chunk_lo = jnp.where(valid > 0, tok_lo // TOK_CHUNK, 1).astype(jnp.int32)
    chunk_hi = jnp.where(valid > 0, tok_hi // TOK_CHUNK, 0).astype(jnp.int32)

    rank_edges = jnp.concatenate([jnp.zeros((1, N_EXPERTS), cum.dtype), cum[BM - 1::BM]], axis=0)
    lo = first[None, :] + rank_edges[:-1]
    hi = first[None, :] + rank_edges[1:]
    per_expert = COMBINE_KMAX // N_EXPERTS
    cand = (lo // MOE_SUB)[:, :, None] + jnp.arange(per_expert)[None, None, :]
    ok = (hi > lo)[:, :, None] & (cand <= ((hi - 1) // MOE_SUB)[:, :, None])
    cand = cand.reshape(N_TILES, COMBINE_KMAX)
    ok = ok.reshape(N_TILES, COMBINE_KMAX)
    order = jnp.argsort(jnp.logical_not(ok), axis=1, stable=True)
    chunk_ids = jnp.where(jnp.take_along_axis(ok, order, axis=1), jnp.take_along_axis(cand, order, axis=1), -1)
    chunk_ids = chunk_ids.astype(jnp.int32).reshape(-1)
    n_groups = ((jnp.sum(ok, axis=1) + COMBINE_GROUP - 1) // COMBINE_GROUP).astype(jnp.int32)

    xs, gs = _dispatch(xb, chunk_lo, chunk_hi, _token_rows(d0, -1), _token_rows(d1, -1),
                       _token_rows(g0, 0.0), _token_rows(g1, 0.0))
    h = _moe_gu(xs, w_gu, j, *matmul_tiles(MOE_GU_BM))
    ys = _moe_down(h, w_down, j, gs, *matmul_tiles(MOE_DOWN_BM))
    return _combine_ln(x, ys, n_groups, chunk_ids, d0[:, None], d1[:, None], lnp, layer)


def _block_diag_causal(w_s):
    small = jnp.tril(w_s[:, :DEC_SEQ, :DEC_SEQ])
    eye = jnp.eye(DEC_BATCH, dtype=w_s.dtype)
    return jnp.einsum("ab,gij->gaibj", eye, small).reshape(SGU_GROUPS, CHUNK, CHUNK)


def kernel(x_prompt, x_sample, state_swa_k, state_swa_v, state_conv, ln1_g, ln1_b, ln2_g, ln2_b, a_w_in, a_ln_g, a_ln_b, a_w_s, a_b_s, a_w_out, b_w_qkv, b_sinks, b_w_o, c_w_in, c_w_conv, c_w_out, f_w_gu, f_w_down, m_w_router, m_w_gu, m_w_down):
    x = jnp.concatenate([x_prompt.reshape(N_PROMPT, D_MODEL), x_sample.reshape(N_SAMPLE, D_MODEL)], axis=0)
    xb = x.astype(BF16)
    lnp = jnp.concatenate([ln1_g, ln1_b, ln2_g, ln2_b], axis=0).reshape(4 * DEPTH, 1, D_MODEL)
    a_lnp = jnp.concatenate([a_ln_g, a_ln_b], axis=0).reshape(-1, 1, D_MODEL)
    a_w_in_b, a_w_out_b = a_w_in.astype(BF16), a_w_out.astype(BF16)
    b_w_qkv_b, b_w_o_b = b_w_qkv.astype(BF16), b_w_o.astype(BF16)
    c_w_in_b, c_w_out_b = c_w_in.astype(BF16), c_w_out.astype(BF16)
    f_w_gu_b, f_w_down_b = f_w_gu.astype(BF16), f_w_down.astype(BF16)
    w_router_padded = jnp.pad(m_w_router, ((0, 0), (0, 0), (0, 128 - N_EXPERTS)))

    sgu_v, swa_states, conv_states = [], [], []
    for i in range(DEPTH):
        kind, j = i % N_MIXERS, i // N_MIXERS
        if kind == 0:
            z2 = _sgu_in(xb, a_w_in_b, a_lnp, j)
            sgu_v.append(z2[N_PROMPT:, D_MODEL:].reshape(DEC_BATCH, DEC_SEQ, D_MODEL))
            b_t = a_b_s[j].T
            b_t_sample = jnp.tile(b_t[:DEC_SEQ], (DEC_BATCH, 1))
            x, xb = _sgu_out(z2, a_w_s, _block_diag_causal(a_w_s[j]), b_t, b_t_sample, a_w_out_b, j, x, lnp, i)
        elif kind == 1:
            x, xb, st = _swa_mixer(xb, x, state_swa_k[j], state_swa_v[j], b_w_qkv_b, b_sinks[j], b_w_o_b, j, lnp, i)
            swa_states.append(st)
        else:
            x, xb, st = _conv_mixer(xb, x, state_conv[j], c_w_in_b, c_w_conv, c_w_out_b, j, lnp, i)
            conv_states.append(st)
        if i % 2 == 0:
            h = _dense_gu(xb, f_w_gu_b, i // 2)
            x, xb = _mm_ln(h, f_w_down_b, i // 2, x, lnp, 1, i, 320, "dense_down")
        else:
            x, xb = _moe_layer(x, xb, w_router_padded, m_w_gu, m_w_down, i // 2, lnp, i)

    y_prompt = x[:N_PROMPT].reshape(BATCH, SEQ, D_MODEL)
    y_sample = x[N_PROMPT:].reshape(DEC_BATCH, DEC_SEQ, D_MODEL)
    return (y_prompt, y_sample, jnp.stack(sgu_v),
            jnp.stack([s[0] for s in swa_states]), jnp.stack([s[1] for s in swa_states]),
            jnp.stack([s[2] for s in swa_states]), jnp.stack([s[3] for s in swa_states]),
            jnp.stack([s[0] for s in conv_states]), jnp.stack([s[1] for s in conv_states]))
```

```python
import functools

import numpy as np
import jax
import jax.numpy as jnp
from jax import lax
from jax.experimental import pallas as pl
from jax.experimental.pallas import tpu as pltpu

D_MODEL = 2048
BATCH = 2
SEQ = 4096
DEPTH = 4
DEC_BATCH = 32
DEC_SEQ = 4
N_MIXERS = 3
CHUNK = 128
SGU_GROUPS = 8
SGU_GROUP_DIM = D_MODEL // SGU_GROUPS
HEAD_DIM = 64
N_HEADS = D_MODEL // HEAD_DIM
N_KV_HEADS = N_HEADS // 8
GQA_GROUP = N_HEADS // N_KV_HEADS
WINDOW = 128
CONV_WIDTH = 3
D_FF = 5632
N_EXPERTS = 8
LN_EPS = 1e-5
DEEPNORM_ALPHA = (2.0 * DEPTH) ** 0.25
NEG_INF = -1e30

N_PROMPT = BATCH * SEQ
N_SAMPLE = DEC_BATCH * DEC_SEQ
N_TOK = N_PROMPT + N_SAMPLE
BM = 640
N_TILES = N_TOK // BM
SGU_BM = 128
SGU_TILES = N_TOK // SGU_BM

MOE_SUB = 256
MOE_PAD = 4 * MOE_SUB
MOE_SLOTS = (-(-(2 * N_TOK) // MOE_PAD) + N_EXPERTS) * MOE_PAD
MOE_GU_BM = MOE_PAD
MOE_DOWN_BM = 2 * MOE_SUB
MOE_BN = 512
COMBINE_GROUP = 4
TOK_CHUNK = 512
TOK_CHUNKS_FULL = N_TOK // TOK_CHUNK
TOK_TAIL = N_TOK - TOK_CHUNKS_FULL * TOK_CHUNK
TOK_CHUNKS = TOK_CHUNKS_FULL + 1
COMBINE_KMAX = N_EXPERTS * ((MOE_SUB - 1 + BM - 1) // MOE_SUB + 1)

F32 = jnp.float32
BF16 = jnp.bfloat16
MIB = 1024 * 1024


def _params(semantics, vmem_mib):
    return pltpu.CompilerParams(dimension_semantics=semantics, vmem_limit_bytes=vmem_mib * MIB)


def _layer_norm(y, g, b):
    mu = jnp.mean(y, axis=-1, keepdims=True)
    d = y - mu
    var = jnp.mean(d * d, axis=-1, keepdims=True)
    return d * lax.rsqrt(var + LN_EPS) * g + b


def _gelu(x):
    return 0.5 * x * (1.0 + lax.erf(x * np.float32(np.sqrt(0.5))))


def _dot(a, b):
    return jnp.dot(a, b, preferred_element_type=F32)


def _ln_specs(norm, layer, ngrid):
    def spec(kind):
        r = (2 * norm + kind) * DEPTH + layer
        if ngrid == 1:
            return pl.BlockSpec((None, 1, D_MODEL), lambda i: (r, 0, 0))
        return pl.BlockSpec((None, 1, D_MODEL), lambda j, i: (r, 0, 0))
    return [spec(0), spec(1)]


def _mm_kernel(x_ref, w_ref, o_ref):
    o_ref[...] = _dot(x_ref[...], w_ref[...]).astype(o_ref.dtype)


def _mm(x, w, layer, bn, out_dtype, name):
    m, k = x.shape
    n = w.shape[-1]
    return pl.pallas_call(
        _mm_kernel,
        grid=(n // bn, m // BM),
        in_specs=[pl.BlockSpec((BM, k), lambda j, i: (i, 0)),
                  pl.BlockSpec((None, k, bn), lambda j, i: (layer, 0, j))],
        out_specs=pl.BlockSpec((BM, bn), lambda j, i: (i, j)),
        out_shape=jax.ShapeDtypeStruct((m, n), out_dtype),
        compiler_params=_params(("arbitrary", "arbitrary"), 48),
        name=name,
    )(x, w)


def _mm_ln_kernel(x_ref, w_ref, r_ref, g_ref, b_ref, of_ref, ob_ref):
    bm = x_ref.shape[0]
    group = bm // 4 if x_ref.shape[1] <= D_MODEL else bm // 2
    for start in range(0, bm, group):
        rows = slice(start, start + group)
        f = _dot(x_ref[rows, :], w_ref[...])
        y = _layer_norm(DEEPNORM_ALPHA * r_ref[rows, :] + f, g_ref[...], b_ref[...])
        of_ref[rows, :] = y
        ob_ref[rows, :] = y.astype(BF16)


def _mm_ln(x, w, w_layer, resid, lnp, norm, layer, bm, name):
    m, k = x.shape
    row = lambda i: (i, 0)
    return pl.pallas_call(
        _mm_ln_kernel,
        grid=(m // bm,),
        in_specs=[pl.BlockSpec((bm, k), row),
                  pl.BlockSpec((None, k, D_MODEL), lambda i: (w_layer, 0, 0), pipeline_mode=pl.Buffered(1)),
                  pl.BlockSpec((bm, D_MODEL), row)] + _ln_specs(norm, layer, 1),
        out_specs=[pl.BlockSpec((bm, D_MODEL), row), pl.BlockSpec((bm, D_MODEL), row)],
        out_shape=[jax.ShapeDtypeStruct((m, D_MODEL), F32), jax.ShapeDtypeStruct((m, D_MODEL), BF16)],
        compiler_params=_params(("arbitrary",), 56),
        name=name,
    )(x, w, resid, lnp, lnp)


def _sgu_in_kernel(x_ref, w_ref, g_ref, b_ref, o_ref):
    half = BM // 2

    @pl.when(pl.program_id(0) == 0)
    def _():
        for rows in (slice(0, half), slice(half, BM)):
            o_ref[rows, :] = _gelu(_dot(x_ref[rows, :], w_ref[...]))

    @pl.when(pl.program_id(0) == 1)
    def _():
        for rows in (slice(0, half), slice(half, BM)):
            o_ref[rows, :] = _layer_norm(_gelu(_dot(x_ref[rows, :], w_ref[...])), g_ref[...], b_ref[...])


def _sgu_in(xb, w_in, a_lnp, j):
    return pl.pallas_call(
        _sgu_in_kernel,
        grid=(2, N_TILES),
        in_specs=[pl.BlockSpec((BM, D_MODEL), lambda c, i: (i, 0)),
                  pl.BlockSpec((None, D_MODEL, D_MODEL), lambda c, i: (j, 0, c)),
                  pl.BlockSpec((None, 1, D_MODEL), lambda c, i: (j, 0, 0)),
                  pl.BlockSpec((None, 1, D_MODEL), lambda c, i: (2 + j, 0, 0))],
        out_specs=pl.BlockSpec((BM, D_MODEL), lambda c, i: (i, c)),
        out_shape=jax.ShapeDtypeStruct((N_TOK, 2 * D_MODEL), F32),
        compiler_params=_params(("arbitrary", "arbitrary"), 56),
        name="sgu_in",
    )(xb, w_in, a_lnp, a_lnp)


def _sgu_out_kernel(u_ref, v_ref, wp_ref, ws_ref, bp_ref, bs_ref, wo_ref, r_ref, g_ref, b_ref, of_ref, ob_ref):
    is_sample_tile = pl.program_id(0) == SGU_TILES - 1
    row = lax.broadcasted_iota(jnp.int32, (CHUNK, CHUNK), 0)
    col = lax.broadcasted_iota(jnp.int32, (CHUNK, CHUNK), 1)
    causal = row >= col
    groups = range(SGU_GROUPS)
    cols = [slice(grp * SGU_GROUP_DIM, (grp + 1) * SGU_GROUP_DIM) for grp in groups]
    w = [jnp.where(is_sample_tile, ws_ref[grp], jnp.where(causal, wp_ref[grp], 0.0)).astype(BF16) for grp in groups]
    bias = [jnp.where(is_sample_tile, bs_ref[:, grp:grp + 1], bp_ref[:, grp:grp + 1]) for grp in groups]
    mixed = [_dot(w[grp], v_ref[:, cols[grp]].astype(BF16)) + bias[grp] for grp in groups]
    gated = jnp.concatenate([(u_ref[:, cols[grp]] * mixed[grp]).astype(BF16) for grp in groups], axis=1)
    f = _dot(gated, wo_ref[...])
    y = _layer_norm(DEEPNORM_ALPHA * r_ref[...] + f, g_ref[...], b_ref[...])
    of_ref[...] = y
    ob_ref[...] = y.astype(BF16)


def _sgu_out(z2, w_s, w_s_sample, b_t, b_t_sample, w_out, j, resid, lnp, layer):
    row = lambda i: (i, 0)
    const2 = lambda i: (0, 0)
    const3 = lambda i: (0, 0, 0)
    return pl.pallas_call(
        _sgu_out_kernel,
        grid=(SGU_TILES,),
        in_specs=[pl.BlockSpec((SGU_BM, D_MODEL), lambda i: (i, 0)),
                  pl.BlockSpec((SGU_BM, D_MODEL), lambda i: (i, 1)),
                  pl.BlockSpec((None, SGU_GROUPS, CHUNK, CHUNK), lambda i: (j, 0, 0, 0)),
                  pl.BlockSpec((SGU_GROUPS, CHUNK, CHUNK), const3),
                  pl.BlockSpec((CHUNK, SGU_GROUPS), const2),
                  pl.BlockSpec((CHUNK, SGU_GROUPS), const2),
                  pl.BlockSpec((None, D_MODEL, D_MODEL), lambda i: (j, 0, 0), pipeline_mode=pl.Buffered(1)),
                  pl.BlockSpec((SGU_BM, D_MODEL), row)] + _ln_specs(0, layer, 1),
        out_specs=[pl.BlockSpec((SGU_BM, D_MODEL), row), pl.BlockSpec((SGU_BM, D_MODEL), row)],
        out_shape=[jax.ShapeDtypeStruct((N_TOK, D_MODEL), F32), jax.ShapeDtypeStruct((N_TOK, D_MODEL), BF16)],
        compiler_params=_params(("arbitrary",), 40),
        name="sgu_out",
    )(z2, z2, w_s, w_s_sample, b_t, b_t_sample, w_out, resid, lnp, lnp)


def _gu_kernel(x_ref, wg_ref, wu_ref, o_ref, wgb_ref, wub_ref):
    def run(weights):
        wg, wu = weights()
        x = x_ref[...]
        o_ref[...] = (jax.nn.silu(_dot(x, wg)) * _dot(x, wu)).astype(o_ref.dtype)

    def load_weights():
        wg = wg_ref[...].astype(BF16)
        wu = wu_ref[...].astype(BF16)
        wgb_ref[...] = wg
        wub_ref[...] = wu
        return wg, wu

    def kept_weights():
        return wgb_ref[...], wub_ref[...]

    first = pl.program_id(1) == 0
    pl.when(first)(functools.partial(run, load_weights))
    pl.when(jnp.logical_not(first))(functools.partial(run, kept_weights))


def _dense_gu(xb, w_gu, layer):
    bn = 512
    nt = D_FF // bn
    return pl.pallas_call(
        _gu_kernel,
        grid=(nt, N_TILES),
        in_specs=[pl.BlockSpec((BM, D_MODEL), lambda j, i: (i, 0)),
                  pl.BlockSpec((None, D_MODEL, bn), lambda j, i: (layer, 0, j)),
                  pl.BlockSpec((None, D_MODEL, bn), lambda j, i: (layer, 0, j + nt))],
        out_specs=pl.BlockSpec((BM, bn), lambda j, i: (i, j)),
        out_shape=jax.ShapeDtypeStruct((N_TOK, D_FF), BF16),
        scratch_shapes=[pltpu.VMEM((D_MODEL, bn), BF16), pltpu.VMEM((D_MODEL, bn), BF16)],
        compiler_params=_params(("arbitrary", "arbitrary"), 40),
        name="dense_gu",
    )(xb, w_gu, w_gu)


def _attn_kernel(sink_ref, q_ref, kp_ref, vp_ref, kc_ref, vc_ref, o_ref, *, blocks_per_seq):
    has_prev = (pl.program_id(0) % blocks_per_seq) != 0
    qi = lax.broadcasted_iota(jnp.int32, (WINDOW, 2 * WINDOW), 0)
    kj = lax.broadcasted_iota(jnp.int32, (WINDOW, 2 * WINDOW), 1)
    delta_i = WINDOW + qi - kj
    valid = (delta_i >= 0) & (delta_i <= WINDOW) & (has_prev | (kj >= WINDOW))
    delta = delta_i.astype(F32)
    low_half = lax.broadcasted_iota(jnp.int32, (2 * WINDOW, 128), 1) < HEAD_DIM

    kk = jnp.concatenate([kp_ref[...], kc_ref[...]], axis=0)
    vv = jnp.concatenate([vp_ref[...], vc_ref[...]], axis=0)

    def halves(slab, kv):
        if kv % 2 == 0:
            lo = jnp.where(low_half, slab, 0.0)
            hi = jnp.where(low_half, 0.0, pltpu.roll(slab, HEAD_DIM, 1))
        else:
            lo = jnp.where(low_half, pltpu.roll(slab, HEAD_DIM, 1), 0.0)
            hi = jnp.where(low_half, 0.0, slab)
        return lo.astype(BF16), hi.astype(BF16)

    for kv in range(N_KV_HEADS):
        lanes = slice((kv // 2) * 128, (kv // 2 + 1) * 128)
        k_both = jnp.concatenate(halves(kk[:, lanes], kv), axis=0)
        v_both = jnp.concatenate(halves(vv[:, lanes], kv), axis=0)
        pairs = range(GQA_GROUP // 2)
        heads = range(GQA_GROUP)
        qcols = [slice((kv * 4 + pair) * 128, (kv * 4 + pair + 1) * 128) for pair in pairs]
        scores = [lax.dot_general(q_ref[:, qcols[pair]].astype(BF16), k_both, (((1,), (1,)), ((), ())),
                                  preferred_element_type=F32) * (HEAD_DIM ** -0.5) for pair in pairs]
        slopes = [float(2.0 ** (-8.0 * (kv * GQA_GROUP + g + 1) / N_HEADS)) for g in heads]
        sinks = [sink_ref[kv * GQA_GROUP + g] for g in heads]
        s = [scores[g // 2][:, (g % 2) * 2 * WINDOW:(g % 2 + 1) * 2 * WINDOW] - slopes[g] * delta for g in heads]
        s = [jnp.where(valid, s[g], NEG_INF) for g in heads]
        mx = [jnp.maximum(jnp.max(s[g], axis=-1, keepdims=True), sinks[g]) for g in heads]
        p = [jnp.exp(s[g] - mx[g]) for g in heads]
        den = [jnp.sum(p[g], axis=-1, keepdims=True) + jnp.exp(sinks[g] - mx[g]) for g in heads]
        p = [(p[g] / den[g]).astype(BF16) for g in heads]
        for pair in pairs:
            both = jnp.concatenate([p[2 * pair], p[2 * pair + 1]], axis=1)
            o_ref[:, qcols[pair]] = _dot(both, v_both).astype(o_ref.dtype)


def _attention(sinks, q, kp, vp, kc, vc, n_blocks, specs, blocks_per_seq, name):
    return pl.pallas_call(
        functools.partial(_attn_kernel, blocks_per_seq=blocks_per_seq),
        grid=(n_blocks,),
        in_specs=[pl.BlockSpec(memory_space=pltpu.SMEM)] + specs,
        out_specs=pl.BlockSpec((WINDOW, D_MODEL), lambda i: (i, 0)),
        out_shape=jax.ShapeDtypeStruct((n_blocks * WINDOW, D_MODEL), BF16),
        compiler_params=_params(("arbitrary",), 32),
        name=name,
    )(sinks, q, kp, vp, kc, vc)


SAMPLE_ROWS = GQA_GROUP * DEC_SEQ
SAMPLE_KEYS = 2 * WINDOW


def _attn_sample_kernel(q_ref, k_ref, v_ref, sink_ref, slope_ref, o_ref):
    step = lax.broadcasted_iota(jnp.int32, (SAMPLE_ROWS, SAMPLE_KEYS), 0) % DEC_SEQ
    kj = lax.broadcasted_iota(jnp.int32, (SAMPLE_ROWS, SAMPLE_KEYS), 1)
    delta_i = WINDOW + step - kj
    valid = (delta_i >= 0) & (delta_i <= WINDOW)
    delta = delta_i.astype(F32)
    kvs = range(N_KV_HEADS)
    lanes = [slice((kv // 2) * 128, (kv // 2 + 1) * 128) for kv in kvs]
    s = [lax.dot_general(q_ref[kv].astype(BF16), k_ref[:, lanes[kv]].astype(BF16), (((1,), (1,)), ((), ())),
                         preferred_element_type=F32) * (HEAD_DIM ** -0.5) for kv in kvs]
    s = [jnp.where(valid, s[kv] - slope_ref[kv] * delta, NEG_INF) for kv in kvs]
    mx = [jnp.maximum(jnp.max(s[kv], axis=-1, keepdims=True), sink_ref[kv]) for kv in kvs]
    p = [jnp.exp(s[kv] - mx[kv]) for kv in kvs]
    den = [jnp.sum(p[kv], axis=-1, keepdims=True) + jnp.exp(sink_ref[kv] - mx[kv]) for kv in kvs]
    p = [(p[kv] / den[kv]).astype(BF16) for kv in kvs]
    for kv in kvs:
        o_ref[kv] = _dot(p[kv], v_ref[:, lanes[kv]].astype(BF16)).astype(o_ref.dtype)


def _attn_sample(q_s, k_new, v_new, state_k, state_v, sinks):
    kvw = N_KV_HEADS * HEAD_DIM
    q = q_s.reshape(DEC_BATCH, DEC_SEQ, N_KV_HEADS, GQA_GROUP, HEAD_DIM).transpose(0, 2, 3, 1, 4)
    q = q.reshape(DEC_BATCH, N_KV_HEADS, SAMPLE_ROWS, HEAD_DIM)
    zero = jnp.zeros_like(q)
    odd = (jnp.arange(N_KV_HEADS) % 2 == 1)[None, :, None, None]
    q = jnp.where(odd, jnp.concatenate([zero, q], axis=-1), jnp.concatenate([q, zero], axis=-1))

    def keys(state, new):
        pad = jnp.zeros((DEC_BATCH, SAMPLE_KEYS - WINDOW - DEC_SEQ, kvw), F32)
        return jnp.concatenate([state.reshape(DEC_BATCH, WINDOW, kvw), new, pad], axis=1)

    head = np.arange(N_KV_HEADS)[:, None] * GQA_GROUP + np.arange(SAMPLE_ROWS)[None, :] // DEC_SEQ
    slopes = jnp.asarray(np.exp2(-8.0 * (head + 1) / N_HEADS), F32)[:, :, None]
    sink_rows = sinks[head][:, :, None]
    seq3 = lambda i: (i, 0, 0)
    whole = lambda i: (0, 0, 0)
    out = pl.pallas_call(
        _attn_sample_kernel,
        grid=(DEC_BATCH,),
        in_specs=[pl.BlockSpec((None, N_KV_HEADS, SAMPLE_ROWS, 128), lambda i: (i, 0, 0, 0)),
                  pl.BlockSpec((None, SAMPLE_KEYS, kvw), seq3),
                  pl.BlockSpec((None, SAMPLE_KEYS, kvw), seq3),
                  pl.BlockSpec((N_KV_HEADS, SAMPLE_ROWS, 1), whole),
                  pl.BlockSpec((N_KV_HEADS, SAMPLE_ROWS, 1), whole)],
        out_specs=pl.BlockSpec((None, N_KV_HEADS, SAMPLE_ROWS, 128), lambda i: (i, 0, 0, 0)),
        out_shape=jax.ShapeDtypeStruct((DEC_BATCH, N_KV_HEADS, SAMPLE_ROWS, 128), BF16),
        compiler_params=_params(("arbitrary",), 16),
        name="attn_sample",
    )(q, keys(state_k, k_new), keys(state_v, v_new), sink_rows, slopes)
    out = jnp.where(odd, out[..., HEAD_DIM:], out[..., :HEAD_DIM])
    out = out.reshape(DEC_BATCH, N_KV_HEADS, GQA_GROUP, DEC_SEQ, HEAD_DIM).transpose(0, 3, 1, 2, 4)
    return out.reshape(N_SAMPLE, D_MODEL)


def _swa_mixer(xb, x, state_k, state_v, w_qkv, sinks, w_o, j, lnp, layer):
    kvw = N_KV_HEADS * HEAD_DIM
    qkv = _mm(xb, w_qkv, j, 1280, F32, "qkv")
    kcol, vcol = D_MODEL // kvw, D_MODEL // kvw + 1
    nb = SEQ // WINDOW

    def prev_block(i):
        return jnp.maximum(i - 1, 0)

    prompt_specs = [pl.BlockSpec((WINDOW, D_MODEL), lambda i: (i, 0)),
                    pl.BlockSpec((WINDOW, kvw), lambda i: (prev_block(i), kcol)),
                    pl.BlockSpec((WINDOW, kvw), lambda i: (prev_block(i), vcol)),
                    pl.BlockSpec((WINDOW, kvw), lambda i: (i, kcol)),
                    pl.BlockSpec((WINDOW, kvw), lambda i: (i, vcol))]
    o_prompt = _attention(sinks, qkv, qkv, qkv, qkv, qkv, N_PROMPT // WINDOW, prompt_specs, nb, "attn_prompt")

    qkv_s = qkv[N_PROMPT:].reshape(DEC_BATCH, DEC_SEQ, D_MODEL + 2 * kvw)
    k_new = qkv_s[..., D_MODEL:D_MODEL + kvw]
    v_new = qkv_s[..., D_MODEL + kvw:]
    o_sample = _attn_sample(qkv_s[..., :D_MODEL], k_new, v_new, state_k, state_v, sinks)
    o_all = jnp.concatenate([o_prompt, o_sample], axis=0)
    x, xb = _mm_ln(o_all, w_o, j, x, lnp, 0, layer, BM, "attn_out")

    k_p = qkv[:N_PROMPT, D_MODEL:D_MODEL + kvw].reshape(BATCH, SEQ, N_KV_HEADS, HEAD_DIM)[:, -WINDOW:]
    v_p = qkv[:N_PROMPT, D_MODEL + kvw:].reshape(BATCH, SEQ, N_KV_HEADS, HEAD_DIM)[:, -WINDOW:]
    k_s = jnp.concatenate([state_k, k_new.reshape(DEC_BATCH, DEC_SEQ, N_KV_HEADS, HEAD_DIM)], axis=1)[:, -WINDOW:]
    v_s = jnp.concatenate([state_v, v_new.reshape(DEC_BATCH, DEC_SEQ, N_KV_HEADS, HEAD_DIM)], axis=1)[:, -WINDOW:]
    return x, xb, (k_p, v_p, k_s, v_s)


CONV_BC = 512
SAMPLE_ROW0 = N_PROMPT - (N_TILES - 1) * BM


def _conv_taps(z, z1, z2, gb, taps_ref):
    conv = taps_ref[0:1, :] * z2 + taps_ref[1:2, :] * z1 + taps_ref[2:3, :] * z
    return (gb * conv).astype(BF16)


def _conv_kernel(x_ref, wb_ref, wc_ref, wh_ref, taps_ref, pa_ref, pb_ref, y_ref, z_ref, carry_ref):
    i = pl.program_id(1)

    @pl.when(i == 0)
    def _():
        carry_ref[...] = jnp.zeros_like(carry_ref)

    x = x_ref[...]
    gate_b = _dot(x, wb_ref[...])
    z = _dot(x, wc_ref[...]) * _dot(x, wh_ref[...])
    local = lax.broadcasted_iota(jnp.int32, z.shape, 0)
    pos = (i * BM + local) % SEQ
    c6 = carry_ref[6:7, :]
    c7 = carry_ref[7:8, :]
    back1 = jnp.where(local >= 1, pltpu.roll(z, 1, 0), c7)
    back2 = jnp.where(local >= 2, pltpu.roll(z, 2, 0), jnp.where(local == 1, c7, c6))
    z1 = jnp.where(pos >= 1, back1, 0.0)
    z2 = jnp.where(pos >= 2, back2, 0.0)
    y_ref[...] = _conv_taps(z, z1, z2, gate_b, taps_ref)
    z_ref[...] = z
    carry_ref[...] = z[BM - 8:, :]

    @pl.when(i == N_TILES - 1)
    def _():
        zs = z[SAMPLE_ROW0:, :]
        t = lax.broadcasted_iota(jnp.int32, zs.shape, 0) % DEC_SEQ
        z1s = jnp.where(t >= 1, pltpu.roll(zs, 1, 0), pa_ref[...])
        z2s = jnp.where(t >= 2, pltpu.roll(zs, 2, 0), pb_ref[...])
        y_ref[SAMPLE_ROW0:, :] = _conv_taps(zs, z1s, z2s, gate_b[SAMPLE_ROW0:, :], taps_ref)


def _conv_mixer(xb, x, state, w_in, w_conv, w_out, j, lnp, layer):
    zeros = jnp.zeros((DEC_BATCH, DEC_SEQ, D_MODEL), F32)
    past_a = zeros.at[:, 0].set(state[:, 1]).reshape(N_SAMPLE, D_MODEL)
    past_b = zeros.at[:, 0].set(state[:, 0]).at[:, 1].set(state[:, 1]).reshape(N_SAMPLE, D_MODEL)
    ncb = D_MODEL // CONV_BC
    tile = lambda c, i: (i, c)
    y, z = pl.pallas_call(
        _conv_kernel,
        grid=(ncb, N_TILES),
        in_specs=[pl.BlockSpec((BM, D_MODEL), lambda c, i: (i, 0)),
                  pl.BlockSpec((None, D_MODEL, CONV_BC), lambda c, i: (j, 0, c)),
                  pl.BlockSpec((None, D_MODEL, CONV_BC), lambda c, i: (j, 0, ncb + c)),
                  pl.BlockSpec((None, D_MODEL, CONV_BC), lambda c, i: (j, 0, 2 * ncb + c)),
                  pl.BlockSpec((None, CONV_WIDTH, CONV_BC), lambda c, i: (j, 0, c)),
                  pl.BlockSpec((N_SAMPLE, CONV_BC), lambda c, i: (0, c)),
                  pl.BlockSpec((N_SAMPLE, CONV_BC), lambda c, i: (0, c))],
        out_specs=[pl.BlockSpec((BM, CONV_BC), tile), pl.BlockSpec((BM, CONV_BC), tile)],
        out_shape=[jax.ShapeDtypeStruct((N_TOK, D_MODEL), BF16), jax.ShapeDtypeStruct((N_TOK, D_MODEL), F32)],
        scratch_shapes=[pltpu.VMEM((8, CONV_BC), F32)],
        compiler_params=_params(("arbitrary", "arbitrary"), 40),
        name="conv_in",
    )(xb, w_in, w_in, w_in, w_conv, past_a, past_b)
    x, xb = _mm_ln(y, w_out, j, x, lnp, 0, layer, BM, "conv_out")
    conv_p = z[:N_PROMPT].reshape(BATCH, SEQ, D_MODEL)[:, -(CONV_WIDTH - 1):]
    conv_s = z[N_PROMPT:].reshape(DEC_BATCH, DEC_SEQ, D_MODEL)[:, -(CONV_WIDTH - 1):]
    return x, xb, (conv_p, conv_s)


def _router_kernel(x_ref, w_ref, sel_ref, gate_ref):
    logits = jnp.dot(x_ref[...], w_ref[...], preferred_element_type=F32, precision=lax.Precision.HIGHEST)
    lane = lax.broadcasted_iota(jnp.int32, logits.shape, 1)
    neg = jnp.float32(-jnp.inf)
    l1 = jnp.where(lane < N_EXPERTS, logits, neg)
    m1 = jnp.max(l1, axis=-1, keepdims=True)
    i1 = jnp.min(jnp.where(l1 == m1, lane, 128), axis=-1, keepdims=True)
    l2 = jnp.where(lane == i1, neg, l1)
    m2 = jnp.max(l2, axis=-1, keepdims=True)
    i2 = jnp.min(jnp.where(l2 == m2, lane, 128), axis=-1, keepdims=True)
    e2 = jnp.exp(m2 - m1)
    den = 1.0 + e2
    g1 = 1.0 / den
    g2 = e2 / den
    sel_ref[...] = ((lane == i1) | (lane == i2)).astype(jnp.int32)
    gate_ref[...] = jnp.where(lane == i1, g1, jnp.where(lane == i2, g2, 0.0))


def _router(x, w_router_padded, layer):
    row = lambda i: (i, 0)
    return pl.pallas_call(
        _router_kernel,
        grid=(N_TILES,),
        in_specs=[pl.BlockSpec((BM, D_MODEL), row),
                  pl.BlockSpec((None, D_MODEL, 128), lambda i: (layer, 0, 0))],
        out_specs=[pl.BlockSpec((BM, 128), row), pl.BlockSpec((BM, 128), row)],
        out_shape=[jax.ShapeDtypeStruct((N_TOK, 128), jnp.int32), jax.ShapeDtypeStruct((N_TOK, 128), F32)],
        compiler_params=_params(("arbitrary",), 32),
        name="router",
    )(x, w_router_padded)


def _dispatch_kernel(lo_ref, hi_ref, d0_ref, d1_ref, g0_ref, g1_ref, x_ref, xs_ref, gs_ref, acc_ref, gacc_ref):
    i = pl.program_id(0)
    lo = lo_ref[i]
    hi = hi_ref[i]
    slot = i * MOE_SUB + lax.broadcasted_iota(jnp.int32, (MOE_SUB, TOK_CHUNK), 0)
    acc_ref[...] = jnp.zeros_like(acc_ref)
    gacc_ref[...] = jnp.zeros_like(gacc_ref)

    def chunk(c, width, x_rows):
        s = slot[:, :width]
        m0 = s == d0_ref[c][:, :width]
        m1 = s == d1_ref[c][:, :width]
        onehot = jnp.where(m0 | m1, 1.0, 0.0).astype(BF16)
        acc_ref[...] += _dot(onehot, x_rows)
        picked = jnp.where(m0, g0_ref[c][:, :width], 0.0) + jnp.where(m1, g1_ref[c][:, :width], 0.0)
        gacc_ref[...] += jnp.sum(picked, axis=1, keepdims=True)

    def body(c, carry):
        start = pl.multiple_of(c * TOK_CHUNK, TOK_CHUNK)
        chunk(c, TOK_CHUNK, x_ref[pl.ds(start, TOK_CHUNK), :])
        return carry

    lax.fori_loop(lo, jnp.minimum(hi, TOK_CHUNKS_FULL - 1) + 1, body, 0)

    @pl.when(hi == TOK_CHUNKS_FULL)
    def _():
        chunk(TOK_CHUNKS_FULL, TOK_TAIL, x_ref[TOK_CHUNKS_FULL * TOK_CHUNK:, :])

    xs_ref[...] = acc_ref[...].astype(BF16)
    gs_ref[...] = gacc_ref[...]


def _dispatch(xb, chunk_lo, chunk_hi, d0r, d1r, g0r, g1r):
    n_tiles = MOE_SLOTS // MOE_SUB
    whole3 = lambda i, lo, hi: (0, 0, 0)
    rows = pl.BlockSpec((TOK_CHUNKS, 1, TOK_CHUNK), whole3)
    grid_spec = pltpu.PrefetchScalarGridSpec(
        num_scalar_prefetch=2,
        grid=(n_tiles,),
        in_specs=[rows, rows, rows, rows,
                  pl.BlockSpec((N_TOK, D_MODEL), lambda i, lo, hi: (0, 0))],
        out_specs=[pl.BlockSpec((MOE_SUB, D_MODEL), lambda i, lo, hi: (i, 0)),
                   pl.BlockSpec((MOE_SUB, 1), lambda i, lo, hi: (i, 0))],
        scratch_shapes=[pltpu.VMEM((MOE_SUB, D_MODEL), F32), pltpu.VMEM((MOE_SUB, 1), F32)],
    )
    return pl.pallas_call(
        _dispatch_kernel,
        grid_spec=grid_spec,
        out_shape=[jax.ShapeDtypeStruct((MOE_SLOTS, D_MODEL), BF16), jax.ShapeDtypeStruct((MOE_SLOTS, 1), F32)],
        compiler_params=_params(("arbitrary",), 52),
        name="moe_dispatch",
    )(chunk_lo, chunk_hi, d0r, d1r, g0r, g1r, xb)


def _for_expert_tile(n_sub, live_subs, fresh, o_ref, compute, load_weights, kept_weights):
    total = n_sub * MOE_SUB

    def run(k, weights):
        dead = (n_sub - k) * MOE_SUB
        o_ref[dead:, :] = compute(slice(dead, total), weights())
        if dead:
            o_ref[:dead, :] = jnp.zeros((dead, o_ref.shape[1]), o_ref.dtype)

    for k in range(1, n_sub + 1):
        pl.when((fresh == 1) & (live_subs == k))(functools.partial(run, k, load_weights))
    pl.when((fresh == 0) & (live_subs == n_sub))(functools.partial(run, n_sub, kept_weights))

    @pl.when(live_subs == 0)
    def _():
        o_ref[...] = jnp.zeros_like(o_ref)


def _moe_gu_kernel(be_ref, ls_ref, fr_ref, nt_ref, x_ref, wg_ref, wu_ref, o_ref, wgb_ref, wub_ref):
    i = pl.program_id(1)

    def load_weights():
        wg = wg_ref[...].astype(BF16)
        wu = wu_ref[...].astype(BF16)
        wgb_ref[...] = wg
        wub_ref[...] = wu
        return wg, wu

    def kept_weights():
        return wgb_ref[...], wub_ref[...]

    def compute(rows, weights):
        x = x_ref[rows, :]
        gate = _dot(x, weights[0])
        up = _dot(x, weights[1])
        return (jax.nn.silu(gate) * up).astype(o_ref.dtype)

    _for_expert_tile(MOE_GU_BM // MOE_SUB, ls_ref[i], fr_ref[i], o_ref, compute, load_weights, kept_weights)


def _live_tile(i, nt):
    return jnp.minimum(i, nt[0] - 1)


def _moe_gu(xs, w_gu, layer, block_e, live_subs, fresh, n_live):
    nt = D_FF // MOE_BN
    grid_spec = pltpu.PrefetchScalarGridSpec(
        num_scalar_prefetch=4,
        grid=(nt, MOE_SLOTS // MOE_GU_BM),
        in_specs=[pl.BlockSpec((MOE_GU_BM, D_MODEL), lambda j, i, be, ls, fr, n: (_live_tile(i, n), 0)),
                  pl.BlockSpec((None, None, D_MODEL, MOE_BN),
                               lambda j, i, be, ls, fr, n: (layer, be[_live_tile(i, n)], 0, j)),
                  pl.BlockSpec((None, None, D_MODEL, MOE_BN),
                               lambda j, i, be, ls, fr, n: (layer, be[_live_tile(i, n)], 0, j + nt))],
        out_specs=pl.BlockSpec((MOE_GU_BM, MOE_BN), lambda j, i, be, ls, fr, n: (i, j)),
        scratch_shapes=[pltpu.VMEM((D_MODEL, MOE_BN), BF16), pltpu.VMEM((D_MODEL, MOE_BN), BF16)],
    )
    return pl.pallas_call(
        _moe_gu_kernel,
        grid_spec=grid_spec,
        out_shape=jax.ShapeDtypeStruct((MOE_SLOTS, D_FF), BF16),
        compiler_params=_params(("arbitrary", "arbitrary"), 48),
        name="moe_gu",
    )(block_e, live_subs, fresh, n_live, xs, w_gu, w_gu)


def _moe_down_kernel(be_ref, ls_ref, fr_ref, nt_ref, h_ref, w_ref, gs_ref, o_ref, wb_ref):
    i = pl.program_id(1)

    def load_weights():
        w = w_ref[...].astype(BF16)
        wb_ref[...] = w
        return w

    def kept_weights():
        return wb_ref[...]

    def compute(rows, w):
        return (_dot(h_ref[rows, :], w) * gs_ref[rows, :]).astype(o_ref.dtype)

    _for_expert_tile(MOE_DOWN_BM // MOE_SUB, ls_ref[i], fr_ref[i], o_ref, compute, load_weights, kept_weights)


def _moe_down(h, w_down, layer, gs, block_e, live_subs, fresh, n_live):
    grid_spec = pltpu.PrefetchScalarGridSpec(
        num_scalar_prefetch=4,
        grid=(D_MODEL // MOE_BN, MOE_SLOTS // MOE_DOWN_BM),
        in_specs=[pl.BlockSpec((MOE_DOWN_BM, D_FF), lambda j, i, be, ls, fr, n: (_live_tile(i, n), 0)),
                  pl.BlockSpec((None, None, D_FF, MOE_BN),
                               lambda j, i, be, ls, fr, n: (layer, be[_live_tile(i, n)], 0, j)),
                  pl.BlockSpec((MOE_DOWN_BM, 1), lambda j, i, be, ls, fr, n: (_live_tile(i, n), 0))],
        out_specs=pl.BlockSpec((MOE_DOWN_BM, MOE_BN), lambda j, i, be, ls, fr, n: (i, j)),
        scratch_shapes=[pltpu.VMEM((D_FF, MOE_BN), BF16)],
    )
    return pl.pallas_call(
        _moe_down_kernel,
        grid_spec=grid_spec,
        out_shape=jax.ShapeDtypeStruct((MOE_SLOTS, D_MODEL), BF16),
        compiler_params=_params(("arbitrary", "arbitrary"), 52),
        name="moe_down",
    )(block_e, live_subs, fresh, n_live, h, w_down, gs)


def _combine_ln_kernel(ng_ref, ck_ref, d0_ref, d1_ref, x_ref, ys_ref, g_ref, b_ref, of_ref, ob_ref,
                       buf_ref, sem_ref, acc_ref):
    i = pl.program_id(0)
    n = ng_ref[i]

    def chunk_id(g, c):
        return ck_ref[i * COMBINE_KMAX + g * COMBINE_GROUP + c]

    def copies(g, which):
        out = []
        for c in range(COMBINE_GROUP):
            cid = chunk_id(g, c)
            start = pl.multiple_of(jnp.maximum(cid, 0) * MOE_SUB, MOE_SUB)
            out.append((cid >= 0, pltpu.make_async_copy(
                ys_ref.at[pl.ds(start, MOE_SUB)],
                buf_ref.at[which, pl.ds(c * MOE_SUB, MOE_SUB)],
                sem_ref.at[which])))
        return out

    def start_group(g, which):
        for present, copy in copies(g, which):
            pl.when(present)(copy.start)

    def wait_group(g, which):
        for present, copy in copies(g, which):
            pl.when(present)(copy.wait)

    @pl.when(i == 0)
    def _():
        buf_ref[...] = jnp.zeros_like(buf_ref)

    @pl.when(n > 0)
    def _():
        start_group(0, 0)

    acc_ref[...] = jnp.zeros_like(acc_ref)
    d0 = d0_ref[...]
    d1 = d1_ref[...]
    lane = lax.broadcasted_iota(jnp.int32, (BM, MOE_SUB), 1)

    def body(g, carry):
        which = g % 2
        wait_group(g, which)

        @pl.when(g + 1 < n)
        def _():
            start_group(g + 1, 1 - which)

        pieces = []
        for c in range(COMBINE_GROUP):
            slot = chunk_id(g, c) * MOE_SUB + lane
            pieces.append(jnp.where((d0 == slot) | (d1 == slot), 1.0, 0.0).astype(BF16))
        acc_ref[...] += _dot(jnp.concatenate(pieces, axis=1), buf_ref[which])
        return carry

    lax.fori_loop(0, n, body, 0)
    y = _layer_norm(DEEPNORM_ALPHA * x_ref[...] + acc_ref[...], g_ref[...], b_ref[...])
    of_ref[...] = y
    ob_ref[...] = y.astype(BF16)


def _combine_ln(x, ys, n_groups, chunk_ids, d0c, d1c, lnp, layer):
    row = lambda i, ng, ck: (i, 0)

    def ln_spec(kind):
        r = (2 + kind) * DEPTH + layer
        return pl.BlockSpec((None, 1, D_MODEL), lambda i, ng, ck: (r, 0, 0))

    grid_spec = pltpu.PrefetchScalarGridSpec(
        num_scalar_prefetch=2,
        grid=(N_TILES,),
        in_specs=[pl.BlockSpec((BM, 1), row),
                  pl.BlockSpec((BM, 1), row),
                  pl.BlockSpec((BM, D_MODEL), row),
                  pl.BlockSpec(memory_space=pl.ANY),
                  ln_spec(0), ln_spec(1)],
        out_specs=[pl.BlockSpec((BM, D_MODEL), row), pl.BlockSpec((BM, D_MODEL), row)],
        scratch_shapes=[pltpu.VMEM((2, COMBINE_GROUP * MOE_SUB, D_MODEL), BF16),
                        pltpu.SemaphoreType.DMA((2,)),
                        pltpu.VMEM((BM, D_MODEL), F32)],
    )
    return pl.pallas_call(
        _combine_ln_kernel,
        grid_spec=grid_spec,
        out_shape=[jax.ShapeDtypeStruct((N_TOK, D_MODEL), F32), jax.ShapeDtypeStruct((N_TOK, D_MODEL), BF16)],
        compiler_params=_params(("arbitrary",), 56),
        name="moe_combine_ln",
    )(n_groups, chunk_ids, d0c, d1c, x, ys, lnp, lnp)


def _token_rows(a, fill):
    pad = TOK_CHUNKS * TOK_CHUNK - N_TOK
    return jnp.pad(a, (0, pad), constant_values=fill).reshape(TOK_CHUNKS, 1, TOK_CHUNK)


def _moe_layer(x, xb, w_router_padded, w_gu, w_down, j, lnp, layer):
    sel, gates = _router(x, w_router_padded, j)
    sel = sel[:, :N_EXPERTS]
    gates = gates[:, :N_EXPERTS]

    cum = jnp.cumsum(sel, axis=0)
    rank = cum - sel
    counts = cum[-1]
    padded = (counts + MOE_PAD - 1) // MOE_PAD * MOE_PAD
    pad_end = jnp.cumsum(padded)
    first = pad_end - padded + (padded - counts) // MOE_SUB * MOE_SUB
    slot = first[None, :] + rank
    chosen = sel > 0
    d0 = jnp.min(jnp.where(chosen, slot, MOE_SLOTS), axis=1).astype(jnp.int32)
    d1 = jnp.max(jnp.where(chosen, slot, -1), axis=1).astype(jnp.int32)
    g0 = jnp.sum(jnp.where(chosen & (slot == d0[:, None]), gates, 0.0), axis=1)
    g1 = jnp.sum(jnp.where(chosen & (slot == d1[:, None]), gates, 0.0), axis=1)

    def tiles(rows_per_tile):
        start = jnp.arange(MOE_SLOTS // rows_per_tile, dtype=jnp.int32) * rows_per_tile
        e = jnp.minimum(jnp.sum(pad_end[None, :] <= start[:, None], axis=1), N_EXPERTS - 1).astype(jnp.int32)
        r0 = jnp.maximum(start - first[e], 0)
        valid = jnp.clip(jnp.minimum(first[e] + counts[e], start + rows_per_tile) - jnp.maximum(first[e], start),
                         0, rows_per_tile)
        return e, r0, valid

    def matmul_tiles(rows_per_tile):
        e, _, valid = tiles(rows_per_tile)
        live_subs = ((valid + MOE_SUB - 1) // MOE_SUB).astype(jnp.int32)
        opens = jnp.concatenate([jnp.ones((1,), bool), (e[1:] != e[:-1]) | (live_subs[:-1] == 0)])
        fresh = ((live_subs > 0) & opens).astype(jnp.int32)
        n_live = (pad_end[-1:] // rows_per_tile).astype(jnp.int32)
        return e, live_subs, fresh, n_live

    sub_e, r0, valid = tiles(MOE_SUB)
    cum_rows = cum.T[sub_e]
    tok_lo = jnp.sum(cum_rows < (r0 + 1)[:, None], axis=1)
    tok_hi = jnp.sum(cum_rows < (r0 + valid)[:, None], axis=1)
    chunk_lo = jnp.where(valid > 0, tok_lo // TOK_CHUNK, 1).astype(jnp.int32)
    chunk_hi = jnp.where(valid > 0, tok_hi // TOK_CHUNK, 0).astype(jnp.int32)

    rank_edges = jnp.concatenate([jnp.zeros((1, N_EXPERTS), cum.dtype), cum[BM - 1::BM]], axis=0)
    lo = first[None, :] + rank_edges[:-1]
    hi = first[None, :] + rank_edges[1:]
    per_expert = COMBINE_KMAX // N_EXPERTS
    cand = (lo // MOE_SUB)[:, :, None] + jnp.arange(per_expert)[None, None, :]
    ok = (hi > lo)[:, :, None] & (cand <= ((hi - 1) // MOE_SUB)[:, :, None])
    cand = cand.reshape(N_TILES, COMBINE_KMAX)
    ok = ok.reshape(N_TILES, COMBINE_KMAX)
    order = jnp.argsort(jnp.logical_not(ok), axis=1, stable=True)
    chunk_ids = jnp.where(jnp.take_along_axis(ok, order, axis=1), jnp.take_along_axis(cand, order, axis=1), -1)
    chunk_ids = chunk_ids.astype(jnp.int32).reshape(-1)
    n_groups = ((jnp.sum(ok, axis=1) + COMBINE_GROUP - 1) // COMBINE_GROUP).astype(jnp.int32)

    xs, gs = _dispatch(xb, chunk_lo, chunk_hi, _token_rows(d0, -1), _token_rows(d1, -1),
                       _token_rows(g0, 0.0), _token_rows(g1, 0.0))
    h = _moe_gu(xs, w_gu, j, *matmul_tiles(MOE_GU_BM))
    ys = _moe_down(h, w_down, j, gs, *matmul_tiles(MOE_DOWN_BM))
    return _combine_ln(x, ys, n_groups, chunk_ids, d0[:, None], d1[:, None], lnp, layer)


def _block_diag_causal(w_s):
    small = jnp.tril(w_s[:, :DEC_SEQ, :DEC_SEQ])
    eye = jnp.eye(DEC_BATCH, dtype=w_s.dtype)
    return jnp.einsum("ab,gij->gaibj", eye, small).reshape(SGU_GROUPS, CHUNK, CHUNK)


def kernel(x_prompt, x_sample, state_swa_k, state_swa_v, state_conv, ln1_g, ln1_b, ln2_g, ln2_b, a_w_in, a_ln_g, a_ln_b, a_w_s, a_b_s, a_w_out, b_w_qkv, b_sinks, b_w_o, c_w_in, c_w_conv, c_w_out, f_w_gu, f_w_down, m_w_router, m_w_gu, m_w_down):
    x = jnp.concatenate([x_prompt.reshape(N_PROMPT, D_MODEL), x_sample.reshape(N_SAMPLE, D_MODEL)], axis=0)
    xb = x.astype(BF16)
    lnp = jnp.concatenate([ln1_g, ln1_b, ln2_g, ln2_b], axis=0).reshape(4 * DEPTH, 1, D_MODEL)
    a_lnp = jnp.concatenate([a_ln_g, a_ln_b], axis=0).reshape(-1, 1, D_MODEL)
    a_w_in_b, a_w_out_b = a_w_in.astype(BF16), a_w_out.astype(BF16)
    b_w_qkv_b, b_w_o_b = b_w_qkv.astype(BF16), b_w_o.astype(BF16)
    c_w_in_b, c_w_out_b = c_w_in.astype(BF16), c_w_out.astype(BF16)
    f_w_down_b = f_w_down.astype(BF16)
    w_router_padded = jnp.pad(m_w_router, ((0, 0), (0, 0), (0, 128 - N_EXPERTS)))

    sgu_v, swa_states, conv_states = [], [], []
    for i in range(DEPTH):
        kind, j = i % N_MIXERS, i // N_MIXERS
        if kind == 0:
            z2 = _sgu_in(xb, a_w_in_b, a_lnp, j)
            sgu_v.append(z2[N_PROMPT:, D_MODEL:].reshape(DEC_BATCH, DEC_SEQ, D_MODEL))
            b_t = a_b_s[j].T
            b_t_sample = jnp.tile(b_t[:DEC_SEQ], (DEC_BATCH, 1))
            x, xb = _sgu_out(z2, a_w_s, _block_diag_causal(a_w_s[j]), b_t, b_t_sample, a_w_out_b, j, x, lnp, i)
        elif kind == 1:
            x, xb, st = _swa_mixer(xb, x, state_swa_k[j], state_swa_v[j], b_w_qkv_b, b_sinks[j], b_w_o_b, j, lnp, i)
            swa_states.append(st)
        else:
            x, xb, st = _conv_mixer(xb, x, state_conv[j], c_w_in_b, c_w_conv, c_w_out_b, j, lnp, i)
            conv_states.append(st)
        if i % 2 == 0:
            h = _dense_gu(xb, f_w_gu, i // 2)
            x, xb = _mm_ln(h, f_w_down_b, i // 2, x, lnp, 1, i, 320, "dense_down")
        else:
            x, xb = _moe_layer(x, xb, w_router_padded, m_w_gu, m_w_down, i // 2, lnp, i)

    y_prompt = x[:N_PROMPT].reshape(BATCH, SEQ, D_MODEL)
    y_sample = x[N_PROMPT:].reshape(DEC_BATCH, DEC_SEQ, D_MODEL)
    return (y_prompt, y_sample, jnp.stack(sgu_v),
            jnp.stack([s[0] for s in swa_states]), jnp.stack([s[1] for s in swa_states]),
            jnp.stack([s[2] for s in swa_states]), jnp.stack([s[3] for s in swa_states]),
            jnp.stack([s[0] for s in conv_states]), jnp.stack([s[1] for s in conv_states]))
```

```python
import functools

import numpy as np
import jax
import jax.numpy as jnp
from jax import lax
from jax.experimental import pallas as pl
from jax.experimental.pallas import tpu as pltpu

D_MODEL = 2048
BATCH = 2
SEQ = 4096
DEPTH = 4
DEC_BATCH = 32
DEC_SEQ = 4
N_MIXERS = 3
CHUNK = 128
SGU_GROUPS = 8
SGU_GROUP_DIM = D_MODEL // SGU_GROUPS
HEAD_DIM = 64
N_HEADS = D_MODEL // HEAD_DIM
N_KV_HEADS = N_HEADS // 8
GQA_GROUP = N_HEADS // N_KV_HEADS
WINDOW = 128
CONV_WIDTH = 3
D_FF = 5632
N_EXPERTS = 8
LN_EPS = 1e-5
DEEPNORM_ALPHA = (2.0 * DEPTH) ** 0.25
NEG_INF = -1e30

N_PROMPT = BATCH * SEQ
N_SAMPLE = DEC_BATCH * DEC_SEQ
N_TOK = N_PROMPT + N_SAMPLE
BM = 640
N_TILES = N_TOK // BM
SGU_BM = 128
SGU_TILES = N_TOK // SGU_BM

MOE_SUB = 256
MOE_PAD = 4 * MOE_SUB
MOE_SLOTS = (-(-(2 * N_TOK) // MOE_PAD) + N_EXPERTS) * MOE_PAD
MOE_GU_BM = MOE_PAD
MOE_DOWN_BM = 2 * MOE_SUB
MOE_BN = 512
COMBINE_GROUP = 4
TOK_CHUNK = 512
TOK_CHUNKS_FULL = N_TOK // TOK_CHUNK
TOK_TAIL = N_TOK - TOK_CHUNKS_FULL * TOK_CHUNK
TOK_CHUNKS = TOK_CHUNKS_FULL + 1
COMBINE_KMAX = N_EXPERTS * ((MOE_SUB - 1 + BM - 1) // MOE_SUB + 1)

F32 = jnp.float32
BF16 = jnp.bfloat16
MIB = 1024 * 1024


def _params(semantics, vmem_mib):
    return pltpu.CompilerParams(dimension_semantics=semantics, vmem_limit_bytes=vmem_mib * MIB)


def _layer_norm(y, g, b):
    mu = jnp.mean(y, axis=-1, keepdims=True)
    d = y - mu
    var = jnp.mean(d * d, axis=-1, keepdims=True)
    return d * lax.rsqrt(var + LN_EPS) * g + b


def _gelu(x):
    return 0.5 * x * (1.0 + lax.erf(x * np.float32(np.sqrt(0.5))))


def _dot(a, b):
    return jnp.dot(a, b, preferred_element_type=F32)


def _ln_specs(norm, layer, ngrid):
    def spec(kind):
        r = (2 * norm + kind) * DEPTH + layer
        if ngrid == 1:
            return pl.BlockSpec((None, 1, D_MODEL), lambda i: (r, 0, 0))
        return pl.BlockSpec((None, 1, D_MODEL), lambda j, i: (r, 0, 0))
    return [spec(0), spec(1)]


def _mm_kernel(x_ref, w_ref, o_ref):
    o_ref[...] = _dot(x_ref[...], w_ref[...]).astype(o_ref.dtype)


def _mm(x, w, layer, bn, out_dtype, name):
    m, k = x.shape
    n = w.shape[-1]
    return pl.pallas_call(
        _mm_kernel,
        grid=(n // bn, m // BM),
        in_specs=[pl.BlockSpec((BM, k), lambda j, i: (i, 0)),
                  pl.BlockSpec((None, k, bn), lambda j, i: (layer, 0, j))],
        out_specs=pl.BlockSpec((BM, bn), lambda j, i: (i, j)),
        out_shape=jax.ShapeDtypeStruct((m, n), out_dtype),
        compiler_params=_params(("arbitrary", "arbitrary"), 48),
        name=name,
    )(x, w)


def _mm_ln_kernel(x_ref, w_ref, r_ref, g_ref, b_ref, of_ref, ob_ref):
    bm = x_ref.shape[0]
    group = bm // 4 if x_ref.shape[1] <= D_MODEL else bm // 2
    for start in range(0, bm, group):
        rows = slice(start, start + group)
        f = _dot(x_ref[rows, :], w_ref[...])
        y = _layer_norm(DEEPNORM_ALPHA * r_ref[rows, :] + f, g_ref[...], b_ref[...])
        of_ref[rows, :] = y
        ob_ref[rows, :] = y.astype(BF16)


def _mm_ln(x, w, w_layer, resid, lnp, norm, layer, bm, name):
    m, k = x.shape
    row = lambda i: (i, 0)
    return pl.pallas_call(
        _mm_ln_kernel,
        grid=(m // bm,),
        in_specs=[pl.BlockSpec((bm, k), row),
                  pl.BlockSpec((None, k, D_MODEL), lambda i: (w_layer, 0, 0), pipeline_mode=pl.Buffered(1)),
                  pl.BlockSpec((bm, D_MODEL), row)] + _ln_specs(norm, layer, 1),
        out_specs=[pl.BlockSpec((bm, D_MODEL), row), pl.BlockSpec((bm, D_MODEL), row)],
        out_shape=[jax.ShapeDtypeStruct((m, D_MODEL), F32), jax.ShapeDtypeStruct((m, D_MODEL), BF16)],
        compiler_params=_params(("arbitrary",), 56),
        name=name,
    )(x, w, resid, lnp, lnp)


def _sgu_in_kernel(x_ref, w_ref, g_ref, b_ref, o_ref):
    half = BM // 2

    @pl.when(pl.program_id(0) == 0)
    def _():
        for rows in (slice(0, half), slice(half, BM)):
            o_ref[rows, :] = _gelu(_dot(x_ref[rows, :], w_ref[...]))

    @pl.when(pl.program_id(0) == 1)
    def _():
        for rows in (slice(0, half), slice(half, BM)):
            o_ref[rows, :] = _layer_norm(_gelu(_dot(x_ref[rows, :], w_ref[...])), g_ref[...], b_ref[...])


def _sgu_in(xb, w_in, a_lnp, j):
    return pl.pallas_call(
        _sgu_in_kernel,
        grid=(2, N_TILES),
        in_specs=[pl.BlockSpec((BM, D_MODEL), lambda c, i: (i, 0)),
                  pl.BlockSpec((None, D_MODEL, D_MODEL), lambda c, i: (j, 0, c)),
                  pl.BlockSpec((None, 1, D_MODEL), lambda c, i: (j, 0, 0)),
                  pl.BlockSpec((None, 1, D_MODEL), lambda c, i: (2 + j, 0, 0))],
        out_specs=pl.BlockSpec((BM, D_MODEL), lambda c, i: (i, c)),
        out_shape=jax.ShapeDtypeStruct((N_TOK, 2 * D_MODEL), F32),
        compiler_params=_params(("arbitrary", "arbitrary"), 56),
        name="sgu_in",
    )(xb, w_in, a_lnp, a_lnp)


def _sgu_out_kernel(u_ref, v_ref, wp_ref, ws_ref, bp_ref, bs_ref, wo_ref, r_ref, g_ref, b_ref, of_ref, ob_ref):
    is_sample_tile = pl.program_id(0) == SGU_TILES - 1
    row = lax.broadcasted_iota(jnp.int32, (CHUNK, CHUNK), 0)
    col = lax.broadcasted_iota(jnp.int32, (CHUNK, CHUNK), 1)
    causal = row >= col
    groups = range(SGU_GROUPS)
    cols = [slice(grp * SGU_GROUP_DIM, (grp + 1) * SGU_GROUP_DIM) for grp in groups]
    w = [jnp.where(is_sample_tile, ws_ref[grp], jnp.where(causal, wp_ref[grp], 0.0)).astype(BF16) for grp in groups]
    bias = [jnp.where(is_sample_tile, bs_ref[:, grp:grp + 1], bp_ref[:, grp:grp + 1]) for grp in groups]
    mixed = [_dot(w[grp], v_ref[:, cols[grp]].astype(BF16)) + bias[grp] for grp in groups]
    gated = jnp.concatenate([(u_ref[:, cols[grp]] * mixed[grp]).astype(BF16) for grp in groups], axis=1)
    f = _dot(gated, wo_ref[...])
    y = _layer_norm(DEEPNORM_ALPHA * r_ref[...] + f, g_ref[...], b_ref[...])
    of_ref[...] = y
    ob_ref[...] = y.astype(BF16)


def _sgu_out(z2, w_s, w_s_sample, b_t, b_t_sample, w_out, j, resid, lnp, layer):
    row = lambda i: (i, 0)
    const2 = lambda i: (0, 0)
    const3 = lambda i: (0, 0, 0)
    return pl.pallas_call(
        _sgu_out_kernel,
        grid=(SGU_TILES,),
        in_specs=[pl.BlockSpec((SGU_BM, D_MODEL), lambda i: (i, 0)),
                  pl.BlockSpec((SGU_BM, D_MODEL), lambda i: (i, 1)),
                  pl.BlockSpec((None, SGU_GROUPS, CHUNK, CHUNK), lambda i: (j, 0, 0, 0)),
                  pl.BlockSpec((SGU_GROUPS, CHUNK, CHUNK), const3),
                  pl.BlockSpec((CHUNK, SGU_GROUPS), const2),
                  pl.BlockSpec((CHUNK, SGU_GROUPS), const2),
                  pl.BlockSpec((None, D_MODEL, D_MODEL), lambda i: (j, 0, 0), pipeline_mode=pl.Buffered(1)),
                  pl.BlockSpec((SGU_BM, D_MODEL), row)] + _ln_specs(0, layer, 1),
        out_specs=[pl.BlockSpec((SGU_BM, D_MODEL), row), pl.BlockSpec((SGU_BM, D_MODEL), row)],
        out_shape=[jax.ShapeDtypeStruct((N_TOK, D_MODEL), F32), jax.ShapeDtypeStruct((N_TOK, D_MODEL), BF16)],
        compiler_params=_params(("arbitrary",), 40),
        name="sgu_out",
    )(z2, z2, w_s, w_s_sample, b_t, b_t_sample, w_out, resid, lnp, lnp)


def _gu_kernel(x_ref, wg_ref, wu_ref, o_ref, wgb_ref, wub_ref):
    def run(weights):
        wg, wu = weights()
        x = x_ref[...]
        o_ref[...] = (jax.nn.silu(_dot(x, wg)) * _dot(x, wu)).astype(o_ref.dtype)

    def load_weights():
        wg = wg_ref[...].astype(BF16)
        wu = wu_ref[...].astype(BF16)
        wgb_ref[...] = wg
        wub_ref[...] = wu
        return wg, wu

    def kept_weights():
        return wgb_ref[...], wub_ref[...]

    first = pl.program_id(1) == 0
    pl.when(first)(functools.partial(run, load_weights))
    pl.when(jnp.logical_not(first))(functools.partial(run, kept_weights))


def _dense_gu(xb, w_gu, layer):
    bn = 512
    nt = D_FF // bn
    return pl.pallas_call(
        _gu_kernel,
        grid=(nt, N_TILES),
        in_specs=[pl.BlockSpec((BM, D_MODEL), lambda j, i: (i, 0)),
                  pl.BlockSpec((None, D_MODEL, bn), lambda j, i: (layer, 0, j)),
                  pl.BlockSpec((None, D_MODEL, bn), lambda j, i: (layer, 0, j + nt))],
        out_specs=pl.BlockSpec((BM, bn), lambda j, i: (i, j)),
        out_shape=jax.ShapeDtypeStruct((N_TOK, D_FF), BF16),
        scratch_shapes=[pltpu.VMEM((D_MODEL, bn), BF16), pltpu.VMEM((D_MODEL, bn), BF16)],
        compiler_params=_params(("arbitrary", "arbitrary"), 40),
        name="dense_gu",
    )(xb, w_gu, w_gu)


def _attn_kernel(sink_ref, q_ref, kp_ref, vp_ref, kc_ref, vc_ref, o_ref, *, blocks_per_seq):
    has_prev = (pl.program_id(0) % blocks_per_seq) != 0
    qi = lax.broadcasted_iota(jnp.int32, (WINDOW, 2 * WINDOW), 0)
    kj = lax.broadcasted_iota(jnp.int32, (WINDOW, 2 * WINDOW), 1)
    delta_i = WINDOW + qi - kj
    valid = (delta_i >= 0) & (delta_i <= WINDOW) & (has_prev | (kj >= WINDOW))
    delta = delta_i.astype(F32)
    low_half = lax.broadcasted_iota(jnp.int32, (2 * WINDOW, 128), 1) < HEAD_DIM

    kk = jnp.concatenate([kp_ref[...], kc_ref[...]], axis=0)
    vv = jnp.concatenate([vp_ref[...], vc_ref[...]], axis=0)

    def halves(slab, kv):
        if kv % 2 == 0:
            lo = jnp.where(low_half, slab, 0.0)
            hi = jnp.where(low_half, 0.0, pltpu.roll(slab, HEAD_DIM, 1))
        else:
            lo = jnp.where(low_half, pltpu.roll(slab, HEAD_DIM, 1), 0.0)
            hi = jnp.where(low_half, 0.0, slab)
        return lo.astype(BF16), hi.astype(BF16)

    for kv in range(N_KV_HEADS):
        lanes = slice((kv // 2) * 128, (kv // 2 + 1) * 128)
        k_both = jnp.concatenate(halves(kk[:, lanes], kv), axis=0)
        v_both = jnp.concatenate(halves(vv[:, lanes], kv), axis=0)
        pairs = range(GQA_GROUP // 2)
        heads = range(GQA_GROUP)
        qcols = [slice((kv * 4 + pair) * 128, (kv * 4 + pair + 1) * 128) for pair in pairs]
        scores = [lax.dot_general(q_ref[:, qcols[pair]].astype(BF16), k_both, (((1,), (1,)), ((), ())),
                                  preferred_element_type=F32) * (HEAD_DIM ** -0.5) for pair in pairs]
        slopes = [float(2.0 ** (-8.0 * (kv * GQA_GROUP + g + 1) / N_HEADS)) for g in heads]
        sinks = [sink_ref[kv * GQA_GROUP + g] for g in heads]
        s = [scores[g // 2][:, (g % 2) * 2 * WINDOW:(g % 2 + 1) * 2 * WINDOW] - slopes[g] * delta for g in heads]
        s = [jnp.where(valid, s[g], NEG_INF) for g in heads]
        mx = [jnp.maximum(jnp.max(s[g], axis=-1, keepdims=True), sinks[g]) for g in heads]
        p = [jnp.exp(s[g] - mx[g]) for g in heads]
        den = [jnp.sum(p[g], axis=-1, keepdims=True) + jnp.exp(sinks[g] - mx[g]) for g in heads]
        p = [(p[g] / den[g]).astype(BF16) for g in heads]
        for pair in pairs:
            both = jnp.concatenate([p[2 * pair], p[2 * pair + 1]], axis=1)
            o_ref[:, qcols[pair]] = _dot(both, v_both).astype(o_ref.dtype)


def _attention(sinks, q, kp, vp, kc, vc, n_blocks, specs, blocks_per_seq, name):
    return pl.pallas_call(
        functools.partial(_attn_kernel, blocks_per_seq=blocks_per_seq),
        grid=(n_blocks,),
        in_specs=[pl.BlockSpec(memory_space=pltpu.SMEM)] + specs,
        out_specs=pl.BlockSpec((WINDOW, D_MODEL), lambda i: (i, 0)),
        out_shape=jax.ShapeDtypeStruct((n_blocks * WINDOW, D_MODEL), BF16),
        compiler_params=_params(("arbitrary",), 32),
        name=name,
    )(sinks, q, kp, vp, kc, vc)


SAMPLE_ROWS = GQA_GROUP * DEC_SEQ
SAMPLE_KEYS = 2 * WINDOW


def _attn_sample_kernel(q_ref, k_ref, v_ref, sink_ref, slope_ref, o_ref):
    step = lax.broadcasted_iota(jnp.int32, (SAMPLE_ROWS, SAMPLE_KEYS), 0) % DEC_SEQ
    kj = lax.broadcasted_iota(jnp.int32, (SAMPLE_ROWS, SAMPLE_KEYS), 1)
    delta_i = WINDOW + step - kj
    valid = (delta_i >= 0) & (delta_i <= WINDOW)
    delta = delta_i.astype(F32)
    kvs = range(N_KV_HEADS)
    lanes = [slice((kv // 2) * 128, (kv // 2 + 1) * 128) for kv in kvs]
    s = [lax.dot_general(q_ref[kv].astype(BF16), k_ref[:, lanes[kv]].astype(BF16), (((1,), (1,)), ((), ())),
                         preferred_element_type=F32) * (HEAD_DIM ** -0.5) for kv in kvs]
    s = [jnp.where(valid, s[kv] - slope_ref[kv] * delta, NEG_INF) for kv in kvs]
    mx = [jnp.maximum(jnp.max(s[kv], axis=-1, keepdims=True), sink_ref[kv]) for kv in kvs]
    p = [jnp.exp(s[kv] - mx[kv]) for kv in kvs]
    den = [jnp.sum(p[kv], axis=-1, keepdims=True) + jnp.exp(sink_ref[kv] - mx[kv]) for kv in kvs]
    p = [(p[kv] / den[kv]).astype(BF16) for kv in kvs]
    for kv in kvs:
        o_ref[kv] = _dot(p[kv], v_ref[:, lanes[kv]].astype(BF16)).astype(o_ref.dtype)


def _attn_sample(q_s, k_new, v_new, state_k, state_v, sinks):
    kvw = N_KV_HEADS * HEAD_DIM
    q = q_s.reshape(DEC_BATCH, DEC_SEQ, N_KV_HEADS, GQA_GROUP, HEAD_DIM).transpose(0, 2, 3, 1, 4)
    q = q.reshape(DEC_BATCH, N_KV_HEADS, SAMPLE_ROWS, HEAD_DIM)
    zero = jnp.zeros_like(q)
    odd = (jnp.arange(N_KV_HEADS) % 2 == 1)[None, :, None, None]
    q = jnp.where(odd, jnp.concatenate([zero, q], axis=-1), jnp.concatenate([q, zero], axis=-1))

    def keys(state, new):
        pad = jnp.zeros((DEC_BATCH, SAMPLE_KEYS - WINDOW - DEC_SEQ, kvw), F32)
        return jnp.concatenate([state.reshape(DEC_BATCH, WINDOW, kvw), new, pad], axis=1)

    head = np.arange(N_KV_HEADS)[:, None] * GQA_GROUP + np.arange(SAMPLE_ROWS)[None, :] // DEC_SEQ
    slopes = jnp.asarray(np.exp2(-8.0 * (head + 1) / N_HEADS), F32)[:, :, None]
    sink_rows = sinks[head][:, :, None]
    seq3 = lambda i: (i, 0, 0)
    whole = lambda i: (0, 0, 0)
    out = pl.pallas_call(
        _attn_sample_kernel,
        grid=(DEC_BATCH,),
        in_specs=[pl.BlockSpec((None, N_KV_HEADS, SAMPLE_ROWS, 128), lambda i: (i, 0, 0, 0)),
                  pl.BlockSpec((None, SAMPLE_KEYS, kvw), seq3),
                  pl.BlockSpec((None, SAMPLE_KEYS, kvw), seq3),
                  pl.BlockSpec((N_KV_HEADS, SAMPLE_ROWS, 1), whole),
                  pl.BlockSpec((N_KV_HEADS, SAMPLE_ROWS, 1), whole)],
        out_specs=pl.BlockSpec((None, N_KV_HEADS, SAMPLE_ROWS, 128), lambda i: (i, 0, 0, 0)),
        out_shape=jax.ShapeDtypeStruct((DEC_BATCH, N_KV_HEADS, SAMPLE_ROWS, 128), BF16),
        compiler_params=_params(("arbitrary",), 16),
        name="attn_sample",
    )(q, keys(state_k, k_new), keys(state_v, v_new), sink_rows, slopes)
    out = jnp.where(odd, out[..., HEAD_DIM:], out[..., :HEAD_DIM])
    out = out.reshape(DEC_BATCH, N_KV_HEADS, GQA_GROUP, DEC_SEQ, HEAD_DIM).transpose(0, 3, 1, 2, 4)
    return out.reshape(N_SAMPLE, D_MODEL)


def _swa_mixer(xb, x, state_k, state_v, w_qkv, sinks, w_o, j, lnp, layer):
    kvw = N_KV_HEADS * HEAD_DIM
    qkv = _mm(xb, w_qkv, j, 1280, F32, "qkv")
    kcol, vcol = D_MODEL // kvw, D_MODEL // kvw + 1
    nb = SEQ // WINDOW

    def prev_block(i):
        return jnp.maximum(i - 1, 0)

    prompt_specs = [pl.BlockSpec((WINDOW, D_MODEL), lambda i: (i, 0)),
                    pl.BlockSpec((WINDOW, kvw), lambda i: (prev_block(i), kcol)),
                    pl.BlockSpec((WINDOW, kvw), lambda i: (prev_block(i), vcol)),
                    pl.BlockSpec((WINDOW, kvw), lambda i: (i, kcol)),
                    pl.BlockSpec((WINDOW, kvw), lambda i: (i, vcol))]
    o_prompt = _attention(sinks, qkv, qkv, qkv, qkv, qkv, N_PROMPT // WINDOW, prompt_specs, nb, "attn_prompt")

    qkv_s = qkv[N_PROMPT:].reshape(DEC_BATCH, DEC_SEQ, D_MODEL + 2 * kvw)
    k_new = qkv_s[..., D_MODEL:D_MODEL + kvw]
    v_new = qkv_s[..., D_MODEL + kvw:]
    o_sample = _attn_sample(qkv_s[..., :D_MODEL], k_new, v_new, state_k, state_v, sinks)
    o_all = jnp.concatenate([o_prompt, o_sample], axis=0)
    x, xb = _mm_ln(o_all, w_o, j, x, lnp, 0, layer, BM, "attn_out")

    k_p = qkv[:N_PROMPT, D_MODEL:D_MODEL + kvw].reshape(BATCH, SEQ, N_KV_HEADS, HEAD_DIM)[:, -WINDOW:]
    v_p = qkv[:N_PROMPT, D_MODEL + kvw:].reshape(BATCH, SEQ, N_KV_HEADS, HEAD_DIM)[:, -WINDOW:]
    k_s = jnp.concatenate([state_k, k_new.reshape(DEC_BATCH, DEC_SEQ, N_KV_HEADS, HEAD_DIM)], axis=1)[:, -WINDOW:]
    v_s = jnp.concatenate([state_v, v_new.reshape(DEC_BATCH, DEC_SEQ, N_KV_HEADS, HEAD_DIM)], axis=1)[:, -WINDOW:]
    return x, xb, (k_p, v_p, k_s, v_s)


CONV_BC = 512
SAMPLE_ROW0 = N_PROMPT - (N_TILES - 1) * BM


def _conv_taps(z, z1, z2, gb, taps_ref):
    conv = taps_ref[0:1, :] * z2 + taps_ref[1:2, :] * z1 + taps_ref[2:3, :] * z
    return (gb * conv).astype(BF16)


def _conv_kernel(x_ref, wb_ref, wc_ref, wh_ref, taps_ref, pa_ref, pb_ref, y_ref, z_ref,
                 carry_ref, wbb_ref, wcb_ref, whb_ref):
    i = pl.program_id(1)

    @pl.when(i == 0)
    def _():
        carry_ref[...] = jnp.zeros_like(carry_ref)
        wbb_ref[...] = wb_ref[...].astype(BF16)
        wcb_ref[...] = wc_ref[...].astype(BF16)
        whb_ref[...] = wh_ref[...].astype(BF16)

    x = x_ref[...]
    gate_b = _dot(x, wbb_ref[...])
    z = _dot(x, wcb_ref[...]) * _dot(x, whb_ref[...])
    local = lax.broadcasted_iota(jnp.int32, z.shape, 0)
    pos = (i * BM + local) % SEQ
    c6 = carry_ref[6:7, :]
    c7 = carry_ref[7:8, :]
    back1 = jnp.where(local >= 1, pltpu.roll(z, 1, 0), c7)
    back2 = jnp.where(local >= 2, pltpu.roll(z, 2, 0), jnp.where(local == 1, c7, c6))
    z1 = jnp.where(pos >= 1, back1, 0.0)
    z2 = jnp.where(pos >= 2, back2, 0.0)
    y_ref[...] = _conv_taps(z, z1, z2, gate_b, taps_ref)
    z_ref[...] = z
    carry_ref[...] = z[BM - 8:, :]

    @pl.when(i == N_TILES - 1)
    def _():
        zs = z[SAMPLE_ROW0:, :]
        t = lax.broadcasted_iota(jnp.int32, zs.shape, 0) % DEC_SEQ
        z1s = jnp.where(t >= 1, pltpu.roll(zs, 1, 0), pa_ref[...])
        z2s = jnp.where(t >= 2, pltpu.roll(zs, 2, 0), pb_ref[...])
        y_ref[SAMPLE_ROW0:, :] = _conv_taps(zs, z1s, z2s, gate_b[SAMPLE_ROW0:, :], taps_ref)


def _conv_mixer(xb, x, state, w_in, w_conv, w_out, j, lnp, layer):
    zeros = jnp.zeros((DEC_BATCH, DEC_SEQ, D_MODEL), F32)
    past_a = zeros.at[:, 0].set(state[:, 1]).reshape(N_SAMPLE, D_MODEL)
    past_b = zeros.at[:, 0].set(state[:, 0]).at[:, 1].set(state[:, 1]).reshape(N_SAMPLE, D_MODEL)
    ncb = D_MODEL // CONV_BC
    tile = lambda c, i: (i, c)
    y, z = pl.pallas_call(
        _conv_kernel,
        grid=(ncb, N_TILES),
        in_specs=[pl.BlockSpec((BM, D_MODEL), lambda c, i: (i, 0)),
                  pl.BlockSpec((None, D_MODEL, CONV_BC), lambda c, i: (j, 0, c)),
                  pl.BlockSpec((None, D_MODEL, CONV_BC), lambda c, i: (j, 0, ncb + c)),
                  pl.BlockSpec((None, D_MODEL, CONV_BC), lambda c, i: (j, 0, 2 * ncb + c)),
                  pl.BlockSpec((None, CONV_WIDTH, CONV_BC), lambda c, i: (j, 0, c)),
                  pl.BlockSpec((N_SAMPLE, CONV_BC), lambda c, i: (0, c)),
                  pl.BlockSpec((N_SAMPLE, CONV_BC), lambda c, i: (0, c))],
        out_specs=[pl.BlockSpec((BM, CONV_BC), tile), pl.BlockSpec((BM, CONV_BC), tile)],
        out_shape=[jax.ShapeDtypeStruct((N_TOK, D_MODEL), BF16), jax.ShapeDtypeStruct((N_TOK, D_MODEL), F32)],
        scratch_shapes=[pltpu.VMEM((8, CONV_BC), F32)] + [pltpu.VMEM((D_MODEL, CONV_BC), BF16)] * 3,
        compiler_params=_params(("arbitrary", "arbitrary"), 48),
        name="conv_in",
    )(xb, w_in, w_in, w_in, w_conv, past_a, past_b)
    x, xb = _mm_ln(y, w_out, j, x, lnp, 0, layer, BM, "conv_out")
    conv_p = z[:N_PROMPT].reshape(BATCH, SEQ, D_MODEL)[:, -(CONV_WIDTH - 1):]
    conv_s = z[N_PROMPT:].reshape(DEC_BATCH, DEC_SEQ, D_MODEL)[:, -(CONV_WIDTH - 1):]
    return x, xb, (conv_p, conv_s)


def _router_kernel(x_ref, w_ref, sel_ref, gate_ref):
    logits = jnp.dot(x_ref[...], w_ref[...], preferred_element_type=F32, precision=lax.Precision.HIGHEST)
    lane = lax.broadcasted_iota(jnp.int32, logits.shape, 1)
    neg = jnp.float32(-jnp.inf)
    l1 = jnp.where(lane < N_EXPERTS, logits, neg)
    m1 = jnp.max(l1, axis=-1, keepdims=True)
    i1 = jnp.min(jnp.where(l1 == m1, lane, 128), axis=-1, keepdims=True)
    l2 = jnp.where(lane == i1, neg, l1)
    m2 = jnp.max(l2, axis=-1, keepdims=True)
    i2 = jnp.min(jnp.where(l2 == m2, lane, 128), axis=-1, keepdims=True)
    e2 = jnp.exp(m2 - m1)
    den = 1.0 + e2
    g1 = 1.0 / den
    g2 = e2 / den
    sel_ref[...] = ((lane == i1) | (lane == i2)).astype(jnp.int32)
    gate_ref[...] = jnp.where(lane == i1, g1, jnp.where(lane == i2, g2, 0.0))


def _router(x, w_router_padded, layer):
    row = lambda i: (i, 0)
    return pl.pallas_call(
        _router_kernel,
        grid=(N_TILES,),
        in_specs=[pl.BlockSpec((BM, D_MODEL), row),
                  pl.BlockSpec((None, D_MODEL, 128), lambda i: (layer, 0, 0))],
        out_specs=[pl.BlockSpec((BM, 128), row), pl.BlockSpec((BM, 128), row)],
        out_shape=[jax.ShapeDtypeStruct((N_TOK, 128), jnp.int32), jax.ShapeDtypeStruct((N_TOK, 128), F32)],
        compiler_params=_params(("arbitrary",), 32),
        name="router",
    )(x, w_router_padded)


def _dispatch_kernel(lo_ref, hi_ref, d0_ref, d1_ref, g0_ref, g1_ref, x_ref, xs_ref, gs_ref, acc_ref, gacc_ref):
    i = pl.program_id(0)
    lo = lo_ref[i]
    hi = hi_ref[i]
    slot = i * MOE_SUB + lax.broadcasted_iota(jnp.int32, (MOE_SUB, TOK_CHUNK), 0)
    acc_ref[...] = jnp.zeros_like(acc_ref)
    gacc_ref[...] = jnp.zeros_like(gacc_ref)

    def chunk(c, width, x_rows):
        s = slot[:, :width]
        m0 = s == d0_ref[c][:, :width]
        m1 = s == d1_ref[c][:, :width]
        onehot = jnp.where(m0 | m1, 1.0, 0.0).astype(BF16)
        acc_ref[...] += _dot(onehot, x_rows)
        picked = jnp.where(m0, g0_ref[c][:, :width], 0.0) + jnp.where(m1, g1_ref[c][:, :width], 0.0)
        gacc_ref[...] += jnp.sum(picked, axis=1, keepdims=True)

    def body(c, carry):
        start = pl.multiple_of(c * TOK_CHUNK, TOK_CHUNK)
        chunk(c, TOK_CHUNK, x_ref[pl.ds(start, TOK_CHUNK), :])
        return carry

    lax.fori_loop(lo, jnp.minimum(hi, TOK_CHUNKS_FULL - 1) + 1, body, 0)

    @pl.when(hi == TOK_CHUNKS_FULL)
    def _():
        chunk(TOK_CHUNKS_FULL, TOK_TAIL, x_ref[TOK_CHUNKS_FULL * TOK_CHUNK:, :])

    xs_ref[...] = acc_ref[...].astype(BF16)
    gs_ref[...] = gacc_ref[...]


def _dispatch(xb, chunk_lo, chunk_hi, d0r, d1r, g0r, g1r):
    n_tiles = MOE_SLOTS // MOE_SUB
    whole3 = lambda i, lo, hi: (0, 0, 0)
    rows = pl.BlockSpec((TOK_CHUNKS, 1, TOK_CHUNK), whole3)
    grid_spec = pltpu.PrefetchScalarGridSpec(
        num_scalar_prefetch=2,
        grid=(n_tiles,),
        in_specs=[rows, rows, rows, rows,
                  pl.BlockSpec((N_TOK, D_MODEL), lambda i, lo, hi: (0, 0))],
        out_specs=[pl.BlockSpec((MOE_SUB, D_MODEL), lambda i, lo, hi: (i, 0)),
                   pl.BlockSpec((MOE_SUB, 1), lambda i, lo, hi: (i, 0))],
        scratch_shapes=[pltpu.VMEM((MOE_SUB, D_MODEL), F32), pltpu.VMEM((MOE_SUB, 1), F32)],
    )
    return pl.pallas_call(
        _dispatch_kernel,
        grid_spec=grid_spec,
        out_shape=[jax.ShapeDtypeStruct((MOE_SLOTS, D_MODEL), BF16), jax.ShapeDtypeStruct((MOE_SLOTS, 1), F32)],
        compiler_params=_params(("arbitrary",), 52),
        name="moe_dispatch",
    )(chunk_lo, chunk_hi, d0r, d1r, g0r, g1r, xb)


def _for_expert_tile(n_sub, live_subs, fresh, o_ref, compute, load_weights, kept_weights):
    total = n_sub * MOE_SUB

    def run(k, weights):
        dead = (n_sub - k) * MOE_SUB
        o_ref[dead:, :] = compute(slice(dead, total), weights())
        if dead:
            o_ref[:dead, :] = jnp.zeros((dead, o_ref.shape[1]), o_ref.dtype)

    for k in range(1, n_sub + 1):
        pl.when((fresh == 1) & (live_subs == k))(functools.partial(run, k, load_weights))
    pl.when((fresh == 0) & (live_subs == n_sub))(functools.partial(run, n_sub, kept_weights))

    @pl.when(live_subs == 0)
    def _():
        o_ref[...] = jnp.zeros_like(o_ref)


def _moe_gu_kernel(be_ref, ls_ref, fr_ref, nt_ref, x_ref, wg_ref, wu_ref, o_ref, wgb_ref, wub_ref):
    i = pl.program_id(1)

    def load_weights():
        wg = wg_ref[...].astype(BF16)
        wu = wu_ref[...].astype(BF16)
        wgb_ref[...] = wg
        wub_ref[...] = wu
        return wg, wu

    def kept_weights():
        return wgb_ref[...], wub_ref[...]

    def compute(rows, weights):
        x = x_ref[rows, :]
        gate = _dot(x, weights[0])
        up = _dot(x, weights[1])
        return (jax.nn.silu(gate) * up).astype(o_ref.dtype)

    _for_expert_tile(MOE_GU_BM // MOE_SUB, ls_ref[i], fr_ref[i], o_ref, compute, load_weights, kept_weights)


def _live_tile(i, nt):
    return jnp.minimum(i, nt[0] - 1)


def _moe_gu(xs, w_gu, layer, block_e, live_subs, fresh, n_live):
    nt = D_FF // MOE_BN
    grid_spec = pltpu.PrefetchScalarGridSpec(
        num_scalar_prefetch=4,
        grid=(nt, MOE_SLOTS // MOE_GU_BM),
        in_specs=[pl.BlockSpec((MOE_GU_BM, D_MODEL), lambda j, i, be, ls, fr, n: (_live_tile(i, n), 0)),
                  pl.BlockSpec((None, None, D_MODEL, MOE_BN),
                               lambda j, i, be, ls, fr, n: (layer, be[_live_tile(i, n)], 0, j)),
                  pl.BlockSpec((None, None, D_MODEL, MOE_BN),
                               lambda j, i, be, ls, fr, n: (layer, be[_live_tile(i, n)], 0, j + nt))],
        out_specs=pl.BlockSpec((MOE_GU_BM, MOE_BN), lambda j, i, be, ls, fr, n: (i, j)),
        scratch_shapes=[pltpu.VMEM((D_MODEL, MOE_BN), BF16), pltpu.VMEM((D_MODEL, MOE_BN), BF16)],
    )
    return pl.pallas_call(
        _moe_gu_kernel,
        grid_spec=grid_spec,
        out_shape=jax.ShapeDtypeStruct((MOE_SLOTS, D_FF), BF16),
        compiler_params=_params(("arbitrary", "arbitrary"), 48),
        name="moe_gu",
    )(block_e, live_subs, fresh, n_live, xs, w_gu, w_gu)


def _moe_down_kernel(be_ref, ls_ref, fr_ref, nt_ref, h_ref, w_ref, gs_ref, o_ref, wb_ref):
    i = pl.program_id(1)

    def load_weights():
        w = w_ref[...].astype(BF16)
        wb_ref[...] = w
        return w

    def kept_weights():
        return wb_ref[...]

    def compute(rows, w):
        return (_dot(h_ref[rows, :], w) * gs_ref[rows, :]).astype(o_ref.dtype)

    _for_expert_tile(MOE_DOWN_BM // MOE_SUB, ls_ref[i], fr_ref[i], o_ref, compute, load_weights, kept_weights)


def _moe_down(h, w_down, layer, gs, block_e, live_subs, fresh, n_live):
    grid_spec = pltpu.PrefetchScalarGridSpec(
        num_scalar_prefetch=4,
        grid=(D_MODEL // MOE_BN, MOE_SLOTS // MOE_DOWN_BM),
        in_specs=[pl.BlockSpec((MOE_DOWN_BM, D_FF), lambda j, i, be, ls, fr, n: (_live_tile(i, n), 0)),
                  pl.BlockSpec((None, None, D_FF, MOE_BN),
                               lambda j, i, be, ls, fr, n: (layer, be[_live_tile(i, n)], 0, j)),
                  pl.BlockSpec((MOE_DOWN_BM, 1), lambda j, i, be, ls, fr, n: (_live_tile(i, n), 0))],
        out_specs=pl.BlockSpec((MOE_DOWN_BM, MOE_BN), lambda j, i, be, ls, fr, n: (i, j)),
        scratch_shapes=[pltpu.VMEM((D_FF, MOE_BN), BF16)],
    )
    return pl.pallas_call(
        _moe_down_kernel,
        grid_spec=grid_spec,
        out_shape=jax.ShapeDtypeStruct((MOE_SLOTS, D_MODEL), BF16),
        compiler_params=_params(("arbitrary", "arbitrary"), 52),
        name="moe_down",
    )(block_e, live_subs, fresh, n_live, h, w_down, gs)


def _combine_ln_kernel(ng_ref, ck_ref, d0_ref, d1_ref, x_ref, ys_ref, g_ref, b_ref, of_ref, ob_ref,
                       buf_ref, sem_ref, acc_ref):
    i = pl.program_id(0)
    n = ng_ref[i]

    def chunk_id(g, c):
        return ck_ref[i * COMBINE_KMAX + g * COMBINE_GROUP + c]

    def copies(g, which):
        out = []
        for c in range(COMBINE_GROUP):
            cid = chunk_id(g, c)
            start = pl.multiple_of(jnp.maximum(cid, 0) * MOE_SUB, MOE_SUB)
            out.append((cid >= 0, pltpu.make_async_copy(
                ys_ref.at[pl.ds(start, MOE_SUB)],
                buf_ref.at[which, pl.ds(c * MOE_SUB, MOE_SUB)],
                sem_ref.at[which])))
        return out

    def start_group(g, which):
        for present, copy in copies(g, which):
            pl.when(present)(copy.start)

    def wait_group(g, which):
        for present, copy in copies(g, which):
            pl.when(present)(copy.wait)

    @pl.when(i == 0)
    def _():
        buf_ref[...] = jnp.zeros_like(buf_ref)

    @pl.when(n > 0)
    def _():
        start_group(0, 0)

    acc_ref[...] = jnp.zeros_like(acc_ref)
    d0 = d0_ref[...]
    d1 = d1_ref[...]
    lane = lax.broadcasted_iota(jnp.int32, (BM, MOE_SUB), 1)

    def body(g, carry):
        which = g % 2
        wait_group(g, which)

        @pl.when(g + 1 < n)
        def _():
            start_group(g + 1, 1 - which)

        pieces = []
        for c in range(COMBINE_GROUP):
            slot = chunk_id(g, c) * MOE_SUB + lane
            pieces.append(jnp.where((d0 == slot) | (d1 == slot), 1.0, 0.0).astype(BF16))
        acc_ref[...] += _dot(jnp.concatenate(pieces, axis=1), buf_ref[which])
        return carry

    lax.fori_loop(0, n, body, 0)
    y = _layer_norm(DEEPNORM_ALPHA * x_ref[...] + acc_ref[...], g_ref[...], b_ref[...])
    of_ref[...] = y
    ob_ref[...] = y.astype(BF16)


def _combine_ln(x, ys, n_groups, chunk_ids, d0c, d1c, lnp, layer):
    row = lambda i, ng, ck: (i, 0)

    def ln_spec(kind):
        r = (2 + kind) * DEPTH + layer
        return pl.BlockSpec((None, 1, D_MODEL), lambda i, ng, ck: (r, 0, 0))

    grid_spec = pltpu.PrefetchScalarGridSpec(
        num_scalar_prefetch=2,
        grid=(N_TILES,),
        in_specs=[pl.BlockSpec((BM, 1), row),
                  pl.BlockSpec((BM, 1), row),
                  pl.BlockSpec((BM, D_MODEL), row),
                  pl.BlockSpec(memory_space=pl.ANY),
                  ln_spec(0), ln_spec(1)],
        out_specs=[pl.BlockSpec((BM, D_MODEL), row), pl.BlockSpec((BM, D_MODEL), row)],
        scratch_shapes=[pltpu.VMEM((2, COMBINE_GROUP * MOE_SUB, D_MODEL), BF16),
                        pltpu.SemaphoreType.DMA((2,)),
                        pltpu.VMEM((BM, D_MODEL), F32)],
    )
    return pl.pallas_call(
        _combine_ln_kernel,
        grid_spec=grid_spec,
        out_shape=[jax.ShapeDtypeStruct((N_TOK, D_MODEL), F32), jax.ShapeDtypeStruct((N_TOK, D_MODEL), BF16)],
        compiler_params=_params(("arbitrary",), 56),
        name="moe_combine_ln",
    )(n_groups, chunk_ids, d0c, d1c, x, ys, lnp, lnp)


def _token_rows(a, fill):
    pad = TOK_CHUNKS * TOK_CHUNK - N_TOK
    return jnp.pad(a, (0, pad), constant_values=fill).reshape(TOK_CHUNKS, 1, TOK_CHUNK)


def _moe_layer(x, xb, w_router_padded, w_gu, w_down, j, lnp, layer):
    sel, gates = _router(x, w_router_padded, j)
    sel = sel[:, :N_EXPERTS]
    gates = gates[:, :N_EXPERTS]

    cum = jnp.cumsum(sel, axis=0)
    rank = cum - sel
    counts = cum[-1]
    padded = (counts + MOE_PAD - 1) // MOE_PAD * MOE_PAD
    pad_end = jnp.cumsum(padded)
    first = pad_end - padded + (padded - counts) // MOE_SUB * MOE_SUB
    slot = first[None, :] + rank
    chosen = sel > 0
    d0 = jnp.min(jnp.where(chosen, slot, MOE_SLOTS), axis=1).astype(jnp.int32)
    d1 = jnp.max(jnp.where(chosen, slot, -1), axis=1).astype(jnp.int32)
    g0 = jnp.sum(jnp.where(chosen & (slot == d0[:, None]), gates, 0.0), axis=1)
    g1 = jnp.sum(jnp.where(chosen & (slot == d1[:, None]), gates, 0.0), axis=1)

    def tiles(rows_per_tile):
        start = jnp.arange(MOE_SLOTS // rows_per_tile, dtype=jnp.int32) * rows_per_tile
        e = jnp.minimum(jnp.sum(pad_end[None, :] <= start[:, None], axis=1), N_EXPERTS - 1).astype(jnp.int32)
        r0 = jnp.maximum(start - first[e], 0)
        valid = jnp.clip(jnp.minimum(first[e] + counts[e], start + rows_per_tile) - jnp.maximum(first[e], start),
                         0, rows_per_tile)
        return e, r0, valid

    def matmul_tiles(rows_per_tile):
        e, _, valid = tiles(rows_per_tile)
        live_subs = ((valid + MOE_SUB - 1) // MOE_SUB).astype(jnp.int32)
        opens = jnp.concatenate([jnp.ones((1,), bool), (e[1:] != e[:-1]) | (live_subs[:-1] == 0)])
        fresh = ((live_subs > 0) & opens).astype(jnp.int32)
        n_live = (pad_end[-1:] // rows_per_tile).astype(jnp.int32)
        return e, live_subs, fresh, n_live

    sub_e, r0, valid = tiles(MOE_SUB)
    cum_rows = cum.T[sub_e]
    tok_lo = jnp.sum(cum_rows < (r0 + 1)[:, None], axis=1)
    tok_hi = jnp.sum(cum_rows < (r0 + valid)[:, None], axis=1)
    chunk_lo = jnp.where(valid > 0, tok_lo // TOK_CHUNK, 1).astype(jnp.int32)
    chunk_hi = jnp.where(valid > 0, tok_hi // TOK_CHUNK, 0).astype(jnp.int32)

    rank_edges = jnp.concatenate([jnp.zeros((1, N_EXPERTS), cum.dtype), cum[BM - 1::BM]], axis=0)
    lo = first[None, :] + rank_edges[:-1]
    hi = first[None, :] + rank_edges[1:]
    per_expert = COMBINE_KMAX // N_EXPERTS
    cand = (lo // MOE_SUB)[:, :, None] + jnp.arange(per_expert)[None, None, :]
    ok = (hi > lo)[:, :, None] & (cand <= ((hi - 1) // MOE_SUB)[:, :, None])
    cand = cand.reshape(N_TILES, COMBINE_KMAX)
    ok = ok.reshape(N_TILES, COMBINE_KMAX)
    order = jnp.argsort(jnp.logical_not(ok), axis=1, stable=True)
    chunk_ids = jnp.where(jnp.take_along_axis(ok, order, axis=1), jnp.take_along_axis(cand, order, axis=1), -1)
    chunk_ids = chunk_ids.astype(jnp.int32).reshape(-1)
    n_groups = ((jnp.sum(ok, axis=1) + COMBINE_GROUP - 1) // COMBINE_GROUP).astype(jnp.int32)

    xs, gs = _dispatch(xb, chunk_lo, chunk_hi, _token_rows(d0, -1), _token_rows(d1, -1),
                       _token_rows(g0, 0.0), _token_rows(g1, 0.0))
    h = _moe_gu(xs, w_gu, j, *matmul_tiles(MOE_GU_BM))
    ys = _moe_down(h, w_down, j, gs, *matmul_tiles(MOE_DOWN_BM))
    return _combine_ln(x, ys, n_groups, chunk_ids, d0[:, None], d1[:, None], lnp, layer)


def _block_diag_causal(w_s):
    small = jnp.tril(w_s[:, :DEC_SEQ, :DEC_SEQ])
    eye = jnp.eye(DEC_BATCH, dtype=w_s.dtype)
    return jnp.einsum("ab,gij->gaibj", eye, small).reshape(SGU_GROUPS, CHUNK, CHUNK)


def kernel(x_prompt, x_sample, state_swa_k, state_swa_v, state_conv, ln1_g, ln1_b, ln2_g, ln2_b, a_w_in, a_ln_g, a_ln_b, a_w_s, a_b_s, a_w_out, b_w_qkv, b_sinks, b_w_o, c_w_in, c_w_conv, c_w_out, f_w_gu, f_w_down, m_w_router, m_w_gu, m_w_down):
    x = jnp.concatenate([x_prompt.reshape(N_PROMPT, D_MODEL), x_sample.reshape(N_SAMPLE, D_MODEL)], axis=0)
    xb = x.astype(BF16)
    lnp = jnp.concatenate([ln1_g, ln1_b, ln2_g, ln2_b], axis=0).reshape(4 * DEPTH, 1, D_MODEL)
    a_lnp = jnp.concatenate([a_ln_g, a_ln_b], axis=0).reshape(-1, 1, D_MODEL)
    a_w_in_b, a_w_out_b = a_w_in.astype(BF16), a_w_out.astype(BF16)
    b_w_qkv_b, b_w_o_b = b_w_qkv.astype(BF16), b_w_o.astype(BF16)
    c_w_out_b = c_w_out.astype(BF16)
    f_w_down_b = f_w_down.astype(BF16)
    w_router_padded = jnp.pad(m_w_router, ((0, 0), (0, 0), (0, 128 - N_EXPERTS)))

    sgu_v, swa_states, conv_states = [], [], []
    for i in range(DEPTH):
        kind, j = i % N_MIXERS, i // N_MIXERS
        if kind == 0:
            z2 = _sgu_in(xb, a_w_in_b, a_lnp, j)
            sgu_v.append(z2[N_PROMPT:, D_MODEL:].reshape(DEC_BATCH, DEC_SEQ, D_MODEL))
            b_t = a_b_s[j].T
            b_t_sample = jnp.tile(b_t[:DEC_SEQ], (DEC_BATCH, 1))
            x, xb = _sgu_out(z2, a_w_s, _block_diag_causal(a_w_s[j]), b_t, b_t_sample, a_w_out_b, j, x, lnp, i)
        elif kind == 1:
            x, xb, st = _swa_mixer(xb, x, state_swa_k[j], state_swa_v[j], b_w_qkv_b, b_sinks[j], b_w_o_b, j, lnp, i)
            swa_states.append(st)
        else:
            x, xb, st = _conv_mixer(xb, x, state_conv[j], c_w_in, c_w_conv, c_w_out_b, j, lnp, i)
            conv_states.append(st)
        if i % 2 == 0:
            h = _dense_gu(xb, f_w_gu, i // 2)
            x, xb = _mm_ln(h, f_w_down_b, i // 2, x, lnp, 1, i, 320, "dense_down")
        else:
            x, xb = _moe_layer(x, xb, w_router_padded, m_w_gu, m_w_down, i // 2, lnp, i)

    y_prompt = x[:N_PROMPT].reshape(BATCH, SEQ, D_MODEL)
    y_sample = x[N_PROMPT:].reshape(DEC_BATCH, DEC_SEQ, D_MODEL)
    return (y_prompt, y_sample, jnp.stack(sgu_v),
            jnp.stack([s[0] for s in swa_states]), jnp.stack([s[1] for s in swa_states]),
            jnp.stack([s[2] for s in swa_states]), jnp.stack([s[3] for s in swa_states]),
            jnp.stack([s[0] for s in conv_states]), jnp.stack([s[1] for s in conv_states]))
```
